```python
import math
import jax, jax.numpy as jnp
from jax import lax
import numpy as np

D_MODEL = 1024
BATCH = 8
SEQ = 2048
DEPTH = 4
DEC_BATCH = 32
DEC_SEQ = 1
PAST_LEN = 16384
PAGE_SIZE = 128

D_MIX = D_MODEL
S5_WIDTH = D_MIX // 4
S5_GROUP = 16
S5_GROUPS = S5_WIDTH // S5_GROUP
S5_STATE = 64
S5_DT_MIN = 1e-3
S5_DT_MAX = 1e-1
MLA_HEADS = 8
MLA_NOPE = 64
MLA_ROPE = 32
MLA_V = 64
MLA_WIDTH = MLA_HEADS * MLA_V
MLA_Q_LORA = D_MODEL // 4
MLA_KV_LORA = D_MODEL // 8
MLA_SCALE = (MLA_NOPE + MLA_ROPE) ** -0.5
ROPE_THETA = 10000.0
GLA_HEADS = 4
GLA_WIDTH = D_MIX - S5_WIDTH - MLA_WIDTH
GLA_DV = GLA_WIDTH // GLA_HEADS
GLA_DK = GLA_DV // 2
GLA_GATE_RANK = 16
GLA_GATE_NORM = 16.0
GLA_CHUNK = 64
D_FF = 11 * D_MODEL // 4
CONV_W = 3
QBLOCK = 128
RMS_EPS = 1e-6
NEG_INF = -1e30
IN_SPLITS = (S5_WIDTH, MLA_Q_LORA, MLA_KV_LORA, MLA_ROPE, GLA_HEADS * GLA_DK, GLA_HEADS * GLA_DK, GLA_WIDTH, GLA_GATE_RANK, GLA_WIDTH)
D_IN = sum(IN_SPLITS)

kernel_name = 'hymba_style_s5_mla_gla_convglu_step'


def _rmsnorm(x, g):
    x32 = x.astype(jnp.float32)
    y = x32 * lax.rsqrt(jnp.mean(x32 * x32, axis=-1, keepdims=True) + RMS_EPS)
    return (y * g.astype(jnp.float32)).astype(x.dtype)


def _rope(x, pos):
    half = MLA_ROPE // 2
    inv = ROPE_THETA ** (-jnp.arange(half, dtype=jnp.float32) / half)
    ang = pos.astype(jnp.float32)[:, None] * inv[None, :]
    cos = jnp.cos(ang)[None, :, None, :]
    sin = jnp.sin(ang)[None, :, None, :]
    x32 = x.astype(jnp.float32)
    x1, x2 = x32[..., :half], x32[..., half:]
    return jnp.concatenate([x1 * cos - x2 * sin, x1 * sin + x2 * cos], axis=-1).astype(x.dtype)


def _s5_combine(e1, e2):
    a1, b1 = e1
    a2, b2 = e2
    return a1 * a2, a2 * b1 + b2


def _s5_mixer(u, h0, a_re, a_im, log_dt, b_re, b_im, c_re, c_im, d, w_glu, b_glu):
    bsz, s = u.shape[:2]
    u32 = u.astype(jnp.float32).reshape(bsz, s, S5_GROUPS, S5_GROUP)
    lam = lax.complex(a_re.astype(jnp.float32), a_im.astype(jnp.float32))
    dt = jnp.exp(log_dt.astype(jnp.float32))[:, None]
    a_bar = jnp.exp(lam * dt)
    b_c = lax.complex(b_re.astype(jnp.float32), b_im.astype(jnp.float32))
    b_bar = ((a_bar - 1.0) / lam)[..., None] * b_c
    bu = jnp.einsum('bsgc,gpc->bsgp', u32.astype(jnp.complex64), b_bar)
    bu = bu.at[:, 0].add(a_bar[None] * h0)
    a_seq = jnp.broadcast_to(a_bar, bu.shape)
    _, h = lax.associative_scan(_s5_combine, (a_seq, bu), axis=1)
    c_c = lax.complex(c_re.astype(jnp.float32), c_im.astype(jnp.float32))
    y = jnp.real(jnp.einsum('bsgp,gcp->bsgc', h, c_c)) + d.astype(jnp.float32).reshape(S5_GROUPS, S5_GROUP) * u32
    y = jax.nn.gelu(y.reshape(bsz, s, S5_WIDTH))
    out = y * jax.nn.sigmoid(y @ w_glu.astype(jnp.float32) + b_glu.astype(jnp.float32))
    return out.astype(u.dtype), h[:, -1]


def _attn_block(q_lat, q_pe, q_pos, ckv, kpe, k_pos):
    s = (jnp.einsum('bqhl,bkl->bhqk', q_lat, ckv, preferred_element_type=jnp.float32)
         + jnp.einsum('bqhr,bkr->bhqk', q_pe, kpe, preferred_element_type=jnp.float32))
    mask = (k_pos[None, :] <= q_pos[:, None])[None, None]
    s = jnp.where(mask, s * MLA_SCALE, NEG_INF)
    p = jax.nn.softmax(s, axis=-1).astype(ckv.dtype)
    return jnp.einsum('bhqk,bkl->bqhl', p, ckv)


def _mla_attend(q_lat, q_pe, q_pos, ckv, kpe, k_pos):
    bsz, sq, h, l = q_lat.shape
    if sq > QBLOCK and sq % QBLOCK == 0:
        nb = sq // QBLOCK
        qb = q_lat.reshape(bsz, nb, QBLOCK, h, l).swapaxes(0, 1)
        pb = q_pe.reshape(bsz, nb, QBLOCK, h, MLA_ROPE).swapaxes(0, 1)
        posb = q_pos.reshape(nb, QBLOCK)
        out = lax.map(lambda a: _attn_block(a[0], a[1], a[2], ckv, kpe, k_pos), (qb, pb, posb))
        return out.swapaxes(0, 1).reshape(bsz, sq, h, l)
    return _attn_block(q_lat, q_pe, q_pos, ckv, kpe, k_pos)


def _mla_mixer(cq, ckv_raw, kpe_raw, pos, past, q_norm_g, w_qb, kv_norm_g, w_uk, w_uv):
    bsz, s = cq.shape[:2]
    q = (_rmsnorm(cq, q_norm_g) @ w_qb).reshape(bsz, s, MLA_HEADS, MLA_NOPE + MLA_ROPE)
    q_lat = jnp.einsum('bshn,lhn->bshl', q[..., :MLA_NOPE], w_uk)
    q_pe = _rope(q[..., MLA_NOPE:], pos)
    ckv = _rmsnorm(ckv_raw, kv_norm_g)
    kpe = _rope(kpe_raw[:, :, None, :], pos)[:, :, 0]
    if past is None:
        keys_ckv, keys_kpe, k_pos = ckv, kpe, pos
    else:
        past_ckv, past_kpe, past_pos = past
        keys_ckv = jnp.concatenate([past_ckv.astype(ckv.dtype), ckv], axis=1)
        keys_kpe = jnp.concatenate([past_kpe.astype(kpe.dtype), kpe], axis=1)
        k_pos = jnp.concatenate([past_pos, pos])
    o_lat = _mla_attend(q_lat, q_pe, pos, keys_ckv, keys_kpe, k_pos)
    o = jnp.einsum('bshl,lhv->bshv', o_lat, w_uv).reshape(bsz, s, MLA_WIDTH)
    return o, ckv, kpe


def _gla_chunked(q, k, v, logg, s0):
    bsz, s, h, dk = q.shape
    dv = v.shape[-1]
    c = math.gcd(s, GLA_CHUNK)
    n = s // c
    q = q.reshape(bsz, n, c, h, dk)
    k = k.reshape(bsz, n, c, h, dk)
    v = v.reshape(bsz, n, c, h, dv)
    b = jnp.cumsum(logg.reshape(bsz, n, c, h, dk), axis=2)
    b_last = b[:, :, -1:]
    qe = q * jnp.exp(b)
    ke = k * jnp.exp(-b)
    causal = jnp.tril(jnp.ones((c, c), dtype=bool))
    att = jnp.where(causal, jnp.einsum('bnihd,bnjhd->bnhij', qe, ke), 0.0)
    o_intra = jnp.einsum('bnhij,bnjhv->bnihv', att, v)
    u = jnp.einsum('bnjhd,bnjhv->bnhdv', k * jnp.exp(b_last - b), v)
    decay = jnp.exp(b_last[:, :, 0])

    def step(state, xs):
        dec, uc = xs
        return dec[..., None] * state + uc, state

    s_fin, s_prev = lax.scan(step, s0, (jnp.moveaxis(decay, 1, 0), jnp.moveaxis(u, 1, 0)))
    o_inter = jnp.einsum('bnihd,nbhdv->bnihv', qe, s_prev)
    return (o_intra + o_inter).reshape(bsz, s, h, dv), s_fin


def _gla_mixer(gq, gk, gv, glow, gr, s0, w_gate, b_gate, norm_g):
    bsz, s = gq.shape[:2]
    logg = jax.nn.log_sigmoid((glow @ w_gate + b_gate).astype(jnp.float32)) / GLA_GATE_NORM
    q = gq.astype(jnp.float32).reshape(bsz, s, GLA_HEADS, GLA_DK) * (GLA_DK ** -0.5)
    k = gk.astype(jnp.float32).reshape(bsz, s, GLA_HEADS, GLA_DK)
    v = gv.astype(jnp.float32).reshape(bsz, s, GLA_HEADS, GLA_DV)
    o, s_fin = _gla_chunked(q, k, v, logg.reshape(bsz, s, GLA_HEADS, GLA_DK), s0)
    o = _rmsnorm(o, norm_g.reshape(GLA_HEADS, GLA_DV)).reshape(bsz, s, GLA_WIDTH)
    out = o * jax.nn.silu(gr.astype(jnp.float32))
    return out.astype(gq.dtype), s_fin


def _conv_glu(x, buf, g, w_in, conv_w, conv_b, w_out):
    s = x.shape[1]
    hv = _rmsnorm(x, g) @ w_in
    val, gate = hv[..., :D_FF], hv[..., D_FF:]
    full = jnp.concatenate([buf.astype(gate.dtype), gate], axis=1)
    conv = conv_b + sum(conv_w[i] * full[:, i:i + s] for i in range(CONV_W))
    out = (jax.nn.gelu(conv) * val) @ w_out
    return out, full[:, -(CONV_W - 1):]


def _layer(x, pos, past, s5_h0, gla_s0, conv_buf, lw):
    h = _rmsnorm(x, lw['norm_mix_g'])
    p = h @ lw['w_in']
    cuts = np.cumsum(IN_SPLITS)[:-1].tolist()
    u_s5, cq, ckv_raw, kpe_raw, gq, gk, gv, glow, gr = jnp.split(p, cuts, axis=-1)
    o_s5, s5_h = _s5_mixer(u_s5, s5_h0, lw['s5_a_re'], lw['s5_a_im'], lw['s5_log_dt'], lw['s5_b_re'], lw['s5_b_im'],
                           lw['s5_c_re'], lw['s5_c_im'], lw['s5_d'], lw['s5_w_glu'], lw['s5_b_glu'])
    o_mla, ckv, kpe = _mla_mixer(cq, ckv_raw, kpe_raw, pos, past, lw['mla_q_norm_g'], lw['mla_w_qb'],
                                 lw['mla_kv_norm_g'], lw['mla_w_uk'], lw['mla_w_uv'])
    o_gla, gla_s = _gla_mixer(gq, gk, gv, glow, gr, gla_s0, lw['gla_w_gate'], lw['gla_b_gate'], lw['gla_norm_g'])
    mix = jnp.concatenate([o_s5, o_mla.astype(x.dtype), o_gla], axis=-1) @ lw['w_out']
    x = x + mix
    f, new_buf = _conv_glu(x, conv_buf, lw['norm_ffn_g'], lw['w_ffn_in'], lw['ffn_conv_w'], lw['ffn_conv_b'], lw['w_ffn_out'])
    x = x + f
    s5_out = jnp.stack([jnp.real(s5_h), jnp.imag(s5_h)], axis=-1)
    return x, ckv, kpe, s5_out, gla_s, new_buf


def setup_inputs(seed: int = 0) -> dict:
    key = jax.random.key(seed)
    ks = jax.random.split(key, 36)
    f32 = jnp.float32
    n_pages = PAST_LEN // PAGE_SIZE
    n_pool = (DEC_BATCH * n_pages * 5) // 4

    def nrm(i, shape, scale=1.0):
        return jax.random.normal(ks[i], shape, f32) * scale

    def gain(i, shape):
        return 1.0 + nrm(i, shape, 0.05)

    page_table = jax.random.permutation(ks[0], n_pool)[:DEC_BATCH * n_pages].reshape(DEC_BATCH, n_pages).astype(jnp.int32)
    return {
        'x_prompt': nrm(1, (BATCH, SEQ, D_MODEL)),
        'x_sample': nrm(2, (DEC_BATCH, DEC_SEQ, D_MODEL)),
        'cache_mla_ckv': nrm(3, (DEPTH, n_pool, PAGE_SIZE, MLA_KV_LORA)),
        'cache_mla_krope': nrm(4, (DEPTH, n_pool, PAGE_SIZE, MLA_ROPE)),
        'page_table': page_table,
        'state_s5': nrm(5, (DEPTH, DEC_BATCH, S5_GROUPS, S5_STATE, 2), 0.1),
        'state_gla': nrm(6, (DEPTH, DEC_BATCH, GLA_HEADS, GLA_DK, GLA_DV)),
        'state_ffn_conv': nrm(7, (DEPTH, DEC_BATCH, CONV_W - 1, D_FF)),
        'norm_mix_g': gain(8, (DEPTH, D_MODEL)),
        'w_in': nrm(9, (DEPTH, D_MODEL, D_IN), D_MODEL ** -0.5),
        's5_a_re': -0.5 + nrm(10, (DEPTH, S5_GROUPS, S5_STATE), 0.01),
        's5_a_im': math.pi * jnp.arange(S5_STATE, dtype=f32) + nrm(11, (DEPTH, S5_GROUPS, S5_STATE), 0.01),
        's5_log_dt': jax.random.uniform(ks[12], (DEPTH, S5_GROUPS), f32, math.log(S5_DT_MIN), math.log(S5_DT_MAX)),
        's5_b_re': nrm(13, (DEPTH, S5_GROUPS, S5_STATE, S5_GROUP), (2 * S5_GROUP) ** -0.5),
        's5_b_im': nrm(14, (DEPTH, S5_GROUPS, S5_STATE, S5_GROUP), (2 * S5_GROUP) ** -0.5),
        's5_c_re': nrm(15, (DEPTH, S5_GROUPS, S5_GROUP, S5_STATE), S5_STATE ** -0.5),
        's5_c_im': nrm(16, (DEPTH, S5_GROUPS, S5_GROUP, S5_STATE), S5_STATE ** -0.5),
        's5_d': nrm(17, (DEPTH, S5_WIDTH)),
        's5_w_glu': nrm(18, (DEPTH, S5_WIDTH, S5_WIDTH), S5_WIDTH ** -0.5),
        's5_b_glu': nrm(19, (DEPTH, S5_WIDTH), 0.01),
        'mla_q_norm_g': gain(20, (DEPTH, MLA_Q_LORA)),
        'mla_w_qb': nrm(21, (DEPTH, MLA_Q_LORA, MLA_HEADS * (MLA_NOPE + MLA_ROPE)), MLA_Q_LORA ** -0.5),
        'mla_kv_norm_g': gain(22, (DEPTH, MLA_KV_LORA)),
        'mla_w_uk': nrm(23, (DEPTH, MLA_KV_LORA, MLA_HEADS, MLA_NOPE), MLA_KV_LORA ** -0.5),
        'mla_w_uv': nrm(24, (DEPTH, MLA_KV_LORA, MLA_HEADS, MLA_V), MLA_KV_LORA ** -0.5),
        'gla_w_gate': nrm(25, (DEPTH, GLA_GATE_RANK, GLA_HEADS * GLA_DK), GLA_GATE_RANK ** -0.5),
        'gla_b_gate': nrm(26, (DEPTH, GLA_HEADS * GLA_DK), 0.01),
        'gla_norm_g': gain(27, (DEPTH, GLA_WIDTH)),
        'w_out': nrm(28, (DEPTH, D_MIX, D_MODEL), D_MIX ** -0.5),
        'norm_ffn_g': gain(29, (DEPTH, D_MODEL)),
        'w_ffn_in': nrm(30, (DEPTH, D_MODEL, 2 * D_FF), D_MODEL ** -0.5),
        'ffn_conv_w': nrm(31, (DEPTH, CONV_W, D_FF), CONV_W ** -0.5),
        'ffn_conv_b': nrm(32, (DEPTH, D_FF), 0.01),
        'w_ffn_out': nrm(33, (DEPTH, D_FF, D_MODEL), D_FF ** -0.5),
        'norm_final_g': gain(34, (D_MODEL,)),
    }


def reference(x_prompt, x_sample, cache_mla_ckv, cache_mla_krope, page_table, state_s5, state_gla, state_ffn_conv,
              norm_mix_g, w_in, s5_a_re, s5_a_im, s5_log_dt, s5_b_re, s5_b_im, s5_c_re, s5_c_im, s5_d, s5_w_glu, s5_b_glu,
              mla_q_norm_g, mla_w_qb, mla_kv_norm_g, mla_w_uk, mla_w_uv, gla_w_gate, gla_b_gate, gla_norm_g, w_out,
              norm_ffn_g, w_ffn_in, ffn_conv_w, ffn_conv_b, w_ffn_out, norm_final_g):
    bp, sp = x_prompt.shape[:2]
    bs, ss = x_sample.shape[:2]
    past_len = page_table.shape[1] * cache_mla_ckv.shape[2]
    pos_p = jnp.arange(sp)
    pos_s = past_len + jnp.arange(ss)
    past_pos = jnp.arange(past_len)
    xp, xd = x_prompt, x_sample
    ckv_p, kpe_p, s5_p, gla_p, conv_p = [], [], [], [], []
    ckv_s, kpe_s, s5_s, gla_s, conv_s = [], [], [], [], []
    for l in range(DEPTH):
        lw = dict(norm_mix_g=norm_mix_g[l], w_in=w_in[l], s5_a_re=s5_a_re[l], s5_a_im=s5_a_im[l], s5_log_dt=s5_log_dt[l],
                  s5_b_re=s5_b_re[l], s5_b_im=s5_b_im[l], s5_c_re=s5_c_re[l], s5_c_im=s5_c_im[l], s5_d=s5_d[l],
                  s5_w_glu=s5_w_glu[l], s5_b_glu=s5_b_glu[l], mla_q_norm_g=mla_q_norm_g[l], mla_w_qb=mla_w_qb[l],
                  mla_kv_norm_g=mla_kv_norm_g[l], mla_w_uk=mla_w_uk[l], mla_w_uv=mla_w_uv[l], gla_w_gate=gla_w_gate[l],
                  gla_b_gate=gla_b_gate[l], gla_norm_g=gla_norm_g[l], w_out=w_out[l], norm_ffn_g=norm_ffn_g[l],
                  w_ffn_in=w_ffn_in[l], ffn_conv_w=ffn_conv_w[l], ffn_conv_b=ffn_conv_b[l], w_ffn_out=w_ffn_out[l])
        xp, c1, k1, h1, g1, b1 = _layer(
            xp, pos_p, None,
            jnp.zeros((bp, S5_GROUPS, S5_STATE), jnp.complex64),
            jnp.zeros((bp, GLA_HEADS, GLA_DK, GLA_DV), jnp.float32),
            jnp.zeros((bp, CONV_W - 1, D_FF), xp.dtype), lw)
        ckv_p.append(c1.reshape(bp, sp // PAGE_SIZE, PAGE_SIZE, MLA_KV_LORA))
        kpe_p.append(k1.reshape(bp, sp // PAGE_SIZE, PAGE_SIZE, MLA_ROPE))
        s5_p.append(h1)
        gla_p.append(g1)
        conv_p.append(b1)
        past = (cache_mla_ckv[l][page_table].reshape(bs, past_len, MLA_KV_LORA),
                cache_mla_krope[l][page_table].reshape(bs, past_len, MLA_ROPE),
                past_pos)
        h0 = lax.complex(state_s5[l, ..., 0].astype(jnp.float32), state_s5[l, ..., 1].astype(jnp.float32))
        xd, c2, k2, h2, g2, b2 = _layer(xd, pos_s, past, h0, state_gla[l].astype(jnp.float32), state_ffn_conv[l], lw)
        ckv_s.append(c2)
        kpe_s.append(k2)
        s5_s.append(h2)
        gla_s.append(g2)
        conv_s.append(b2)
    y_prompt = _rmsnorm(xp, norm_final_g)
    y_sample = _rmsnorm(xd, norm_final_g)
    return (y_prompt, y_sample,
            jnp.stack(ckv_p), jnp.stack(kpe_p), jnp.stack(s5_p), jnp.stack(gla_p), jnp.stack(conv_p),
            jnp.stack(ckv_s), jnp.stack(kpe_s), jnp.stack(s5_s), jnp.stack(gla_s), jnp.stack(conv_s))
```

```python
import functools
import math

import jax
import jax.numpy as jnp
from jax import lax
from jax.experimental import pallas as pl
from jax.experimental.pallas import tpu as pltpu

F32 = jnp.float32
BF16 = jnp.bfloat16

D_MODEL = 1024
DEPTH = 4
PAGE = 128
S5_W = 256
S5_G = 16
S5_C = 16
S5_P = 64
S5_N = S5_G * S5_P
MLA_H = 8
MLA_NOPE = 64
MLA_ROPE = 32
MLA_V = 64
MLA_W = MLA_H * MLA_V
MLA_QL = 256
MLA_L = 128
MLA_SCALE = (MLA_NOPE + MLA_ROPE) ** -0.5
ROPE_THETA = 10000.0
GLA_H = 4
GLA_W = 256
GLA_DV = 64
GLA_DK = 32
GLA_R = 16
GLA_NORM = 16.0
GLA_CHUNK = 64
D_FF = 2816
RMS_EPS = 1e-6
NEG_INF = -1e30

LANE = 128
PM_W = 640
PG_W = 896
W_EXT = S5_W + PM_W + PG_W
QA_W = MLA_H * MLA_NOPE + 2 * MLA_H * LANE
KC_W = 2 * LANE

FF_CHUNK = 256
FF_NC = D_FF // FF_CHUNK
VMEM_LIMIT = 56 * 1024 * 1024


def _cparams(sem, vmem=VMEM_LIMIT):
    return pltpu.CompilerParams(dimension_semantics=sem, vmem_limit_bytes=vmem)


def _rms(x, g):
    ms = jnp.mean(x * x, axis=-1, keepdims=True)
    return x * lax.rsqrt(ms + RMS_EPS) * g


def _const_spec(shape):
    nd = len(shape)
    return pl.BlockSpec(shape, lambda *_: (0,) * nd)


def _resident_spec(shape):
    nd = len(shape)
    return pl.BlockSpec(shape, lambda *_: (0,) * nd, pipeline_mode=pl.Buffered(1))


def _in_proj_kernel(x_ref, g_ref, w_ref, u_ref, pm_ref, pg_ref):
    h = _rms(x_ref[...], g_ref[...]).astype(BF16)
    u_ref[...] = jnp.dot(h, w_ref[:, :S5_W], preferred_element_type=F32)
    pm_ref[...] = jnp.dot(h, w_ref[:, S5_W:S5_W + PM_W], preferred_element_type=F32)
    pg_ref[...] = jnp.dot(h, w_ref[:, S5_W + PM_W:], preferred_element_type=F32)


def _in_proj(x2d, g, w_ext, *, n_seq, seq_len, tm):
    n = n_seq * seq_len
    nj = seq_len // tm
    row = lambda b, j: (b * nj + j, 0)
    return pl.pallas_call(
        _in_proj_kernel,
        grid=(n_seq, nj),
        in_specs=[pl.BlockSpec((tm, D_MODEL), row), _const_spec((1, D_MODEL)),
                  _const_spec((D_MODEL, W_EXT))],
        out_specs=[pl.BlockSpec((tm, S5_W), lambda b, j: (j, b)),
                   pl.BlockSpec((tm, PM_W), row), pl.BlockSpec((tm, PG_W), row)],
        out_shape=[jax.ShapeDtypeStruct((seq_len, n_seq * S5_W), F32),
                   jax.ShapeDtypeStruct((n, PM_W), F32),
                   jax.ShapeDtypeStruct((n, PG_W), F32)],
        compiler_params=_cparams(("parallel", "parallel")),
        name="in_proj",
    )(x2d, g, w_ext)


def _s5_kernel(u_ref, h0_ref, a_ref, bm_ref, cm_ref, d_ref, wg_ref, bg_ref,
               o_ref, hfin_ref, h_sc, bu_sc, *, r_rows, t_steps):
    @pl.when(pl.program_id(0) == 0)
    def _():
        h_sc[...] = h0_ref[...]

    u = u_ref[...]
    bu_sc[...] = jnp.dot(u.astype(BF16), bm_ref[...], preferred_element_type=F32)
    ar = jnp.broadcast_to(a_ref[0:1, :], (r_rows, S5_N))
    ai = jnp.broadcast_to(a_ref[1:2, :], (r_rows, S5_N))

    def step(t, carry):
        hr, hi = carry
        r0 = pl.multiple_of(t * r_rows, r_rows)
        nr = ar * hr - ai * hi + bu_sc[pl.ds(r0, r_rows), :S5_N]
        ni = ar * hi + ai * hr + bu_sc[pl.ds(r0, r_rows), S5_N:]
        bu_sc[pl.ds(r0, r_rows), :S5_N] = nr
        bu_sc[pl.ds(r0, r_rows), S5_N:] = ni
        return nr, ni

    hr, hi = lax.fori_loop(0, t_steps, step, (h_sc[:, :S5_N], h_sc[:, S5_N:]))
    h_sc[:, :S5_N] = hr
    h_sc[:, S5_N:] = hi
    hfin_ref[:, :S5_N] = hr
    hfin_ref[:, S5_N:] = hi

    y = jnp.dot(bu_sc[...].astype(BF16), cm_ref[...], preferred_element_type=F32) + d_ref[...] * u
    y = jax.nn.gelu(y)
    z = jnp.dot(y.astype(BF16), wg_ref[...], preferred_element_type=F32) + bg_ref[...]
    o_ref[...] = y * jax.nn.sigmoid(z)


def _s5(u_tm, h0, sp, *, r_rows, t_total, t_steps):
    rows = r_rows * t_steps
    kern = functools.partial(_s5_kernel, r_rows=r_rows, t_steps=t_steps)
    return pl.pallas_call(
        kern,
        grid=(t_total // t_steps,),
        in_specs=[pl.BlockSpec((rows, S5_W), lambda i: (i, 0)),
                  _const_spec((r_rows, 2 * S5_N)), _const_spec((2, S5_N)),
                  _const_spec((S5_W, 2 * S5_N)), _const_spec((2 * S5_N, S5_W)),
                  _const_spec((1, S5_W)), _const_spec((S5_W, S5_W)), _const_spec((1, S5_W))],
        out_specs=[pl.BlockSpec((rows, S5_W), lambda i: (i, 0)),
                   _const_spec((r_rows, 2 * S5_N))],
        out_shape=[jax.ShapeDtypeStruct((t_total * r_rows, S5_W), F32),
                   jax.ShapeDtypeStruct((r_rows, 2 * S5_N), F32)],
        scratch_shapes=[pltpu.VMEM((r_rows, 2 * S5_N), F32),
                        pltpu.VMEM((rows, 2 * S5_N), F32)],
        compiler_params=_cparams(("arbitrary",)),
        name="s5",
    )(u_tm, h0, sp["a"], sp["bm"], sp["cm"], sp["d"], sp["wglu"], sp["bglu"])


def _mla_prep_kernel(pm_ref, cos_ref, sin_ref, gq_ref, wq_ref, wuk_ref, gkv_ref,
                     qcat_ref, kcat_ref, ckv_ref, kpe_ref):
    pm = pm_ref[...]
    cos = cos_ref[...]
    sin = sin_ref[...]
    qn = _rms(pm[:, :MLA_QL], gq_ref[...]).astype(BF16)
    qa = jnp.dot(qn, wq_ref[...], preferred_element_type=F32)
    n_nope = MLA_H * MLA_NOPE
    qlat = jnp.dot(qa[:, :n_nope].astype(BF16), wuk_ref[...], preferred_element_type=F32)
    for h in range(MLA_H):
        lo = n_nope + h * LANE
        qpe = qa[:, lo:lo + LANE] * cos + qa[:, lo + MLA_H * LANE:lo + (MLA_H + 1) * LANE] * sin
        qcat_ref[h, :, :LANE] = (qlat[:, h * LANE:(h + 1) * LANE] * MLA_SCALE).astype(qcat_ref.dtype)
        qcat_ref[h, :, LANE:] = (qpe * MLA_SCALE).astype(qcat_ref.dtype)
    ckv = _rms(pm[:, MLA_QL:MLA_QL + MLA_L], gkv_ref[...])
    kpe = pm[:, MLA_QL + MLA_L:MLA_QL + 2 * MLA_L] * cos + pm[:, MLA_QL + 2 * MLA_L:] * sin
    ckv_ref[...] = ckv
    kpe_ref[...] = kpe[:, :MLA_ROPE]
    kcat_ref[:, :LANE] = ckv.astype(kcat_ref.dtype)
    kcat_ref[:, LANE:] = kpe.astype(kcat_ref.dtype)


def _mla_prep(pm, cos_t, sin_t, mp, *, n_seq, seq_len, tm, q_dtype):
    n = n_seq * seq_len
    nj = seq_len // tm
    row = lambda b, j: (b * nj + j, 0)
    pos = lambda b, j: (j, 0)
    return pl.pallas_call(
        _mla_prep_kernel,
        grid=(n_seq, nj),
        in_specs=[pl.BlockSpec((tm, PM_W), row),
                  pl.BlockSpec((tm, LANE), pos), pl.BlockSpec((tm, LANE), pos),
                  _const_spec((1, MLA_QL)), _const_spec((MLA_QL, QA_W)),
                  _const_spec((MLA_H * MLA_NOPE, MLA_H * LANE)), _const_spec((1, MLA_L))],
        out_specs=[pl.BlockSpec((MLA_H, tm, KC_W), lambda b, j: (0, b * nj + j, 0)),
                   pl.BlockSpec((tm, KC_W), row),
                   pl.BlockSpec((tm, MLA_L), row), pl.BlockSpec((tm, MLA_ROPE), row)],
        out_shape=[jax.ShapeDtypeStruct((MLA_H, n, KC_W), q_dtype),
                   jax.ShapeDtypeStruct((n, KC_W), q_dtype),
                   jax.ShapeDtypeStruct((n, MLA_L), F32),
                   jax.ShapeDtypeStruct((n, MLA_ROPE), F32)],
        compiler_params=_cparams(("parallel", "parallel")),
        name="mla_prep",
    )(pm, cos_t, sin_t, mp["gq"], mp["wq"], mp["wuk"], mp["gkv"])


ATT_TQ = 128
ATT_TK = 256


def _attn_kernel(q_ref, k_ref, wuv_ref, o_ref, m_sc, l_sc, acc_sc):
    i = pl.program_id(1)
    rows = MLA_H * ATT_TQ
    q = q_ref[...].reshape(rows, KC_W)
    m_sc[...] = jnp.full((rows, 1), NEG_INF, F32)
    l_sc[...] = jnp.zeros((rows, 1), F32)
    acc_sc[...] = jnp.zeros((rows, MLA_L), F32)

    def tile(j, masked):
        kt = k_ref[pl.ds(pl.multiple_of(j * ATT_TK, ATT_TK), ATT_TK), :]
        s = lax.dot_general(q, kt, (((1,), (1,)), ((), ())), preferred_element_type=F32)
        if masked:
            qpos = i * ATT_TQ + lax.broadcasted_iota(jnp.int32, (ATT_TQ, ATT_TK), 0)
            kpos = j * ATT_TK + lax.broadcasted_iota(jnp.int32, (ATT_TQ, ATT_TK), 1)
            s3 = jnp.where((kpos <= qpos)[None], s.reshape(MLA_H, ATT_TQ, ATT_TK), NEG_INF)
            s = s3.reshape(rows, ATT_TK)
        m_old = m_sc[...]
        m_new = jnp.maximum(m_old, jnp.max(s, axis=-1, keepdims=True))
        alpha = jnp.exp(m_old - m_new)
        p = jnp.exp(s - m_new)
        l_sc[...] = alpha * l_sc[...] + jnp.sum(p, axis=-1, keepdims=True)
        acc_sc[...] = alpha * acc_sc[...] + jnp.dot(p.astype(kt.dtype), kt[:, :MLA_L],
                                                    preferred_element_type=F32)
        m_sc[...] = m_new

    n_full = (i * ATT_TQ) // ATT_TK

    def body(j, c):
        tile(j, False)
        return c

    lax.fori_loop(0, n_full, body, 0)
    tile(n_full, True)
    o_lat = (acc_sc[...] / l_sc[...]).astype(wuv_ref.dtype)
    out = jnp.dot(o_lat[:ATT_TQ], wuv_ref[0], preferred_element_type=F32)
    for h in range(1, MLA_H):
        out += jnp.dot(o_lat[h * ATT_TQ:(h + 1) * ATT_TQ], wuv_ref[h], preferred_element_type=F32)
    o_ref[...] = out


def _attn_prompt(qcat, kcat, wuv, *, n_seq, seq_len):
    nq = seq_len // ATT_TQ
    rows = MLA_H * ATT_TQ
    return pl.pallas_call(
        _attn_kernel,
        grid=(n_seq, nq),
        in_specs=[pl.BlockSpec((MLA_H, ATT_TQ, KC_W), lambda b, i: (0, b * nq + i, 0)),
                  pl.BlockSpec((seq_len, KC_W), lambda b, i: (b, 0)),
                  _const_spec((MLA_H, MLA_L, MLA_W))],
        out_specs=pl.BlockSpec((ATT_TQ, MLA_W), lambda b, i: (b * nq + i, 0)),
        out_shape=jax.ShapeDtypeStruct((n_seq * seq_len, MLA_W), F32),
        scratch_shapes=[pltpu.VMEM((rows, 1), F32), pltpu.VMEM((rows, 1), F32),
                        pltpu.VMEM((rows, MLA_L), F32)],
        compiler_params=_cparams(("parallel", "parallel")),
        name="mla_attn",
    )(qcat, kcat, wuv)


DEC_G = 16


def _page_copies(pt_ref, ckv_hbm, kpe_hbm, ckv_buf, kpe_buf, sems, layer, b, c, slot):
    copies = []
    for g in range(DEC_G):
        page = pt_ref[b, c * DEC_G + g]
        dst = pl.ds(g * PAGE, PAGE)
        copies.append(pltpu.make_async_copy(ckv_hbm.at[layer, page], ckv_buf.at[slot, dst, :],
                                            sems.at[0, slot]))
        copies.append(pltpu.make_async_copy(kpe_hbm.at[layer, page], kpe_buf.at[slot, dst, :],
                                            sems.at[1, slot]))
    return copies


def _decode_kernel(pt_ref, q_ref, knew_ref, ckv_hbm, kpe_hbm, o_ref,
                   ckv_buf, kpe_buf, sems, m_sc, l_sc, acc_sc, *, layer, n_chunks):
    b = pl.program_id(0)
    c = pl.program_id(1)
    step = b * n_chunks + c
    total = pl.num_programs(0) * n_chunks
    slot = lax.rem(step, 2)
    copies = functools.partial(_page_copies, pt_ref, ckv_hbm, kpe_hbm, ckv_buf, kpe_buf, sems, layer)

    @pl.when(step == 0)
    def _():
        for cp in copies(b, c, slot):
            cp.start()

    @pl.when(step + 1 < total)
    def _():
        nxt = step + 1
        for cp in copies(nxt // n_chunks, lax.rem(nxt, n_chunks), 1 - slot):
            cp.start()

    @pl.when(c == 0)
    def _():
        m_sc[...] = jnp.full((MLA_H, 1), NEG_INF, F32)
        l_sc[...] = jnp.zeros((MLA_H, 1), F32)
        acc_sc[...] = jnp.zeros((MLA_H, MLA_L), F32)

    for cp in copies(b, c, slot):
        cp.wait()

    q = q_ref[0]
    nt = (((1,), (1,)), ((), ()))
    ck = ckv_buf[slot].astype(BF16)
    kp = kpe_buf[slot].astype(BF16)
    s = (lax.dot_general(q[:, :MLA_L].astype(BF16), ck, nt, preferred_element_type=F32)
         + lax.dot_general(q[:, MLA_L:MLA_L + MLA_ROPE].astype(BF16), kp, nt,
                           preferred_element_type=F32))
    m_old = m_sc[...]
    m_new = jnp.maximum(m_old, jnp.max(s, axis=-1, keepdims=True))
    alpha = jnp.exp(m_old - m_new)
    p = jnp.exp(s - m_new)
    l_sc[...] = alpha * l_sc[...] + jnp.sum(p, axis=-1, keepdims=True)
    acc_sc[...] = alpha * acc_sc[...] + jnp.dot(p.astype(BF16), ck, preferred_element_type=F32)
    m_sc[...] = m_new

    @pl.when(c == n_chunks - 1)
    def _():
        kn = knew_ref[0]
        s_new = jnp.sum(q * kn, axis=-1, keepdims=True)
        m_old = m_sc[...]
        m_new = jnp.maximum(m_old, s_new)
        alpha = jnp.exp(m_old - m_new)
        p_new = jnp.exp(s_new - m_new)
        l = alpha * l_sc[...] + p_new
        acc = alpha * acc_sc[...] + p_new * kn[:, :MLA_L]
        o_ref[0] = acc / l


def _attn_decode(page_table, q_dec, knew, cache_ckv, cache_kpe, *, layer):
    n_seq, n_pages = page_table.shape
    n_chunks = n_pages // DEC_G
    keys = DEC_G * PAGE
    kern = functools.partial(_decode_kernel, layer=layer, n_chunks=n_chunks)
    grid_spec = pltpu.PrefetchScalarGridSpec(
        num_scalar_prefetch=1,
        grid=(n_seq, n_chunks),
        in_specs=[pl.BlockSpec((1, MLA_H, KC_W), lambda b, c, pt: (b, 0, 0)),
                  pl.BlockSpec((1, 1, KC_W), lambda b, c, pt: (b, 0, 0)),
                  pl.BlockSpec(memory_space=pl.ANY), pl.BlockSpec(memory_space=pl.ANY)],
        out_specs=pl.BlockSpec((1, MLA_H, MLA_L), lambda b, c, pt: (b, 0, 0)),
        scratch_shapes=[pltpu.VMEM((2, keys, MLA_L), F32), pltpu.VMEM((2, keys, MLA_ROPE), F32),
                        pltpu.SemaphoreType.DMA((2, 2)),
                        pltpu.VMEM((MLA_H, 1), F32), pltpu.VMEM((MLA_H, 1), F32),
                        pltpu.VMEM((MLA_H, MLA_L), F32)])
    return pl.pallas_call(
        kern,
        grid_spec=grid_spec,
        out_shape=jax.ShapeDtypeStruct((n_seq, MLA_H, MLA_L), F32),
        compiler_params=_cparams(("arbitrary", "arbitrary")),
        name="mla_decode",
    )(page_table, q_dec, knew, cache_ckv, cache_kpe)


GLA_TG = 512


def _head_norm_gate(o, gr, g_ref, seg_ref):
    ms = jnp.dot(o * o, seg_ref[...], preferred_element_type=F32, precision=lax.Precision.HIGHEST)
    return o * lax.rsqrt(ms + RMS_EPS) * g_ref[...] * (gr * jax.nn.sigmoid(gr))


def _gla_gates(glow, wgate_ref, bgate_ref):
    z = jnp.dot(glow.astype(BF16), wgate_ref[...], preferred_element_type=F32) + bgate_ref[...]
    return jax.nn.log_sigmoid(z) / GLA_NORM


def _gla_kernel(pg_ref, wgate_ref, bgate_ref, g_ref, seg_ref, tri_ref, o_ref, st_ref, st_sc, o_sc):
    @pl.when(pl.program_id(1) == 0)
    def _():
        st_sc[...] = jnp.zeros_like(st_sc)

    kw = GLA_H * GLA_DK
    c = GLA_CHUNK
    pg = pg_ref[...]
    logg = _gla_gates(pg[:, 2 * kw + GLA_W:2 * kw + GLA_W + LANE], wgate_ref, bgate_ref)
    hm_q = (lax.broadcasted_iota(jnp.int32, (GLA_H, c, kw), 2) // GLA_DK
            == lax.broadcasted_iota(jnp.int32, (GLA_H, c, kw), 0))
    hm_o = (lax.broadcasted_iota(jnp.int32, (GLA_H, c, GLA_W), 2) // GLA_DV
            == lax.broadcasted_iota(jnp.int32, (GLA_H, c, GLA_W), 0))
    hm_s = (lax.broadcasted_iota(jnp.int32, (GLA_W, kw), 0) // GLA_DV
            == lax.broadcasted_iota(jnp.int32, (GLA_W, kw), 1) // GLA_DK)
    causal = (lax.broadcasted_iota(jnp.int32, (c, c), 1) <= lax.broadcasted_iota(jnp.int32, (c, c), 0))
    nt = (((1,), (1,)), ((), ()))
    for ci in range(GLA_TG // c):
        r = slice(ci * c, (ci + 1) * c)
        q = pg[r, :kw] * (GLA_DK ** -0.5)
        k = pg[r, kw:2 * kw]
        v = pg[r, 2 * kw:2 * kw + GLA_W]
        bcum = jnp.dot(tri_ref[...], logg[r], preferred_element_type=F32,
                       precision=lax.Precision.HIGHEST)
        b_last = bcum[c - 1:c, :]
        qe = q * jnp.exp(bcum)
        ke = (k * jnp.exp(-bcum)).astype(BF16)
        kd = (k * jnp.exp(b_last - bcum)).astype(BF16)
        st = st_sc[...]
        qs = jnp.where(hm_q, qe[None], 0.0).reshape(GLA_H * c, kw).astype(BF16)
        att = lax.dot_general(qs, ke, nt, preferred_element_type=F32)
        att = jnp.where(causal[None], att.reshape(GLA_H, c, c), 0.0).reshape(GLA_H * c, c)
        oi = jnp.dot(att.astype(BF16), v.astype(BF16), preferred_element_type=F32)
        o_intra = jnp.sum(jnp.where(hm_o, oi.reshape(GLA_H, c, GLA_W), 0.0), axis=0)
        o_inter = lax.dot_general(qe.astype(BF16), st.astype(BF16), nt, preferred_element_type=F32)
        o_sc[r, :] = o_intra + o_inter
        ut = jnp.dot(v.T.astype(BF16), kd, preferred_element_type=F32)
        st_sc[...] = st * jnp.exp(b_last) + jnp.where(hm_s, ut, 0.0)

    st_ref[0] = st_sc[...]
    gr = pg[:, 2 * kw + GLA_W + LANE:]
    o_ref[...] = _head_norm_gate(o_sc[...], gr, g_ref, seg_ref)


def _gla_prompt(pg, gp, *, n_seq, seq_len):
    nj = seq_len // GLA_TG
    kw = GLA_H * GLA_DK
    row = lambda b, j: (b * nj + j, 0)
    return pl.pallas_call(
        _gla_kernel,
        grid=(n_seq, nj),
        in_specs=[pl.BlockSpec((GLA_TG, PG_W), row),
                  _const_spec((LANE, kw)), _const_spec((1, kw)), _const_spec((1, GLA_W)),
                  _const_spec((GLA_W, GLA_W)), _const_spec((GLA_CHUNK, GLA_CHUNK))],
        out_specs=[pl.BlockSpec((GLA_TG, GLA_W), row),
                   pl.BlockSpec((1, GLA_W, kw), lambda b, j: (b, 0, 0))],
        out_shape=[jax.ShapeDtypeStruct((n_seq * seq_len, GLA_W), F32),
                   jax.ShapeDtypeStruct((n_seq, GLA_W, kw), F32)],
        scratch_shapes=[pltpu.VMEM((GLA_W, kw), F32), pltpu.VMEM((GLA_TG, GLA_W), F32)],
        compiler_params=_cparams(("parallel", "arbitrary")),
        name="gla",
    )(pg, gp["wgate"], gp["bgate"], gp["g"], gp["seg"], gp["tri"])


def _gla_step_kernel(pg_ref, s_ref, wgate_ref, bgate_ref, g_ref, seg_ref, o_ref, snew_ref):
    kw = GLA_H * GLA_DK
    pg = pg_ref[...]
    n = pg.shape[0]
    q = pg[:, :kw] * (GLA_DK ** -0.5)
    k = pg[:, kw:2 * kw]
    v = pg[:, 2 * kw:2 * kw + GLA_W]
    decay = jnp.exp(_gla_gates(pg[:, 2 * kw + GLA_W:2 * kw + GLA_W + LANE], wgate_ref, bgate_ref))
    q_t, k_t, d_t = q.T, k.T, decay.T
    o_rows = []
    for b in range(n):
        vexp = jnp.concatenate(
            [jnp.broadcast_to(v[b:b + 1, h * GLA_DV:(h + 1) * GLA_DV], (GLA_DK, GLA_DV))
             for h in range(GLA_H)], axis=0)
        s_new = d_t[:, b:b + 1] * s_ref[b] + k_t[:, b:b + 1] * vexp
        snew_ref[b] = s_new
        w = q_t[:, b:b + 1] * s_new
        o_rows.append(jnp.concatenate(
            [jnp.sum(w[h * GLA_DK:(h + 1) * GLA_DK], axis=0, keepdims=True) for h in range(GLA_H)],
            axis=1))
    o = jnp.concatenate(o_rows, axis=0)
    o_ref[...] = _head_norm_gate(o, pg[:, 2 * kw + GLA_W + LANE:], g_ref, seg_ref)


def _gla_step(pg, state, gp):
    n = pg.shape[0]
    kw = GLA_H * GLA_DK
    return pl.pallas_call(
        _gla_step_kernel,
        out_shape=[jax.ShapeDtypeStruct((n, GLA_W), F32),
                   jax.ShapeDtypeStruct((n, kw, GLA_DV), F32)],
        compiler_params=pltpu.CompilerParams(vmem_limit_bytes=VMEM_LIMIT),
        name="gla_step",
    )(pg, state, gp["wgate"], gp["bgate"], gp["g"], gp["seg"])


FFN_TM = 512


def _mix_residual(x_ref, os5_ref, omla_ref, ogla_ref, wout_ref):
    return (x_ref[...]
            + jnp.dot(os5_ref[...].astype(BF16), wout_ref[:S5_W], preferred_element_type=F32)
            + jnp.dot(omla_ref[...].astype(BF16), wout_ref[S5_W:S5_W + MLA_W],
                      preferred_element_type=F32)
            + jnp.dot(ogla_ref[...].astype(BF16), wout_ref[S5_W + MLA_W:],
                      preferred_element_type=F32))


def _ffn_kernel(x_ref, os5_ref, omla_ref, ogla_ref, wout_ref, gf_ref, wv_ref, wg_ref, cw_ref,
                cb_ref, wo_ref, *rest, final_norm):
    if final_norm:
        gfin_ref, y_ref, buf_ref, hn_sc, acc_sc, carry_sc = rest
    else:
        y_ref, buf_ref, hn_sc, acc_sc, carry_sc = rest
    tm = x_ref.shape[0]

    @pl.when(pl.program_id(1) == 0)
    def _():
        carry_sc[...] = jnp.zeros_like(carry_sc)

    x1 = _mix_residual(x_ref, os5_ref, omla_ref, ogla_ref, wout_ref)
    y_ref[...] = x1
    hn_sc[...] = _rms(x1, gf_ref[...]).astype(BF16)
    acc_sc[...] = jnp.zeros_like(acc_sc)
    row = lax.broadcasted_iota(jnp.int32, (tm, FF_CHUNK), 0)

    def chunk(c, carry):
        hn = hn_sc[...]
        val = jnp.dot(hn, wv_ref[c], preferred_element_type=F32)
        gate = jnp.dot(hn, wg_ref[c], preferred_element_type=F32)
        prev = carry_sc[c]
        g1 = jnp.where(row == 0, prev[1:2], pltpu.roll(gate, 1, 0))
        g2 = jnp.where(row == 0, prev[0:1], jnp.where(row == 1, prev[1:2], pltpu.roll(gate, 2, 0)))
        cw = cw_ref[c]
        conv = cb_ref[c] + cw[0:1] * g2 + cw[1:2] * g1 + cw[2:3] * gate
        a = jax.nn.gelu(conv) * val
        acc_sc[...] += jnp.dot(a.astype(BF16), wo_ref[c], preferred_element_type=F32)
        carry_sc[c, 0:2, :] = gate[tm - 2:tm]
        return carry

    lax.fori_loop(0, FF_NC, chunk, 0)
    y = y_ref[...] + acc_sc[...]
    if final_norm:
        y = _rms(y, gfin_ref[...])
    y_ref[...] = y
    buf_ref[0] = carry_sc[:, 0:2, :]


def _ffn_prompt(x2d, os5_tm, omla, ogla, fp, *, n_seq, seq_len, final_g=None):
    tm = FFN_TM
    nj = seq_len // tm
    row = lambda b, j: (b * nj + j, 0)
    final_norm = final_g is not None
    in_specs = [pl.BlockSpec((tm, D_MODEL), row),
                pl.BlockSpec((tm, S5_W), lambda b, j: (j, b)),
                pl.BlockSpec((tm, MLA_W), row), pl.BlockSpec((tm, GLA_W), row),
                _resident_spec((D_MODEL, D_MODEL)), _const_spec((1, D_MODEL)),
                _resident_spec((FF_NC, D_MODEL, FF_CHUNK)), _resident_spec((FF_NC, D_MODEL, FF_CHUNK)),
                _const_spec((FF_NC, 3, FF_CHUNK)), _const_spec((FF_NC, 1, FF_CHUNK)),
                _resident_spec((FF_NC, FF_CHUNK, D_MODEL))]
    args = [x2d, os5_tm, omla, ogla, fp["wout"], fp["g"], fp["wv"], fp["wg"], fp["cw"], fp["cb"],
            fp["wo"]]
    if final_norm:
        in_specs.append(_const_spec((1, D_MODEL)))
        args.append(final_g)
    return pl.pallas_call(
        functools.partial(_ffn_kernel, final_norm=final_norm),
        grid=(n_seq, nj),
        in_specs=in_specs,
        out_specs=[pl.BlockSpec((tm, D_MODEL), row),
                   pl.BlockSpec((1, FF_NC, 2, FF_CHUNK), lambda b, j: (b, 0, 0, 0))],
        out_shape=[jax.ShapeDtypeStruct((n_seq * seq_len, D_MODEL), F32),
                   jax.ShapeDtypeStruct((n_seq, FF_NC, 2, FF_CHUNK), F32)],
        scratch_shapes=[pltpu.VMEM((tm, D_MODEL), BF16), pltpu.VMEM((tm, D_MODEL), F32),
                        pltpu.VMEM((FF_NC, 8, FF_CHUNK), F32)],
        compiler_params=_cparams(("parallel", "arbitrary")),
        name="ffn",
    )(*args)


def _ffn_step_kernel(x_ref, os5_ref, olat_ref, wuv_ref, ogla_ref, wout_ref, gf_ref, b0_ref, b1_ref,
                     wv_ref, wg_ref, cw_ref, cb_ref, wo_ref, *rest, final_norm):
    if final_norm:
        gfin_ref, y_ref, gate_ref, x1_sc, hn_sc, acc_sc = rest
    else:
        y_ref, gate_ref, x1_sc, hn_sc, acc_sc = rest
    c = pl.program_id(0)

    @pl.when(c == 0)
    def _():
        omla = jnp.dot(olat_ref[...].astype(BF16), wuv_ref[...], preferred_element_type=F32)
        x1 = (x_ref[...]
              + jnp.dot(os5_ref[...].astype(BF16), wout_ref[:S5_W], preferred_element_type=F32)
              + jnp.dot(omla.astype(BF16), wout_ref[S5_W:S5_W + MLA_W], preferred_element_type=F32)
              + jnp.dot(ogla_ref[...].astype(BF16), wout_ref[S5_W + MLA_W:],
                        preferred_element_type=F32))
        x1_sc[...] = x1
        hn_sc[...] = _rms(x1, gf_ref[...]).astype(BF16)
        acc_sc[...] = jnp.zeros_like(acc_sc)

    hn = hn_sc[...]
    val = jnp.dot(hn, wv_ref[0], preferred_element_type=F32)
    gate = jnp.dot(hn, wg_ref[0], preferred_element_type=F32)
    cw = cw_ref[0]
    conv = cb_ref[0] + cw[0:1] * b0_ref[...] + cw[1:2] * b1_ref[...] + cw[2:3] * gate
    a = jax.nn.gelu(conv) * val
    acc_sc[...] += jnp.dot(a.astype(BF16), wo_ref[0], preferred_element_type=F32)
    gate_ref[...] = gate

    @pl.when(c == FF_NC - 1)
    def _():
        y = x1_sc[...] + acc_sc[...]
        if final_norm:
            y = _rms(y, gfin_ref[...])
        y_ref[...] = y


def _ffn_step(x2d, os5, olat, wuv_bd, ogla, buf0, buf1, fp, *, final_g=None):
    n = x2d.shape[0]
    final_norm = final_g is not None
    chunk3 = lambda c: (c, 0, 0)
    in_specs = [_const_spec((n, D_MODEL)), _const_spec((n, S5_W)),
                _const_spec((n, MLA_H * MLA_L)), _const_spec((MLA_H * MLA_L, MLA_W)),
                _const_spec((n, GLA_W)), _const_spec((D_MODEL, D_MODEL)), _const_spec((1, D_MODEL)),
                pl.BlockSpec((n, FF_CHUNK), lambda c: (0, c)), pl.BlockSpec((n, FF_CHUNK), lambda c: (0, c)),
                pl.BlockSpec((1, D_MODEL, FF_CHUNK), chunk3), pl.BlockSpec((1, D_MODEL, FF_CHUNK), chunk3),
                pl.BlockSpec((1, 3, FF_CHUNK), chunk3), pl.BlockSpec((1, 1, FF_CHUNK), chunk3),
                pl.BlockSpec((1, FF_CHUNK, D_MODEL), chunk3)]
    args = [x2d, os5, olat, wuv_bd, ogla, fp["wout"], fp["g"], buf0, buf1, fp["wv"], fp["wg"],
            fp["cw"], fp["cb"], fp["wo"]]
    if final_norm:
        in_specs.append(_const_spec((1, D_MODEL)))
        args.append(final_g)
    return pl.pallas_call(
        functools.partial(_ffn_step_kernel, final_norm=final_norm),
        grid=(FF_NC,),
        in_specs=in_specs,
        out_specs=[_const_spec((n, D_MODEL)), pl.BlockSpec((n, FF_CHUNK), lambda c: (0, c))],
        out_shape=[jax.ShapeDtypeStruct((n, D_MODEL), F32), jax.ShapeDtypeStruct((n, D_FF), F32)],
        scratch_shapes=[pltpu.VMEM((n, D_MODEL), F32), pltpu.VMEM((n, D_MODEL), BF16),
                        pltpu.VMEM((n, D_MODEL), F32)],
        compiler_params=_cparams(("arbitrary",)),
        name="ffn_step",
    )(*args)


def _rot_cols(w):
    half = MLA_ROPE // 2
    return jnp.concatenate([-w[..., half:], w[..., :half]], axis=-1)


def _pad_cols(w, width):
    return jnp.pad(w, [(0, 0)] * (w.ndim - 1) + [(0, width - w.shape[-1])])


def _layer_params(l, norm_mix_g, w_in, s5_a_re, s5_a_im, s5_log_dt, s5_b_re, s5_b_im, s5_c_re, s5_c_im,
                  s5_d, s5_w_glu, s5_b_glu, mla_q_norm_g, mla_w_qb, mla_kv_norm_g, mla_w_uk, mla_w_uv,
                  gla_w_gate, gla_b_gate, gla_norm_g, w_out, norm_ffn_g, w_ffn_in, ffn_conv_w, ffn_conv_b,
                  w_ffn_out):
    w = w_in[l]
    o = 0
    cols = {}
    for name, width in (("u", 256), ("cq", 256), ("ckv", 128), ("kpe", 32), ("gq", 128), ("gk", 128),
                        ("gv", 256), ("glow", 16), ("gr", 256)):
        cols[name] = w[:, o:o + width]
        o += width
    w_ext = jnp.concatenate(
        [cols["u"], cols["cq"], cols["ckv"], _pad_cols(cols["kpe"], LANE),
         _pad_cols(_rot_cols(cols["kpe"]), LANE), cols["gq"], cols["gk"], cols["gv"],
         _pad_cols(cols["glow"], LANE), cols["gr"]], axis=1).astype(BF16)
    inp = dict(g=norm_mix_g[l][None], w=w_ext)

    a_re, a_im = s5_a_re[l], s5_a_im[l]
    dt = jnp.exp(s5_log_dt[l])[:, None]
    mag = jnp.exp(a_re * dt)
    ab_re, ab_im = mag * jnp.cos(a_im * dt), mag * jnp.sin(a_im * dt)
    den = a_re * a_re + a_im * a_im
    k_re = ((ab_re - 1.0) * a_re + ab_im * a_im) / den
    k_im = (ab_im * a_re - (ab_re - 1.0) * a_im) / den
    bb_re = k_re[..., None] * s5_b_re[l] - k_im[..., None] * s5_b_im[l]
    bb_im = k_re[..., None] * s5_b_im[l] + k_im[..., None] * s5_b_re[l]
    eye = jnp.eye(S5_G, dtype=F32)
    bd_in = lambda m: jnp.einsum("gpc,gh->gchp", m, eye).reshape(S5_W, S5_N)
    bd_out = lambda m: jnp.einsum("gcp,gh->gphc", m, eye).reshape(S5_N, S5_W)
    s5p = dict(
        a=jnp.stack([ab_re.reshape(S5_N), ab_im.reshape(S5_N)]),
        bm=jnp.concatenate([bd_in(bb_re), bd_in(bb_im)], axis=1).astype(BF16),
        cm=jnp.concatenate([bd_out(s5_c_re[l]), -bd_out(s5_c_im[l])], axis=0).astype(BF16),
        d=s5_d[l][None], wglu=s5_w_glu[l].astype(BF16), bglu=s5_b_glu[l][None])

    wqb = mla_w_qb[l].reshape(MLA_QL, MLA_H, MLA_NOPE + MLA_ROPE)
    w_nope = wqb[:, :, :MLA_NOPE].reshape(MLA_QL, MLA_H * MLA_NOPE)
    w_rope = wqb[:, :, MLA_NOPE:]
    wq = jnp.concatenate(
        [w_nope, _pad_cols(w_rope, LANE).reshape(MLA_QL, MLA_H * LANE),
         _pad_cols(_rot_cols(w_rope), LANE).reshape(MLA_QL, MLA_H * LANE)], axis=1).astype(BF16)
    eye_h = jnp.eye(MLA_H, dtype=F32)
    wuk = jnp.einsum("lhn,hk->hnkl", mla_w_uk[l], eye_h).reshape(MLA_H * MLA_NOPE, MLA_H * MLA_L)
    wuv = jnp.einsum("lhv,hk->hlkv", mla_w_uv[l], eye_h).reshape(MLA_H, MLA_L, MLA_W)
    mlap = dict(gq=mla_q_norm_g[l][None], wq=wq, wuk=wuk.astype(BF16), gkv=mla_kv_norm_g[l][None],
                wuv=wuv.astype(BF16))

    seg = (jnp.arange(GLA_W)[:, None] // GLA_DV == jnp.arange(GLA_W)[None, :] // GLA_DV)
    tri = jnp.arange(GLA_CHUNK)[None, :] <= jnp.arange(GLA_CHUNK)[:, None]
    glap = dict(wgate=jnp.pad(gla_w_gate[l], ((0, LANE - GLA_R), (0, 0))).astype(BF16),
                bgate=gla_b_gate[l][None], g=gla_norm_g[l][None],
                seg=seg.astype(F32) / GLA_DV, tri=tri.astype(F32))

    wf = w_ffn_in[l]
    chunked = lambda m: m.reshape(D_MODEL, FF_NC, FF_CHUNK).transpose(1, 0, 2).astype(BF16)
    ffnp = dict(wout=w_out[l].astype(BF16), g=norm_ffn_g[l][None],
                wv=chunked(wf[:, :D_FF]), wg=chunked(wf[:, D_FF:]),
                cw=ffn_conv_w[l].reshape(3, FF_NC, FF_CHUNK).transpose(1, 0, 2),
                cb=ffn_conv_b[l].reshape(FF_NC, 1, FF_CHUNK),
                wo=w_ffn_out[l].reshape(FF_NC, FF_CHUNK, D_MODEL).astype(BF16))
    return inp, s5p, mlap, glap, ffnp


def _rope_tables(pos):
    half = MLA_ROPE // 2
    inv = ROPE_THETA ** (-jnp.arange(half, dtype=F32) / half)
    ang = pos.astype(F32)[:, None] * inv[None, :]
    cos = jnp.cos(ang)
    sin = jnp.sin(ang)
    return (_pad_cols(jnp.concatenate([cos, cos], axis=1), LANE),
            _pad_cols(jnp.concatenate([sin, sin], axis=1), LANE))


def _s5_state_out(hfin):
    n = hfin.shape[0]
    return jnp.stack([hfin[:, :S5_N].reshape(n, S5_G, S5_P), hfin[:, S5_N:].reshape(n, S5_G, S5_P)],
                     axis=-1)


def kernel(x_prompt, x_sample, cache_mla_ckv, cache_mla_krope, page_table, state_s5, state_gla, state_ffn_conv, norm_mix_g, w_in, s5_a_re, s5_a_im, s5_log_dt, s5_b_re, s5_b_im, s5_c_re, s5_c_im, s5_d, s5_w_glu, s5_b_glu, mla_q_norm_g, mla_w_qb, mla_kv_norm_g, mla_w_uk, mla_w_uv, gla_w_gate, gla_b_gate, gla_norm_g, w_out, norm_ffn_g, w_ffn_in, ffn_conv_w, ffn_conv_b, w_ffn_out, norm_final_g):
    bp, sp = x_prompt.shape[:2]
    bs = x_sample.shape[0]
    n_pages = page_table.shape[1]
    past_len = n_pages * cache_mla_ckv.shape[2]
    kw = GLA_H * GLA_DK

    cos_p, sin_p = _rope_tables(jnp.arange(sp))
    cos_s, sin_s = _rope_tables(jnp.full((bs,), past_len))
    gfin = norm_final_g[None]
    xp = x_prompt.reshape(bp * sp, D_MODEL)
    xd = x_sample.reshape(bs, D_MODEL)
    outs = [[] for _ in range(10)]

    for l in range(DEPTH):
        inp, s5p, mlap, glap, ffnp = _layer_params(
            l, norm_mix_g, w_in, s5_a_re, s5_a_im, s5_log_dt, s5_b_re, s5_b_im, s5_c_re, s5_c_im, s5_d,
            s5_w_glu, s5_b_glu, mla_q_norm_g, mla_w_qb, mla_kv_norm_g, mla_w_uk, mla_w_uv, gla_w_gate,
            gla_b_gate, gla_norm_g, w_out, norm_ffn_g, w_ffn_in, ffn_conv_w, ffn_conv_b, w_ffn_out)
        last = l == DEPTH - 1

        u, pm, pg = _in_proj(xp, inp["g"], inp["w"], n_seq=bp, seq_len=sp, tm=512)
        o_s5, hfin = _s5(u.reshape(sp * bp, S5_W), jnp.zeros((bp, 2 * S5_N), F32), s5p,
                         r_rows=bp, t_total=sp, t_steps=128)
        qcat, kcat, ckv, kpe = _mla_prep(pm, cos_p, sin_p, mlap, n_seq=bp, seq_len=sp, tm=512,
                                         q_dtype=BF16)
        o_mla = _attn_prompt(qcat, kcat, mlap["wuv"], n_seq=bp, seq_len=sp)
        o_gla, st = _gla_prompt(pg, glap, n_seq=bp, seq_len=sp)
        xp, cbuf = _ffn_prompt(xp, o_s5.reshape(sp, bp * S5_W), o_mla, o_gla, ffnp, n_seq=bp,
                               seq_len=sp, final_g=gfin if last else None)
        outs[0].append(ckv.reshape(bp, sp // PAGE, PAGE, MLA_L))
        outs[1].append(kpe.reshape(bp, sp // PAGE, PAGE, MLA_ROPE))
        outs[2].append(_s5_state_out(hfin))
        st5 = st.reshape(bp, GLA_H, GLA_DV, GLA_H, GLA_DK)
        outs[3].append(jnp.stack([st5[:, h, :, h, :] for h in range(GLA_H)], axis=1).swapaxes(2, 3))
        outs[4].append(cbuf.transpose(0, 2, 1, 3).reshape(bp, 2, D_FF))

        u, pm, pg = _in_proj(xd, inp["g"], inp["w"], n_seq=1, seq_len=bs, tm=bs)
        h0 = jnp.concatenate([state_s5[l, ..., 0].reshape(bs, S5_N), state_s5[l, ..., 1].reshape(bs, S5_N)],
                             axis=1)
        o_s5, hfin = _s5(u, h0, s5p, r_rows=bs, t_total=1, t_steps=1)
        qcat, kcat, ckv, kpe = _mla_prep(pm, cos_s, sin_s, mlap, n_seq=1, seq_len=bs, tm=bs, q_dtype=F32)
        o_lat = _attn_decode(page_table, qcat.transpose(1, 0, 2), kcat[:, None, :], cache_mla_ckv,
                             cache_mla_krope, layer=l)
        o_gla, st = _gla_step(pg, state_gla[l].reshape(bs, kw, GLA_DV), glap)
        buf = state_ffn_conv[l]
        xd, gate = _ffn_step(xd, o_s5, o_lat.reshape(bs, MLA_H * MLA_L),
                             mlap["wuv"].reshape(MLA_H * MLA_L, MLA_W), o_gla, buf[:, 0], buf[:, 1], ffnp,
                             final_g=gfin if last else None)
        outs[5].append(ckv[:, None, :])
        outs[6].append(kpe[:, None, :])
        outs[7].append(_s5_state_out(hfin))
        outs[8].append(st.reshape(bs, GLA_H, GLA_DK, GLA_DV))
        outs[9].append(jnp.stack([buf[:, 1], gate], axis=1))

    return (xp.reshape(bp, sp, D_MODEL), xd.reshape(bs, 1, D_MODEL)) + tuple(jnp.stack(o) for o in outs)
```

```python
import functools
import math

import jax
import jax.numpy as jnp
from jax import lax
from jax.experimental import pallas as pl
from jax.experimental.pallas import tpu as pltpu

F32 = jnp.float32
BF16 = jnp.bfloat16

D_MODEL = 1024
DEPTH = 4
PAGE = 128
S5_W = 256
S5_G = 16
S5_C = 16
S5_P = 64
S5_N = S5_G * S5_P
MLA_H = 8
MLA_NOPE = 64
MLA_ROPE = 32
MLA_V = 64
MLA_W = MLA_H * MLA_V
MLA_QL = 256
MLA_L = 128
MLA_SCALE = (MLA_NOPE + MLA_ROPE) ** -0.5
Q_SCALE = MLA_SCALE * math.log2(math.e)
ROPE_THETA = 10000.0
GLA_H = 4
GLA_W = 256
GLA_DV = 64
GLA_DK = 32
GLA_R = 16
GLA_NORM = 16.0
GLA_CHUNK = 64
D_FF = 2816
RMS_EPS = 1e-6
NEG_INF = -1e30

LANE = 128
PM_W = 640
PG_W = 896
W_EXT = S5_W + PM_W + PG_W
QA_W = MLA_H * MLA_NOPE + 2 * MLA_H * LANE
KC_W = 2 * LANE

FF_CHUNK = 256
FF_NC = D_FF // FF_CHUNK
VMEM_LIMIT = 56 * 1024 * 1024


def _cparams(sem, vmem=VMEM_LIMIT):
    return pltpu.CompilerParams(dimension_semantics=sem, vmem_limit_bytes=vmem)


def _rms(x, g):
    ms = jnp.mean(x * x, axis=-1, keepdims=True)
    return x * lax.rsqrt(ms + RMS_EPS) * g


def _const_spec(shape):
    nd = len(shape)
    return pl.BlockSpec(shape, lambda *_: (0,) * nd)


def _resident_spec(shape):
    nd = len(shape)
    return pl.BlockSpec(shape, lambda *_: (0,) * nd, pipeline_mode=pl.Buffered(1))


def _in_proj_kernel(x_ref, g_ref, w_ref, u_ref, pm_ref, pg_ref):
    h = _rms(x_ref[...], g_ref[...]).astype(BF16)
    u_ref[...] = jnp.dot(h, w_ref[:, :S5_W], preferred_element_type=F32)
    pm_ref[...] = jnp.dot(h, w_ref[:, S5_W:S5_W + PM_W], preferred_element_type=F32)
    pg_ref[...] = jnp.dot(h, w_ref[:, S5_W + PM_W:], preferred_element_type=F32)


def _in_proj(x2d, g, w_ext, *, n_seq, seq_len, tm):
    n = n_seq * seq_len
    nj = seq_len // tm
    row = lambda b, j: (b * nj + j, 0)
    return pl.pallas_call(
        _in_proj_kernel,
        grid=(n_seq, nj),
        in_specs=[pl.BlockSpec((tm, D_MODEL), row), _const_spec((1, D_MODEL)),
                  _const_spec((D_MODEL, W_EXT))],
        out_specs=[pl.BlockSpec((tm, S5_W), lambda b, j: (j, b)),
                   pl.BlockSpec((tm, PM_W), row), pl.BlockSpec((tm, PG_W), row)],
        out_shape=[jax.ShapeDtypeStruct((seq_len, n_seq * S5_W), F32),
                   jax.ShapeDtypeStruct((n, PM_W), F32),
                   jax.ShapeDtypeStruct((n, PG_W), F32)],
        compiler_params=_cparams(("parallel", "parallel")),
        name="in_proj",
    )(x2d, g, w_ext)


def _s5_kernel(u_ref, h0_ref, a_ref, bm_ref, cm_ref, d_ref, wg_ref, bg_ref,
               o_ref, hfin_ref, h_sc, bu_sc, *, r_rows, t_steps):
    @pl.when(pl.program_id(0) == 0)
    def _():
        h_sc[...] = h0_ref[...]

    u = u_ref[...]
    bu_sc[...] = jnp.dot(u.astype(BF16), bm_ref[...], preferred_element_type=F32)
    ar = jnp.broadcast_to(a_ref[0:1, :], (r_rows, S5_N))
    ai = jnp.broadcast_to(a_ref[1:2, :], (r_rows, S5_N))

    def step(t, carry):
        hr, hi = carry
        r0 = pl.multiple_of(t * r_rows, r_rows)
        nr = ar * hr - ai * hi + bu_sc[pl.ds(r0, r_rows), :S5_N]
        ni = ar * hi + ai * hr + bu_sc[pl.ds(r0, r_rows), S5_N:]
        bu_sc[pl.ds(r0, r_rows), :S5_N] = nr
        bu_sc[pl.ds(r0, r_rows), S5_N:] = ni
        return nr, ni

    hr, hi = lax.fori_loop(0, t_steps, step, (h_sc[:, :S5_N], h_sc[:, S5_N:]))
    h_sc[:, :S5_N] = hr
    h_sc[:, S5_N:] = hi
    hfin_ref[:, :S5_N] = hr
    hfin_ref[:, S5_N:] = hi

    y = jnp.dot(bu_sc[...].astype(BF16), cm_ref[...], preferred_element_type=F32) + d_ref[...] * u
    y = jax.nn.gelu(y)
    z = jnp.dot(y.astype(BF16), wg_ref[...], preferred_element_type=F32) + bg_ref[...]
    o_ref[...] = y * jax.nn.sigmoid(z)


def _s5(u_tm, h0, sp, *, r_rows, t_total, t_steps):
    rows = r_rows * t_steps
    kern = functools.partial(_s5_kernel, r_rows=r_rows, t_steps=t_steps)
    return pl.pallas_call(
        kern,
        grid=(t_total // t_steps,),
        in_specs=[pl.BlockSpec((rows, S5_W), lambda i: (i, 0)),
                  _const_spec((r_rows, 2 * S5_N)), _const_spec((2, S5_N)),
                  _const_spec((S5_W, 2 * S5_N)), _const_spec((2 * S5_N, S5_W)),
                  _const_spec((1, S5_W)), _const_spec((S5_W, S5_W)), _const_spec((1, S5_W))],
        out_specs=[pl.BlockSpec((rows, S5_W), lambda i: (i, 0)),
                   _const_spec((r_rows, 2 * S5_N))],
        out_shape=[jax.ShapeDtypeStruct((t_total * r_rows, S5_W), F32),
                   jax.ShapeDtypeStruct((r_rows, 2 * S5_N), F32)],
        scratch_shapes=[pltpu.VMEM((r_rows, 2 * S5_N), F32),
                        pltpu.VMEM((rows, 2 * S5_N), F32)],
        compiler_params=_cparams(("arbitrary",)),
        name="s5",
    )(u_tm, h0, sp["a"], sp["bm"], sp["cm"], sp["d"], sp["wglu"], sp["bglu"])


def _mla_prep_kernel(pm_ref, cos_ref, sin_ref, gq_ref, wq_ref, wuk_ref, gkv_ref,
                     qcat_ref, kcat_ref, ckv_ref, kpe_ref):
    pm = pm_ref[...]
    cos = cos_ref[...]
    sin = sin_ref[...]
    qn = _rms(pm[:, :MLA_QL], gq_ref[...]).astype(BF16)
    qa = jnp.dot(qn, wq_ref[...], preferred_element_type=F32)
    n_nope = MLA_H * MLA_NOPE
    qlat = jnp.dot(qa[:, :n_nope].astype(BF16), wuk_ref[...], preferred_element_type=F32)
    for h in range(MLA_H):
        lo = n_nope + h * LANE
        qpe = qa[:, lo:lo + LANE] * cos + qa[:, lo + MLA_H * LANE:lo + (MLA_H + 1) * LANE] * sin
        qcat_ref[h, :, :LANE] = (qlat[:, h * LANE:(h + 1) * LANE] * Q_SCALE).astype(qcat_ref.dtype)
        qcat_ref[h, :, LANE:] = (qpe * Q_SCALE).astype(qcat_ref.dtype)
    ckv = _rms(pm[:, MLA_QL:MLA_QL + MLA_L], gkv_ref[...])
    kpe = pm[:, MLA_QL + MLA_L:MLA_QL + 2 * MLA_L] * cos + pm[:, MLA_QL + 2 * MLA_L:] * sin
    ckv_ref[...] = ckv
    kpe_ref[...] = kpe[:, :MLA_ROPE]
    kcat_ref[:, :LANE] = ckv.astype(kcat_ref.dtype)
    kcat_ref[:, LANE:] = kpe.astype(kcat_ref.dtype)


def _mla_prep(pm, cos_t, sin_t, mp, *, n_seq, seq_len, tm, q_dtype):
    n = n_seq * seq_len
    nj = seq_len // tm
    row = lambda b, j: (b * nj + j, 0)
    pos = lambda b, j: (j, 0)
    return pl.pallas_call(
        _mla_prep_kernel,
        grid=(n_seq, nj),
        in_specs=[pl.BlockSpec((tm, PM_W), row),
                  pl.BlockSpec((tm, LANE), pos), pl.BlockSpec((tm, LANE), pos),
                  _const_spec((1, MLA_QL)), _const_spec((MLA_QL, QA_W)),
                  _const_spec((MLA_H * MLA_NOPE, MLA_H * LANE)), _const_spec((1, MLA_L))],
        out_specs=[pl.BlockSpec((MLA_H, tm, KC_W), lambda b, j: (0, b * nj + j, 0)),
                   pl.BlockSpec((tm, KC_W), row),
                   pl.BlockSpec((tm, MLA_L), row), pl.BlockSpec((tm, MLA_ROPE), row)],
        out_shape=[jax.ShapeDtypeStruct((MLA_H, n, KC_W), q_dtype),
                   jax.ShapeDtypeStruct((n, KC_W), q_dtype),
                   jax.ShapeDtypeStruct((n, MLA_L), F32),
                   jax.ShapeDtypeStruct((n, MLA_ROPE), F32)],
        compiler_params=_cparams(("parallel", "parallel")),
        name="mla_prep",
    )(pm, cos_t, sin_t, mp["gq"], mp["wq"], mp["wuk"], mp["gkv"])


ATT_TQ = 256
ATT_TK = 512
ATT_NG = 4


def _attn_kernel(q_ref, k_ref, wuv_ref, o_ref, m_sc, acc_sc):
    i = pl.program_id(1)
    rows = MLA_H * ATT_TQ
    hg = MLA_H // ATT_NG
    grows = hg * ATT_TQ
    m_sc[...] = jnp.full((rows, LANE), NEG_INF, F32)
    acc_sc[...] = jnp.zeros((rows, 2 * MLA_L), F32)
    ones = jnp.ones((ATT_TK, LANE), BF16)

    def tile(j, masked):
        kt = k_ref[pl.ds(pl.multiple_of(j * ATT_TK, ATT_TK), ATT_TK), :]
        v_ext = jnp.concatenate([kt[:, :MLA_L], ones], axis=1)
        if masked:
            qpos = i * ATT_TQ + lax.broadcasted_iota(jnp.int32, (ATT_TQ, ATT_TK), 0)
            kpos = j * ATT_TK + lax.broadcasted_iota(jnp.int32, (ATT_TQ, ATT_TK), 1)
            visible = (kpos <= qpos)[None]
        scores = []
        for g in range(ATT_NG):
            q = q_ref[g * hg:(g + 1) * hg].reshape(grows, KC_W)
            scores.append(lax.dot_general(q, kt, (((1,), (1,)), ((), ())), preferred_element_type=F32))
        for g in range(ATT_NG):
            r = slice(g * grows, (g + 1) * grows)
            s = scores[g]
            if masked:
                s = jnp.where(visible, s.reshape(hg, ATT_TQ, ATT_TK), NEG_INF).reshape(grows, ATT_TK)
            m_old = m_sc[r, :]
            m_new = jnp.maximum(m_old, jnp.max(s, axis=-1, keepdims=True))
            alpha = jnp.exp2(m_old - m_new)
            p = jnp.exp2(s - jnp.tile(m_new, (1, ATT_TK // LANE)))
            pv = jnp.dot(p.astype(kt.dtype), v_ext, preferred_element_type=F32)
            acc_sc[r, :] = jnp.tile(alpha, (1, 2)) * acc_sc[r, :] + pv
            m_sc[r, :] = m_new

    n_full = (i * ATT_TQ) // ATT_TK

    def body(j, c):
        tile(j, False)
        return c

    lax.fori_loop(0, n_full, body, 0)
    tile(n_full, True)
    acc = acc_sc[...]
    o_lat = (acc[:, :MLA_L] / acc[:, MLA_L:]).astype(wuv_ref.dtype)
    out = jnp.dot(o_lat[:ATT_TQ], wuv_ref[0], preferred_element_type=F32)
    for h in range(1, MLA_H):
        out += jnp.dot(o_lat[h * ATT_TQ:(h + 1) * ATT_TQ], wuv_ref[h], preferred_element_type=F32)
    o_ref[...] = out


def _attn_prompt(qcat, kcat, wuv, *, n_seq, seq_len):
    nq = seq_len // ATT_TQ
    rows = MLA_H * ATT_TQ
    return pl.pallas_call(
        _attn_kernel,
        grid=(n_seq, nq),
        in_specs=[pl.BlockSpec((MLA_H, ATT_TQ, KC_W), lambda b, i: (0, b * nq + i, 0)),
                  pl.BlockSpec((seq_len, KC_W), lambda b, i: (b, 0)),
                  _const_spec((MLA_H, MLA_L, MLA_W))],
        out_specs=pl.BlockSpec((ATT_TQ, MLA_W), lambda b, i: (b * nq + i, 0)),
        out_shape=jax.ShapeDtypeStruct((n_seq * seq_len, MLA_W), F32),
        scratch_shapes=[pltpu.VMEM((rows, LANE), F32), pltpu.VMEM((rows, 2 * MLA_L), F32)],
        compiler_params=_cparams(("parallel", "parallel")),
        name="mla_attn",
    )(qcat, kcat, wuv)


DEC_G = 32


def _page_copies(pt_ref, ckv_hbm, kpe_hbm, ckv_buf, kpe_buf, sems, layer, b, c, slot):
    copies = []
    for g in range(DEC_G):
        page = pt_ref[b, c * DEC_G + g]
        dst = pl.ds(g * PAGE, PAGE)
        copies.append(pltpu.make_async_copy(ckv_hbm.at[layer, page], ckv_buf.at[slot, dst, :],
                                            sems.at[0, slot]))
        copies.append(pltpu.make_async_copy(kpe_hbm.at[layer, page], kpe_buf.at[slot, :, dst],
                                            sems.at[1, slot]))
    return copies


def _decode_kernel(pt_ref, q_ref, knew_ref, ckv_hbm, kpe_hbm, o_ref,
                   ckv_buf, kpe_buf, sems, m_sc, l_sc, acc_sc, *, layer, n_chunks):
    b = pl.program_id(0)
    c = pl.program_id(1)
    step = b * n_chunks + c
    total = pl.num_programs(0) * n_chunks
    slot = lax.rem(step, 2)
    copies = functools.partial(_page_copies, pt_ref, ckv_hbm, kpe_hbm, ckv_buf, kpe_buf, sems, layer)

    @pl.when(step == 0)
    def _():
        for cp in copies(b, c, slot):
            cp.start()

    @pl.when(step + 1 < total)
    def _():
        nxt = step + 1
        for cp in copies(nxt // n_chunks, lax.rem(nxt, n_chunks), 1 - slot):
            cp.start()

    @pl.when(c == 0)
    def _():
        m_sc[...] = jnp.full((MLA_H, 1), NEG_INF, F32)
        l_sc[...] = jnp.zeros((MLA_H, 1), F32)
        acc_sc[...] = jnp.zeros((MLA_H, MLA_L), F32)

    for cp in copies(b, c, slot):
        cp.wait()

    q = q_ref[0]
    nt = (((1,), (1,)), ((), ()))
    ck = ckv_buf[slot].astype(BF16)
    kp_t = kpe_buf[slot].astype(BF16)
    s = (lax.dot_general(q[:, :MLA_L].astype(BF16), ck, nt, preferred_element_type=F32)
         + jnp.dot(q[:, MLA_L:MLA_L + MLA_ROPE].astype(BF16), kp_t, preferred_element_type=F32))
    m_old = m_sc[...]
    m_new = jnp.maximum(m_old, jnp.max(s, axis=-1, keepdims=True))
    alpha = jnp.exp2(m_old - m_new)
    p = jnp.exp2(s - m_new)
    l_sc[...] = alpha * l_sc[...] + jnp.sum(p, axis=-1, keepdims=True)
    acc_sc[...] = alpha * acc_sc[...] + jnp.dot(p.astype(BF16), ck, preferred_element_type=F32)
    m_sc[...] = m_new

    @pl.when(c == n_chunks - 1)
    def _():
        kn = knew_ref[0]
        s_new = jnp.sum(q * kn, axis=-1, keepdims=True)
        m_old = m_sc[...]
        m_new = jnp.maximum(m_old, s_new)
        alpha = jnp.exp2(m_old - m_new)
        p_new = jnp.exp2(s_new - m_new)
        l = alpha * l_sc[...] + p_new
        acc = alpha * acc_sc[...] + p_new * kn[:, :MLA_L]
        o_ref[0] = acc / l


def _attn_decode(page_table, q_dec, knew, cache_ckv, cache_kpe, *, layer):
    n_seq, n_pages = page_table.shape
    n_chunks = n_pages // DEC_G
    keys = DEC_G * PAGE
    kern = functools.partial(_decode_kernel, layer=layer, n_chunks=n_chunks)
    grid_spec = pltpu.PrefetchScalarGridSpec(
        num_scalar_prefetch=1,
        grid=(n_seq, n_chunks),
        in_specs=[pl.BlockSpec((1, MLA_H, KC_W), lambda b, c, pt: (b, 0, 0)),
                  pl.BlockSpec((1, 1, KC_W), lambda b, c, pt: (b, 0, 0)),
                  pl.BlockSpec(memory_space=pl.ANY), pl.BlockSpec(memory_space=pl.ANY)],
        out_specs=pl.BlockSpec((1, MLA_H, MLA_L), lambda b, c, pt: (b, 0, 0)),
        scratch_shapes=[pltpu.VMEM((2, keys, MLA_L), F32), pltpu.VMEM((2, MLA_ROPE, keys), F32),
                        pltpu.SemaphoreType.DMA((2, 2)),
                        pltpu.VMEM((MLA_H, 1), F32), pltpu.VMEM((MLA_H, 1), F32),
                        pltpu.VMEM((MLA_H, MLA_L), F32)])
    return pl.pallas_call(
        kern,
        grid_spec=grid_spec,
        out_shape=jax.ShapeDtypeStruct((n_seq, MLA_H, MLA_L), F32),
        compiler_params=_cparams(("arbitrary", "arbitrary")),
        name="mla_decode",
    )(page_table, q_dec, knew, cache_ckv, cache_kpe)


GLA_TG = 512


def _head_norm_gate(o, gr, g_ref, seg_ref):
    ms = jnp.dot(o * o, seg_ref[...], preferred_element_type=F32, precision=lax.Precision.HIGHEST)
    return o * lax.rsqrt(ms + RMS_EPS) * g_ref[...] * (gr * jax.nn.sigmoid(gr))


def _gla_gates(glow, wgate_ref, bgate_ref):
    z = jnp.dot(glow.astype(BF16), wgate_ref[...], preferred_element_type=F32) + bgate_ref[...]
    return jax.nn.log_sigmoid(z) / GLA_NORM


def _gla_kernel(pg_ref, wgate_ref, bgate_ref, g_ref, seg_ref, tri_ref, o_ref, st_ref, st_sc, o_sc):
    @pl.when(pl.program_id(1) == 0)
    def _():
        st_sc[...] = jnp.zeros_like(st_sc)

    kw = GLA_H * GLA_DK
    c = GLA_CHUNK
    pg = pg_ref[...]
    logg = _gla_gates(pg[:, 2 * kw + GLA_W:2 * kw + GLA_W + LANE], wgate_ref, bgate_ref)
    hm_q = (lax.broadcasted_iota(jnp.int32, (GLA_H, c, kw), 2) // GLA_DK
            == lax.broadcasted_iota(jnp.int32, (GLA_H, c, kw), 0))
    hm_o = (lax.broadcasted_iota(jnp.int32, (GLA_H, c, GLA_W), 2) // GLA_DV
            == lax.broadcasted_iota(jnp.int32, (GLA_H, c, GLA_W), 0))
    hm_s = (lax.broadcasted_iota(jnp.int32, (GLA_W, kw), 0) // GLA_DV
            == lax.broadcasted_iota(jnp.int32, (GLA_W, kw), 1) // GLA_DK)
    causal = (lax.broadcasted_iota(jnp.int32, (c, c), 1) <= lax.broadcasted_iota(jnp.int32, (c, c), 0))
    nt = (((1,), (1,)), ((), ()))
    for ci in range(GLA_TG // c):
        r = slice(ci * c, (ci + 1) * c)
        q = pg[r, :kw] * (GLA_DK ** -0.5)
        k = pg[r, kw:2 * kw]
        v = pg[r, 2 * kw:2 * kw + GLA_W]
        bcum = jnp.dot(tri_ref[...], logg[r], preferred_element_type=F32,
                       precision=lax.Precision.HIGHEST)
        b_last = bcum[c - 1:c, :]
        qe = q * jnp.exp(bcum)
        ke = (k * jnp.exp(-bcum)).astype(BF16)
        kd = (k * jnp.exp(b_last - bcum)).astype(BF16)
        st = st_sc[...]
        qs = jnp.where(hm_q, qe[None], 0.0).reshape(GLA_H * c, kw).astype(BF16)
        att = lax.dot_general(qs, ke, nt, preferred_element_type=F32)
        att = jnp.where(causal[None], att.reshape(GLA_H, c, c), 0.0).reshape(GLA_H * c, c)
        oi = jnp.dot(att.astype(BF16), v.astype(BF16), preferred_element_type=F32)
        o_intra = jnp.sum(jnp.where(hm_o, oi.reshape(GLA_H, c, GLA_W), 0.0), axis=0)
        o_inter = lax.dot_general(qe.astype(BF16), st.astype(BF16), nt, preferred_element_type=F32)
        o_sc[r, :] = o_intra + o_inter
        ut = jnp.dot(v.T.astype(BF16), kd, preferred_element_type=F32)
        st_sc[...] = st * jnp.exp(b_last) + jnp.where(hm_s, ut, 0.0)

    st_ref[0] = st_sc[...]
    gr = pg[:, 2 * kw + GLA_W + LANE:]
    o_ref[...] = _head_norm_gate(o_sc[...], gr, g_ref, seg_ref)


def _gla_prompt(pg, gp, *, n_seq, seq_len):
    nj = seq_len // GLA_TG
    kw = GLA_H * GLA_DK
    row = lambda b, j: (b * nj + j, 0)
    return pl.pallas_call(
        _gla_kernel,
        grid=(n_seq, nj),
        in_specs=[pl.BlockSpec((GLA_TG, PG_W), row),
                  _const_spec((LANE, kw)), _const_spec((1, kw)), _const_spec((1, GLA_W)),
                  _const_spec((GLA_W, GLA_W)), _const_spec((GLA_CHUNK, GLA_CHUNK))],
        out_specs=[pl.BlockSpec((GLA_TG, GLA_W), row),
                   pl.BlockSpec((1, GLA_W, kw), lambda b, j: (b, 0, 0))],
        out_shape=[jax.ShapeDtypeStruct((n_seq * seq_len, GLA_W), F32),
                   jax.ShapeDtypeStruct((n_seq, GLA_W, kw), F32)],
        scratch_shapes=[pltpu.VMEM((GLA_W, kw), F32), pltpu.VMEM((GLA_TG, GLA_W), F32)],
        compiler_params=_cparams(("parallel", "arbitrary")),
        name="gla",
    )(pg, gp["wgate"], gp["bgate"], gp["g"], gp["seg"], gp["tri"])


def _gla_step_kernel(pg_ref, s_ref, wgate_ref, bgate_ref, g_ref, seg_ref, o_ref, snew_ref):
    kw = GLA_H * GLA_DK
    pg = pg_ref[...]
    n = pg.shape[0]
    q = pg[:, :kw] * (GLA_DK ** -0.5)
    k = pg[:, kw:2 * kw]
    v = pg[:, 2 * kw:2 * kw + GLA_W]
    decay = jnp.exp(_gla_gates(pg[:, 2 * kw + GLA_W:2 * kw + GLA_W + LANE], wgate_ref, bgate_ref))
    q_t, k_t, d_t = q.T, k.T, decay.T
    o_rows = []
    for b in range(n):
        vexp = jnp.concatenate(
            [jnp.broadcast_to(v[b:b + 1, h * GLA_DV:(h + 1) * GLA_DV], (GLA_DK, GLA_DV))
             for h in range(GLA_H)], axis=0)
        s_new = d_t[:, b:b + 1] * s_ref[b] + k_t[:, b:b + 1] * vexp
        snew_ref[b] = s_new
        w = q_t[:, b:b + 1] * s_new
        o_rows.append(jnp.concatenate(
            [jnp.sum(w[h * GLA_DK:(h + 1) * GLA_DK], axis=0, keepdims=True) for h in range(GLA_H)],
            axis=1))
    o = jnp.concatenate(o_rows, axis=0)
    o_ref[...] = _head_norm_gate(o, pg[:, 2 * kw + GLA_W + LANE:], g_ref, seg_ref)


def _gla_step(pg, state, gp):
    n = pg.shape[0]
    kw = GLA_H * GLA_DK
    return pl.pallas_call(
        _gla_step_kernel,
        out_shape=[jax.ShapeDtypeStruct((n, GLA_W), F32),
                   jax.ShapeDtypeStruct((n, kw, GLA_DV), F32)],
        compiler_params=pltpu.CompilerParams(vmem_limit_bytes=VMEM_LIMIT),
        name="gla_step",
    )(pg, state, gp["wgate"], gp["bgate"], gp["g"], gp["seg"])


FFN_TM = 512


def _mix_residual(x_ref, os5_ref, omla_ref, ogla_ref, wout_ref):
    return (x_ref[...]
            + jnp.dot(os5_ref[...].astype(BF16), wout_ref[:S5_W], preferred_element_type=F32)
            + jnp.dot(omla_ref[...].astype(BF16), wout_ref[S5_W:S5_W + MLA_W],
                      preferred_element_type=F32)
            + jnp.dot(ogla_ref[...].astype(BF16), wout_ref[S5_W + MLA_W:],
                      preferred_element_type=F32))


def _ffn_kernel(x_ref, os5_ref, omla_ref, ogla_ref, wout_ref, gf_ref, win_ref, cw_ref, cb_ref, wo_ref,
                *rest, final_norm):
    if final_norm:
        gfin_ref, y_ref, buf_ref, hn_sc, acc_sc, carry_sc = rest
    else:
        y_ref, buf_ref, hn_sc, acc_sc, carry_sc = rest
    tm = x_ref.shape[0]

    @pl.when(pl.program_id(1) == 0)
    def _():
        carry_sc[...] = jnp.zeros_like(carry_sc)

    x1 = _mix_residual(x_ref, os5_ref, omla_ref, ogla_ref, wout_ref)
    y_ref[...] = x1
    hn_sc[...] = _rms(x1, gf_ref[...]).astype(BF16)
    sub = 8
    row = lax.broadcasted_iota(jnp.int32, (sub, FF_CHUNK), 0)

    for c in range(FF_NC):
        cols = slice(c * FF_CHUNK, (c + 1) * FF_CHUNK)
        hn = hn_sc[...]
        val = jnp.dot(hn, win_ref[:, cols], preferred_element_type=F32)
        gate = jnp.dot(hn, win_ref[:, D_FF + c * FF_CHUNK:D_FF + (c + 1) * FF_CHUNK],
                       preferred_element_type=F32)
        prev = carry_sc[:, cols]
        r1 = pltpu.roll(gate, 1, 0)
        r2 = pltpu.roll(gate, 2, 0)
        head1 = jnp.where(row == 0, prev[1:2], r1[:sub])
        head2 = jnp.where(row == 0, prev[0:1], jnp.where(row == 1, prev[1:2], r2[:sub]))
        g1 = jnp.concatenate([head1, r1[sub:]], axis=0)
        g2 = jnp.concatenate([head2, r2[sub:]], axis=0)
        conv = cb_ref[:, cols] + cw_ref[0:1, cols] * g2 + cw_ref[1:2, cols] * g1 + cw_ref[2:3, cols] * gate
        a = jax.nn.gelu(conv) * val
        part = jnp.dot(a.astype(BF16), wo_ref[cols, :], preferred_element_type=F32)
        if c == 0:
            acc_sc[...] = part
        else:
            acc_sc[...] += part
        carry_sc[0:2, cols] = gate[tm - 2:tm]

    y = y_ref[...] + acc_sc[...]
    if final_norm:
        y = _rms(y, gfin_ref[...])
    y_ref[...] = y
    buf_ref[0] = carry_sc[0:2, :]


def _ffn_prompt(x2d, os5_tm, omla, ogla, fp, *, n_seq, seq_len, final_g=None):
    tm = FFN_TM
    nj = seq_len // tm
    row = lambda b, j: (b * nj + j, 0)
    final_norm = final_g is not None
    in_specs = [pl.BlockSpec((tm, D_MODEL), row),
                pl.BlockSpec((tm, S5_W), lambda b, j: (j, b)),
                pl.BlockSpec((tm, MLA_W), row), pl.BlockSpec((tm, GLA_W), row),
                _resident_spec((D_MODEL, D_MODEL)), _const_spec((1, D_MODEL)),
                _resident_spec((D_MODEL, 2 * D_FF)), _const_spec((3, D_FF)), _const_spec((1, D_FF)),
                _resident_spec((D_FF, D_MODEL))]
    args = [x2d, os5_tm, omla, ogla, fp["wout"], fp["g"], fp["win"], fp["cw"], fp["cb"], fp["wo"]]
    if final_norm:
        in_specs.append(_const_spec((1, D_MODEL)))
        args.append(final_g)
    return pl.pallas_call(
        functools.partial(_ffn_kernel, final_norm=final_norm),
        grid=(n_seq, nj),
        in_specs=in_specs,
        out_specs=[pl.BlockSpec((tm, D_MODEL), row),
                   pl.BlockSpec((1, 2, D_FF), lambda b, j: (b, 0, 0))],
        out_shape=[jax.ShapeDtypeStruct((n_seq * seq_len, D_MODEL), F32),
                   jax.ShapeDtypeStruct((n_seq, 2, D_FF), F32)],
        scratch_shapes=[pltpu.VMEM((tm, D_MODEL), BF16), pltpu.VMEM((tm, D_MODEL), F32),
                        pltpu.VMEM((8, D_FF), F32)],
        compiler_params=_cparams(("parallel", "arbitrary")),
        name="ffn",
    )(*args)


def _ffn_step_kernel(x_ref, os5_ref, olat_ref, wuv_ref, ogla_ref, wout_ref, gf_ref, b0_ref, b1_ref,
                     wv_ref, wg_ref, cw_ref, cb_ref, wo_ref, *rest, final_norm):
    if final_norm:
        gfin_ref, y_ref, gate_ref, x1_sc, hn_sc, acc_sc = rest
    else:
        y_ref, gate_ref, x1_sc, hn_sc, acc_sc = rest
    c = pl.program_id(0)

    @pl.when(c == 0)
    def _():
        omla = jnp.dot(olat_ref[...].astype(BF16), wuv_ref[...], preferred_element_type=F32)
        x1 = (x_ref[...]
              + jnp.dot(os5_ref[...].astype(BF16), wout_ref[:S5_W], preferred_element_type=F32)
              + jnp.dot(omla.astype(BF16), wout_ref[S5_W:S5_W + MLA_W], preferred_element_type=F32)
              + jnp.dot(ogla_ref[...].astype(BF16), wout_ref[S5_W + MLA_W:],
                        preferred_element_type=F32))
        x1_sc[...] = x1
        hn_sc[...] = _rms(x1, gf_ref[...]).astype(BF16)
        acc_sc[...] = jnp.zeros_like(acc_sc)

    hn = hn_sc[...]
    val = jnp.dot(hn, wv_ref[...], preferred_element_type=F32)
    gate = jnp.dot(hn, wg_ref[...], preferred_element_type=F32)
    cw = cw_ref[...]
    conv = cb_ref[...] + cw[0:1] * b0_ref[...] + cw[1:2] * b1_ref[...] + cw[2:3] * gate
    a = jax.nn.gelu(conv) * val
    acc_sc[...] += jnp.dot(a.astype(BF16), wo_ref[...], preferred_element_type=F32)
    gate_ref[...] = gate

    @pl.when(c == FF_NC - 1)
    def _():
        y = x1_sc[...] + acc_sc[...]
        if final_norm:
            y = _rms(y, gfin_ref[...])
        y_ref[...] = y


def _ffn_step(x2d, os5, olat, wuv_bd, ogla, buf0, buf1, fp, *, final_g=None):
    n = x2d.shape[0]
    final_norm = final_g is not None
    col = lambda c: (0, c)
    in_specs = [_const_spec((n, D_MODEL)), _const_spec((n, S5_W)),
                _const_spec((n, MLA_H * MLA_L)), _const_spec((MLA_H * MLA_L, MLA_W)),
                _const_spec((n, GLA_W)), _const_spec((D_MODEL, D_MODEL)), _const_spec((1, D_MODEL)),
                pl.BlockSpec((n, FF_CHUNK), col), pl.BlockSpec((n, FF_CHUNK), col),
                pl.BlockSpec((D_MODEL, FF_CHUNK), col),
                pl.BlockSpec((D_MODEL, FF_CHUNK), lambda c: (0, FF_NC + c)),
                pl.BlockSpec((3, FF_CHUNK), col), pl.BlockSpec((1, FF_CHUNK), col),
                pl.BlockSpec((FF_CHUNK, D_MODEL), lambda c: (c, 0))]
    args = [x2d, os5, olat, wuv_bd, ogla, fp["wout"], fp["g"], buf0, buf1, fp["win"], fp["win"],
            fp["cw"], fp["cb"], fp["wo"]]
    if final_norm:
        in_specs.append(_const_spec((1, D_MODEL)))
        args.append(final_g)
    return pl.pallas_call(
        functools.partial(_ffn_step_kernel, final_norm=final_norm),
        grid=(FF_NC,),
        in_specs=in_specs,
        out_specs=[_const_spec((n, D_MODEL)), pl.BlockSpec((n, FF_CHUNK), lambda c: (0, c))],
        out_shape=[jax.ShapeDtypeStruct((n, D_MODEL), F32), jax.ShapeDtypeStruct((n, D_FF), F32)],
        scratch_shapes=[pltpu.VMEM((n, D_MODEL), F32), pltpu.VMEM((n, D_MODEL), BF16),
                        pltpu.VMEM((n, D_MODEL), F32)],
        compiler_params=_cparams(("arbitrary",)),
        name="ffn_step",
    )(*args)


def _rot_cols(w):
    half = MLA_ROPE // 2
    return jnp.concatenate([-w[..., half:], w[..., :half]], axis=-1)


def _pad_cols(w, width):
    return jnp.pad(w, [(0, 0)] * (w.ndim - 1) + [(0, width - w.shape[-1])])


def _layer_params(l, norm_mix_g, w_in, s5_a_re, s5_a_im, s5_log_dt, s5_b_re, s5_b_im, s5_c_re, s5_c_im,
                  s5_d, s5_w_glu, s5_b_glu, mla_q_norm_g, mla_w_qb, mla_kv_norm_g, mla_w_uk, mla_w_uv,
                  gla_w_gate, gla_b_gate, gla_norm_g, w_out, norm_ffn_g, w_ffn_in, ffn_conv_w, ffn_conv_b,
                  w_ffn_out):
    w = w_in[l]
    o = 0
    cols = {}
    for name, width in (("u", 256), ("cq", 256), ("ckv", 128), ("kpe", 32), ("gq", 128), ("gk", 128),
                        ("gv", 256), ("glow", 16), ("gr", 256)):
        cols[name] = w[:, o:o + width]
        o += width
    w_ext = jnp.concatenate(
        [cols["u"], cols["cq"], cols["ckv"], _pad_cols(cols["kpe"], LANE),
         _pad_cols(_rot_cols(cols["kpe"]), LANE), cols["gq"], cols["gk"], cols["gv"],
         _pad_cols(cols["glow"], LANE), cols["gr"]], axis=1).astype(BF16)
    inp = dict(g=norm_mix_g[l][None], w=w_ext)

    a_re, a_im = s5_a_re[l], s5_a_im[l]
    dt = jnp.exp(s5_log_dt[l])[:, None]
    mag = jnp.exp(a_re * dt)
    ab_re, ab_im = mag * jnp.cos(a_im * dt), mag * jnp.sin(a_im * dt)
    den = a_re * a_re + a_im * a_im
    k_re = ((ab_re - 1.0) * a_re + ab_im * a_im) / den
    k_im = (ab_im * a_re - (ab_re - 1.0) * a_im) / den
    bb_re = k_re[..., None] * s5_b_re[l] - k_im[..., None] * s5_b_im[l]
    bb_im = k_re[..., None] * s5_b_im[l] + k_im[..., None] * s5_b_re[l]
    eye = jnp.eye(S5_G, dtype=F32)
    bd_in = lambda m: jnp.einsum("gpc,gh->gchp", m, eye).reshape(S5_W, S5_N)
    bd_out = lambda m: jnp.einsum("gcp,gh->gphc", m, eye).reshape(S5_N, S5_W)
    s5p = dict(
        a=jnp.stack([ab_re.reshape(S5_N), ab_im.reshape(S5_N)]),
        bm=jnp.concatenate([bd_in(bb_re), bd_in(bb_im)], axis=1).astype(BF16),
        cm=jnp.concatenate([bd_out(s5_c_re[l]), -bd_out(s5_c_im[l])], axis=0).astype(BF16),
        d=s5_d[l][None], wglu=s5_w_glu[l].astype(BF16), bglu=s5_b_glu[l][None])

    wqb = mla_w_qb[l].reshape(MLA_QL, MLA_H, MLA_NOPE + MLA_ROPE)
    w_nope = wqb[:, :, :MLA_NOPE].reshape(MLA_QL, MLA_H * MLA_NOPE)
    w_rope = wqb[:, :, MLA_NOPE:]
    wq = jnp.concatenate(
        [w_nope, _pad_cols(w_rope, LANE).reshape(MLA_QL, MLA_H * LANE),
         _pad_cols(_rot_cols(w_rope), LANE).reshape(MLA_QL, MLA_H * LANE)], axis=1).astype(BF16)
    eye_h = jnp.eye(MLA_H, dtype=F32)
    wuk = jnp.einsum("lhn,hk->hnkl", mla_w_uk[l], eye_h).reshape(MLA_H * MLA_NOPE, MLA_H * MLA_L)
    wuv = jnp.einsum("lhv,hk->hlkv", mla_w_uv[l], eye_h).reshape(MLA_H, MLA_L, MLA_W)
    mlap = dict(gq=mla_q_norm_g[l][None], wq=wq, wuk=wuk.astype(BF16), gkv=mla_kv_norm_g[l][None],
                wuv=wuv.astype(BF16))

    seg = (jnp.arange(GLA_W)[:, None] // GLA_DV == jnp.arange(GLA_W)[None, :] // GLA_DV)
    tri = jnp.arange(GLA_CHUNK)[None, :] <= jnp.arange(GLA_CHUNK)[:, None]
    glap = dict(wgate=jnp.pad(gla_w_gate[l], ((0, LANE - GLA_R), (0, 0))).astype(BF16),
                bgate=gla_b_gate[l][None], g=gla_norm_g[l][None],
                seg=seg.astype(F32) / GLA_DV, tri=tri.astype(F32))

    ffnp = dict(wout=w_out[l].astype(BF16), g=norm_ffn_g[l][None], win=w_ffn_in[l].astype(BF16),
                cw=ffn_conv_w[l], cb=ffn_conv_b[l][None], wo=w_ffn_out[l].astype(BF16))
    return inp, s5p, mlap, glap, ffnp


def _rope_tables(pos):
    half = MLA_ROPE // 2
    inv = ROPE_THETA ** (-jnp.arange(half, dtype=F32) / half)
    ang = pos.astype(F32)[:, None] * inv[None, :]
    cos = jnp.cos(ang)
    sin = jnp.sin(ang)
    return (_pad_cols(jnp.concatenate([cos, cos], axis=1), LANE),
            _pad_cols(jnp.concatenate([sin, sin], axis=1), LANE))


def _s5_state_out(hfin):
    n = hfin.shape[0]
    return jnp.stack([hfin[:, :S5_N].reshape(n, S5_G, S5_P), hfin[:, S5_N:].reshape(n, S5_G, S5_P)],
                     axis=-1)


def kernel(x_prompt, x_sample, cache_mla_ckv, cache_mla_krope, page_table, state_s5, state_gla, state_ffn_conv, norm_mix_g, w_in, s5_a_re, s5_a_im, s5_log_dt, s5_b_re, s5_b_im, s5_c_re, s5_c_im, s5_d, s5_w_glu, s5_b_glu, mla_q_norm_g, mla_w_qb, mla_kv_norm_g, mla_w_uk, mla_w_uv, gla_w_gate, gla_b_gate, gla_norm_g, w_out, norm_ffn_g, w_ffn_in, ffn_conv_w, ffn_conv_b, w_ffn_out, norm_final_g):
    bp, sp = x_prompt.shape[:2]
    bs = x_sample.shape[0]
    n_pages = page_table.shape[1]
    past_len = n_pages * cache_mla_ckv.shape[2]
    kw = GLA_H * GLA_DK

    cos_p, sin_p = _rope_tables(jnp.arange(sp))
    cos_s, sin_s = _rope_tables(jnp.full((bs,), past_len))
    gfin = norm_final_g[None]
    krope_t = jnp.swapaxes(cache_mla_krope, 2, 3)
    xp = x_prompt.reshape(bp * sp, D_MODEL)
    xd = x_sample.reshape(bs, D_MODEL)
    outs = [[] for _ in range(10)]

    for l in range(DEPTH):
        inp, s5p, mlap, glap, ffnp = _layer_params(
            l, norm_mix_g, w_in, s5_a_re, s5_a_im, s5_log_dt, s5_b_re, s5_b_im, s5_c_re, s5_c_im, s5_d,
            s5_w_glu, s5_b_glu, mla_q_norm_g, mla_w_qb, mla_kv_norm_g, mla_w_uk, mla_w_uv, gla_w_gate,
            gla_b_gate, gla_norm_g, w_out, norm_ffn_g, w_ffn_in, ffn_conv_w, ffn_conv_b, w_ffn_out)
        last = l == DEPTH - 1

        u, pm, pg = _in_proj(xp, inp["g"], inp["w"], n_seq=bp, seq_len=sp, tm=512)
        o_s5, hfin = _s5(u.reshape(sp * bp, S5_W), jnp.zeros((bp, 2 * S5_N), F32), s5p,
                         r_rows=bp, t_total=sp, t_steps=128)
        qcat, kcat, ckv, kpe = _mla_prep(pm, cos_p, sin_p, mlap, n_seq=bp, seq_len=sp, tm=512,
                                         q_dtype=BF16)
        o_mla = _attn_prompt(qcat, kcat, mlap["wuv"], n_seq=bp, seq_len=sp)
        o_gla, st = _gla_prompt(pg, glap, n_seq=bp, seq_len=sp)
        xp, cbuf = _ffn_prompt(xp, o_s5.reshape(sp, bp * S5_W), o_mla, o_gla, ffnp, n_seq=bp,
                               seq_len=sp, final_g=gfin if last else None)
        outs[0].append(ckv.reshape(bp, sp // PAGE, PAGE, MLA_L))
        outs[1].append(kpe.reshape(bp, sp // PAGE, PAGE, MLA_ROPE))
        outs[2].append(_s5_state_out(hfin))
        st5 = st.reshape(bp, GLA_H, GLA_DV, GLA_H, GLA_DK)
        outs[3].append(jnp.stack([st5[:, h, :, h, :] for h in range(GLA_H)], axis=1).swapaxes(2, 3))
        outs[4].append(cbuf)

        u, pm, pg = _in_proj(xd, inp["g"], inp["w"], n_seq=1, seq_len=bs, tm=bs)
        h0 = jnp.concatenate([state_s5[l, ..., 0].reshape(bs, S5_N), state_s5[l, ..., 1].reshape(bs, S5_N)],
                             axis=1)
        o_s5, hfin = _s5(u, h0, s5p, r_rows=bs, t_total=1, t_steps=1)
        qcat, kcat, ckv, kpe = _mla_prep(pm, cos_s, sin_s, mlap, n_seq=1, seq_len=bs, tm=bs, q_dtype=F32)
        o_lat = _attn_decode(page_table, qcat.transpose(1, 0, 2), kcat[:, None, :], cache_mla_ckv,
                             krope_t, layer=l)
        o_gla, st = _gla_step(pg, state_gla[l].reshape(bs, kw, GLA_DV), glap)
        buf = state_ffn_conv[l]
        xd, gate = _ffn_step(xd, o_s5, o_lat.reshape(bs, MLA_H * MLA_L),
                             mlap["wuv"].reshape(MLA_H * MLA_L, MLA_W), o_gla, buf[:, 0], buf[:, 1], ffnp,
                             final_g=gfin if last else None)
        outs[5].append(ckv[:, None, :])
        outs[6].append(kpe[:, None, :])
        outs[7].append(_s5_state_out(hfin))
        outs[8].append(st.reshape(bs, GLA_H, GLA_DK, GLA_DV))
        outs[9].append(jnp.stack([buf[:, 1], gate], axis=1))

    return (xp.reshape(bp, sp, D_MODEL), xd.reshape(bs, 1, D_MODEL)) + tuple(jnp.stack(o) for o in outs)
```

```python
import functools
import math

import jax
import jax.numpy as jnp
from jax import lax
from jax.experimental import pallas as pl
from jax.experimental.pallas import tpu as pltpu

F32 = jnp.float32
BF16 = jnp.bfloat16

D_MODEL = 1024
DEPTH = 4
PAGE = 128
S5_W = 256
S5_G = 16
S5_C = 16
S5_P = 64
S5_N = S5_G * S5_P
MLA_H = 8
MLA_NOPE = 64
MLA_ROPE = 32
MLA_V = 64
MLA_W = MLA_H * MLA_V
MLA_QL = 256
MLA_L = 128
MLA_SCALE = (MLA_NOPE + MLA_ROPE) ** -0.5
Q_SCALE = MLA_SCALE * math.log2(math.e)
ROPE_THETA = 10000.0
GLA_H = 4
GLA_W = 256
GLA_DV = 64
GLA_DK = 32
GLA_R = 16
GLA_NORM = 16.0
GLA_CHUNK = 64
D_FF = 2816
RMS_EPS = 1e-6
NEG_INF = -1e30

LANE = 128
PM_W = 640
PG_W = 896
W_EXT = S5_W + PM_W + PG_W
QA_W = MLA_H * MLA_NOPE + 2 * MLA_H * LANE
KC_W = 2 * LANE

FF_CHUNK = 256
FF_NC = D_FF // FF_CHUNK
VMEM_LIMIT = 56 * 1024 * 1024


def _cparams(sem, vmem=VMEM_LIMIT):
    return pltpu.CompilerParams(dimension_semantics=sem, vmem_limit_bytes=vmem)


def _rms(x, g):
    ms = jnp.mean(x * x, axis=-1, keepdims=True)
    return x * lax.rsqrt(ms + RMS_EPS) * g


def _const_spec(shape):
    nd = len(shape)
    return pl.BlockSpec(shape, lambda *_: (0,) * nd)


def _resident_spec(shape):
    nd = len(shape)
    return pl.BlockSpec(shape, lambda *_: (0,) * nd, pipeline_mode=pl.Buffered(1))


def _in_proj_kernel(x_ref, g_ref, w_ref, u_ref, pm_ref, pg_ref):
    h = _rms(x_ref[...], g_ref[...]).astype(BF16)
    u_ref[...] = jnp.dot(h, w_ref[:, :S5_W], preferred_element_type=F32)
    pm_ref[...] = jnp.dot(h, w_ref[:, S5_W:S5_W + PM_W], preferred_element_type=F32)
    pg_ref[...] = jnp.dot(h, w_ref[:, S5_W + PM_W:], preferred_element_type=F32)


def _in_proj(x2d, g, w_ext, *, n_seq, seq_len, tm):
    n = n_seq * seq_len
    nj = seq_len // tm
    row = lambda b, j: (b * nj + j, 0)
    return pl.pallas_call(
        _in_proj_kernel,
        grid=(n_seq, nj),
        in_specs=[pl.BlockSpec((tm, D_MODEL), row), _const_spec((1, D_MODEL)),
                  _const_spec((D_MODEL, W_EXT))],
        out_specs=[pl.BlockSpec((tm, S5_W), lambda b, j: (j, b)),
                   pl.BlockSpec((tm, PM_W), row), pl.BlockSpec((tm, PG_W), row)],
        out_shape=[jax.ShapeDtypeStruct((seq_len, n_seq * S5_W), F32),
                   jax.ShapeDtypeStruct((n, PM_W), F32),
                   jax.ShapeDtypeStruct((n, PG_W), F32)],
        compiler_params=_cparams(("parallel", "parallel")),
        name="in_proj",
    )(x2d, g, w_ext)


def _s5_kernel(u_ref, h0_ref, a_ref, bm_ref, cm_ref, d_ref, wg_ref, bg_ref,
               o_ref, hfin_ref, h_sc, bu_sc, *, r_rows, t_steps):
    @pl.when(pl.program_id(0) == 0)
    def _():
        h_sc[...] = h0_ref[...]

    u = u_ref[...]
    bu_sc[...] = jnp.dot(u.astype(BF16), bm_ref[...], preferred_element_type=F32)
    ar = jnp.broadcast_to(a_ref[0:1, :], (r_rows, S5_N))
    ai = jnp.broadcast_to(a_ref[1:2, :], (r_rows, S5_N))

    def step(t, carry):
        hr, hi = carry
        r0 = pl.multiple_of(t * r_rows, r_rows)
        nr = ar * hr - ai * hi + bu_sc[pl.ds(r0, r_rows), :S5_N]
        ni = ar * hi + ai * hr + bu_sc[pl.ds(r0, r_rows), S5_N:]
        bu_sc[pl.ds(r0, r_rows), :S5_N] = nr
        bu_sc[pl.ds(r0, r_rows), S5_N:] = ni
        return nr, ni

    hr, hi = lax.fori_loop(0, t_steps, step, (h_sc[:, :S5_N], h_sc[:, S5_N:]))
    h_sc[:, :S5_N] = hr
    h_sc[:, S5_N:] = hi
    hfin_ref[:, :S5_N] = hr
    hfin_ref[:, S5_N:] = hi

    y = jnp.dot(bu_sc[...].astype(BF16), cm_ref[...], preferred_element_type=F32) + d_ref[...] * u
    y = jax.nn.gelu(y)
    z = jnp.dot(y.astype(BF16), wg_ref[...], preferred_element_type=F32) + bg_ref[...]
    o_ref[...] = y * jax.nn.sigmoid(z)


def _s5(u_tm, h0, sp, *, r_rows, t_total, t_steps):
    rows = r_rows * t_steps
    kern = functools.partial(_s5_kernel, r_rows=r_rows, t_steps=t_steps)
    return pl.pallas_call(
        kern,
        grid=(t_total // t_steps,),
        in_specs=[pl.BlockSpec((rows, S5_W), lambda i: (i, 0)),
                  _const_spec((r_rows, 2 * S5_N)), _const_spec((2, S5_N)),
                  _const_spec((S5_W, 2 * S5_N)), _const_spec((2 * S5_N, S5_W)),
                  _const_spec((1, S5_W)), _const_spec((S5_W, S5_W)), _const_spec((1, S5_W))],
        out_specs=[pl.BlockSpec((rows, S5_W), lambda i: (i, 0)),
                   _const_spec((r_rows, 2 * S5_N))],
        out_shape=[jax.ShapeDtypeStruct((t_total * r_rows, S5_W), F32),
                   jax.ShapeDtypeStruct((r_rows, 2 * S5_N), F32)],
        scratch_shapes=[pltpu.VMEM((r_rows, 2 * S5_N), F32),
                        pltpu.VMEM((rows, 2 * S5_N), F32)],
        compiler_params=_cparams(("arbitrary",)),
        name="s5",
    )(u_tm, h0, sp["a"], sp["bm"], sp["cm"], sp["d"], sp["wglu"], sp["bglu"])


def _mla_prep_kernel(pm_ref, cos_ref, sin_ref, gq_ref, wq_ref, wuk_ref, gkv_ref,
                     qcat_ref, kcat_ref, ckv_ref, kpe_ref):
    pm = pm_ref[...]
    cos = cos_ref[...]
    sin = sin_ref[...]
    qn = _rms(pm[:, :MLA_QL], gq_ref[...]).astype(BF16)
    qa = jnp.dot(qn, wq_ref[...], preferred_element_type=F32)
    n_nope = MLA_H * MLA_NOPE
    qlat = jnp.dot(qa[:, :n_nope].astype(BF16), wuk_ref[...], preferred_element_type=F32)
    for h in range(MLA_H):
        lo = n_nope + h * LANE
        qpe = qa[:, lo:lo + LANE] * cos + qa[:, lo + MLA_H * LANE:lo + (MLA_H + 1) * LANE] * sin
        qcat_ref[h, :, :LANE] = (qlat[:, h * LANE:(h + 1) * LANE] * Q_SCALE).astype(qcat_ref.dtype)
        qcat_ref[h, :, LANE:] = (qpe * Q_SCALE).astype(qcat_ref.dtype)
    ckv = _rms(pm[:, MLA_QL:MLA_QL + MLA_L], gkv_ref[...])
    kpe = pm[:, MLA_QL + MLA_L:MLA_QL + 2 * MLA_L] * cos + pm[:, MLA_QL + 2 * MLA_L:] * sin
    ckv_ref[...] = ckv
    kpe_ref[...] = kpe[:, :MLA_ROPE]
    kcat_ref[:, :LANE] = ckv.astype(kcat_ref.dtype)
    kcat_ref[:, LANE:] = kpe.astype(kcat_ref.dtype)


def _mla_prep(pm, cos_t, sin_t, mp, *, n_seq, seq_len, tm, q_dtype):
    n = n_seq * seq_len
    nj = seq_len // tm
    row = lambda b, j: (b * nj + j, 0)
    pos = lambda b, j: (j, 0)
    return pl.pallas_call(
        _mla_prep_kernel,
        grid=(n_seq, nj),
        in_specs=[pl.BlockSpec((tm, PM_W), row),
                  pl.BlockSpec((tm, LANE), pos), pl.BlockSpec((tm, LANE), pos),
                  _const_spec((1, MLA_QL)), _const_spec((MLA_QL, QA_W)),
                  _const_spec((MLA_H * MLA_NOPE, MLA_H * LANE)), _const_spec((1, MLA_L))],
        out_specs=[pl.BlockSpec((MLA_H, tm, KC_W), lambda b, j: (0, b * nj + j, 0)),
                   pl.BlockSpec((tm, KC_W), row),
                   pl.BlockSpec((tm, MLA_L), row), pl.BlockSpec((tm, MLA_ROPE), row)],
        out_shape=[jax.ShapeDtypeStruct((MLA_H, n, KC_W), q_dtype),
                   jax.ShapeDtypeStruct((n, KC_W), q_dtype),
                   jax.ShapeDtypeStruct((n, MLA_L), F32),
                   jax.ShapeDtypeStruct((n, MLA_ROPE), F32)],
        compiler_params=_cparams(("parallel", "parallel")),
        name="mla_prep",
    )(pm, cos_t, sin_t, mp["gq"], mp["wq"], mp["wuk"], mp["gkv"])


ATT_TQ = 256
ATT_TK = 512
ATT_NG = 4


def _attn_kernel(q_ref, k_ref, wuv_ref, o_ref, m_sc, acc_sc):
    i = pl.program_id(1)
    rows = MLA_H * ATT_TQ
    hg = MLA_H // ATT_NG
    grows = hg * ATT_TQ
    m_sc[...] = jnp.full((rows, LANE), NEG_INF, F32)
    acc_sc[...] = jnp.zeros((rows, 2 * MLA_L), F32)
    ones = jnp.ones((ATT_TK, LANE), BF16)

    def tile(j, masked):
        kt = k_ref[pl.ds(pl.multiple_of(j * ATT_TK, ATT_TK), ATT_TK), :]
        v_ext = jnp.concatenate([kt[:, :MLA_L], ones], axis=1)
        if masked:
            qpos = i * ATT_TQ + lax.broadcasted_iota(jnp.int32, (ATT_TQ, ATT_TK), 0)
            kpos = j * ATT_TK + lax.broadcasted_iota(jnp.int32, (ATT_TQ, ATT_TK), 1)
            visible = (kpos <= qpos)[None]
        scores = []
        for g in range(ATT_NG):
            q = q_ref[g * hg:(g + 1) * hg].reshape(grows, KC_W)
            scores.append(lax.dot_general(q, kt, (((1,), (1,)), ((), ())), preferred_element_type=F32))
        for g in range(ATT_NG):
            r = slice(g * grows, (g + 1) * grows)
            s = scores[g]
            if masked:
                s = jnp.where(visible, s.reshape(hg, ATT_TQ, ATT_TK), NEG_INF).reshape(grows, ATT_TK)
            m_old = m_sc[r, :]
            m_new = jnp.maximum(m_old, jnp.max(s, axis=-1, keepdims=True))
            alpha = jnp.exp2(m_old - m_new)
            p = jnp.exp2(s - jnp.tile(m_new, (1, ATT_TK // LANE)))
            pv = jnp.dot(p.astype(kt.dtype), v_ext, preferred_element_type=F32)
            acc_sc[r, :] = jnp.tile(alpha, (1, 2)) * acc_sc[r, :] + pv
            m_sc[r, :] = m_new

    n_full = (i * ATT_TQ) // ATT_TK

    def body(j, c):
        tile(j, False)
        return c

    lax.fori_loop(0, n_full, body, 0)
    tile(n_full, True)
    acc = acc_sc[...]
    o_lat = (acc[:, :MLA_L] / acc[:, MLA_L:]).astype(wuv_ref.dtype)
    out = jnp.dot(o_lat[:ATT_TQ], wuv_ref[0], preferred_element_type=F32)
    for h in range(1, MLA_H):
        out += jnp.dot(o_lat[h * ATT_TQ:(h + 1) * ATT_TQ], wuv_ref[h], preferred_element_type=F32)
    o_ref[...] = out


def _attn_prompt(qcat, kcat, wuv, *, n_seq, seq_len):
    nq = seq_len // ATT_TQ
    rows = MLA_H * ATT_TQ
    return pl.pallas_call(
        _attn_kernel,
        grid=(n_seq, nq),
        in_specs=[pl.BlockSpec((MLA_H, ATT_TQ, KC_W), lambda b, i: (0, b * nq + i, 0)),
                  pl.BlockSpec((seq_len, KC_W), lambda b, i: (b, 0)),
                  _const_spec((MLA_H, MLA_L, MLA_W))],
        out_specs=pl.BlockSpec((ATT_TQ, MLA_W), lambda b, i: (b * nq + i, 0)),
        out_shape=jax.ShapeDtypeStruct((n_seq * seq_len, MLA_W), F32),
        scratch_shapes=[pltpu.VMEM((rows, LANE), F32), pltpu.VMEM((rows, 2 * MLA_L), F32)],
        compiler_params=_cparams(("parallel", "parallel")),
        name="mla_attn",
    )(qcat, kcat, wuv)


DEC_G = 64
DEC_NS = 8


def _page_copies(pt_ref, ckv_hbm, kpe_hbm, ckv_buf, kpe_buf, sems, layer, b, c, slot):
    copies = []
    for g in range(DEC_G):
        page = pt_ref[b, c * DEC_G + g]
        dst = pl.ds(g * PAGE, PAGE)
        copies.append(pltpu.make_async_copy(ckv_hbm.at[layer, page], ckv_buf.at[slot, dst, :],
                                            sems.at[0, slot]))
        copies.append(pltpu.make_async_copy(kpe_hbm.at[layer, page], kpe_buf.at[slot, :, dst],
                                            sems.at[1, slot]))
    return copies


def _decode_kernel(pt_ref, q_ref, knew_ref, ckv_hbm, kpe_hbm, o_ref, ckv_buf, kpe_buf, sems,
                   *, layer, n_chunks):
    b = pl.program_id(0)
    n_seq = pl.num_programs(0)
    copies = functools.partial(_page_copies, pt_ref, ckv_hbm, kpe_hbm, ckv_buf, kpe_buf, sems, layer)

    @pl.when(b == 0)
    def _():
        for cp in copies(b, 0, 0):
            cp.start()

    q = q_ref[0]
    nt = (((1,), (1,)), ((), ()))
    q_lat = q[:, :MLA_L].astype(BF16)
    q_pe = q[:, MLA_L:MLA_L + MLA_ROPE].astype(BF16)
    sub = DEC_G * PAGE // DEC_NS
    blocks = [pl.ds(i * sub, sub) for i in range(DEC_NS)]
    m = jnp.full((MLA_H, 1), NEG_INF, F32)
    l = jnp.zeros((MLA_H, 1), F32)
    acc = jnp.zeros((MLA_H, MLA_L), F32)
    for c in range(n_chunks):
        slot = c % 2
        for cp in copies(b, c, slot):
            cp.wait()
        if c + 1 < n_chunks:
            nxt = copies(b, c + 1, 1 - slot)
        else:
            nxt = copies(lax.rem(b + 1, n_seq), 0, 1 - slot)
        for cp in nxt:
            cp.start()
        ck = [ckv_buf[slot, blk, :].astype(BF16) for blk in blocks]
        s_lat = [lax.dot_general(q_lat, k, nt, preferred_element_type=F32) for k in ck]
        s_pe = [jnp.dot(q_pe, kpe_buf[slot, :, blk].astype(BF16), preferred_element_type=F32)
                for blk in blocks]
        s = [x + y for x, y in zip(s_lat, s_pe)]
        m_new = m
        for x in s:
            m_new = jnp.maximum(m_new, jnp.max(x, axis=-1, keepdims=True))
        alpha = jnp.exp2(m - m_new)
        p = [jnp.exp2(x - m_new) for x in s]
        pv = [jnp.dot(x.astype(BF16), k, preferred_element_type=F32) for x, k in zip(p, ck)]
        l = alpha * l + sum(jnp.sum(x, axis=-1, keepdims=True) for x in p)
        acc = alpha * acc + sum(pv)
        m = m_new

    kn = knew_ref[0]
    s_new = jnp.sum(q * kn, axis=-1, keepdims=True)
    m_fin = jnp.maximum(m, s_new)
    a_fin = jnp.exp2(m - m_fin)
    p_new = jnp.exp2(s_new - m_fin)
    o_ref[0] = (a_fin * acc + p_new * kn[:, :MLA_L]) / (a_fin * l + p_new)

    @pl.when(b == n_seq - 1)
    def _():
        for cp in copies(0, 0, n_chunks % 2):
            cp.wait()


def _attn_decode(page_table, q_dec, knew, cache_ckv, cache_kpe, *, layer):
    n_seq, n_pages = page_table.shape
    n_chunks = n_pages // DEC_G
    assert n_chunks % 2 == 0, "buffer roles must repeat from one sequence to the next"
    keys = DEC_G * PAGE
    kern = functools.partial(_decode_kernel, layer=layer, n_chunks=n_chunks)
    grid_spec = pltpu.PrefetchScalarGridSpec(
        num_scalar_prefetch=1,
        grid=(n_seq,),
        in_specs=[pl.BlockSpec((1, MLA_H, KC_W), lambda b, pt: (b, 0, 0)),
                  pl.BlockSpec((1, 1, KC_W), lambda b, pt: (b, 0, 0)),
                  pl.BlockSpec(memory_space=pl.ANY), pl.BlockSpec(memory_space=pl.ANY)],
        out_specs=pl.BlockSpec((1, MLA_H, MLA_L), lambda b, pt: (b, 0, 0)),
        scratch_shapes=[pltpu.VMEM((2, keys, MLA_L), F32), pltpu.VMEM((2, MLA_ROPE, keys), F32),
                        pltpu.SemaphoreType.DMA((2, 2))])
    return pl.pallas_call(
        kern,
        grid_spec=grid_spec,
        out_shape=jax.ShapeDtypeStruct((n_seq, MLA_H, MLA_L), F32),
        compiler_params=_cparams(("arbitrary",)),
        name="mla_decode",
    )(page_table, q_dec, knew, cache_ckv, cache_kpe)


GLA_TG = 512


def _head_norm_gate(o, gr, g_ref, seg_ref):
    ms = jnp.dot(o * o, seg_ref[...], preferred_element_type=F32, precision=lax.Precision.HIGHEST)
    return o * lax.rsqrt(ms + RMS_EPS) * g_ref[...] * (gr * jax.nn.sigmoid(gr))


def _gla_gates(glow, wgate_ref, bgate_ref):
    z = jnp.dot(glow.astype(BF16), wgate_ref[...], preferred_element_type=F32) + bgate_ref[...]
    return jax.nn.log_sigmoid(z) / GLA_NORM


def _gla_kernel(pg_ref, wgate_ref, bgate_ref, g_ref, seg_ref, tri_ref, o_ref, st_ref, st_sc, o_sc):
    @pl.when(pl.program_id(1) == 0)
    def _():
        st_sc[...] = jnp.zeros_like(st_sc)

    kw = GLA_H * GLA_DK
    c = GLA_CHUNK
    pg = pg_ref[...]
    logg = _gla_gates(pg[:, 2 * kw + GLA_W:2 * kw + GLA_W + LANE], wgate_ref, bgate_ref)
    hm_q = (lax.broadcasted_iota(jnp.int32, (GLA_H, c, kw), 2) // GLA_DK
            == lax.broadcasted_iota(jnp.int32, (GLA_H, c, kw), 0))
    hm_o = (lax.broadcasted_iota(jnp.int32, (GLA_H, c, GLA_W), 2) // GLA_DV
            == lax.broadcasted_iota(jnp.int32, (GLA_H, c, GLA_W), 0))
    hm_s = (lax.broadcasted_iota(jnp.int32, (GLA_W, kw), 0) // GLA_DV
            == lax.broadcasted_iota(jnp.int32, (GLA_W, kw), 1) // GLA_DK)
    causal = (lax.broadcasted_iota(jnp.int32, (c, c), 1) <= lax.broadcasted_iota(jnp.int32, (c, c), 0))
    nt = (((1,), (1,)), ((), ()))
    chunks = [slice(ci * c, (ci + 1) * c) for ci in range(GLA_TG // c)]
    bcum = [jnp.dot(tri_ref[...], logg[r], preferred_element_type=F32, precision=lax.Precision.HIGHEST)
            for r in chunks]
    v16 = [pg[r, 2 * kw:2 * kw + GLA_W].astype(BF16) for r in chunks]
    qe, att, ut, decay = [], [], [], []
    for r, b in zip(chunks, bcum):
        b_last = b[c - 1:c, :]
        k = pg[r, kw:2 * kw]
        qe_c = pg[r, :kw] * (GLA_DK ** -0.5) * jnp.exp(b)
        ke = (k * jnp.exp(-b)).astype(BF16)
        kd = (k * jnp.exp(b_last - b)).astype(BF16)
        qe.append(qe_c.astype(BF16))
        decay.append(jnp.exp(b_last))
        qs = jnp.where(hm_q, qe_c[None], 0.0).reshape(GLA_H * c, kw).astype(BF16)
        att.append(lax.dot_general(qs, ke, nt, preferred_element_type=F32))
        ut.append((pg[r, 2 * kw:2 * kw + GLA_W].T.astype(BF16), kd))
    ut = [jnp.dot(vt, kd, preferred_element_type=F32) for vt, kd in ut]
    oi = [jnp.dot(jnp.where(causal[None], a.reshape(GLA_H, c, c), 0.0).reshape(GLA_H * c, c).astype(BF16),
                  v, preferred_element_type=F32) for a, v in zip(att, v16)]
    st = st_sc[...]
    for r, q, o, u, d in zip(chunks, qe, oi, ut, decay):
        o_intra = jnp.sum(jnp.where(hm_o, o.reshape(GLA_H, c, GLA_W), 0.0), axis=0)
        o_inter = lax.dot_general(q, st.astype(BF16), nt, preferred_element_type=F32)
        o_sc[r, :] = o_intra + o_inter
        st = st * d + jnp.where(hm_s, u, 0.0)
    st_sc[...] = st
    st_ref[0] = st
    gr = pg[:, 2 * kw + GLA_W + LANE:]
    o_ref[...] = _head_norm_gate(o_sc[...], gr, g_ref, seg_ref)


def _gla_prompt(pg, gp, *, n_seq, seq_len):
    nj = seq_len // GLA_TG
    kw = GLA_H * GLA_DK
    row = lambda b, j: (b * nj + j, 0)
    return pl.pallas_call(
        _gla_kernel,
        grid=(n_seq, nj),
        in_specs=[pl.BlockSpec((GLA_TG, PG_W), row),
                  _const_spec((LANE, kw)), _const_spec((1, kw)), _const_spec((1, GLA_W)),
                  _const_spec((GLA_W, GLA_W)), _const_spec((GLA_CHUNK, GLA_CHUNK))],
        out_specs=[pl.BlockSpec((GLA_TG, GLA_W), row),
                   pl.BlockSpec((1, GLA_W, kw), lambda b, j: (b, 0, 0))],
        out_shape=[jax.ShapeDtypeStruct((n_seq * seq_len, GLA_W), F32),
                   jax.ShapeDtypeStruct((n_seq, GLA_W, kw), F32)],
        scratch_shapes=[pltpu.VMEM((GLA_W, kw), F32), pltpu.VMEM((GLA_TG, GLA_W), F32)],
        compiler_params=_cparams(("parallel", "arbitrary")),
        name="gla",
    )(pg, gp["wgate"], gp["bgate"], gp["g"], gp["seg"], gp["tri"])


def _gla_step_kernel(pg_ref, s_ref, wgate_ref, bgate_ref, g_ref, seg_ref, o_ref, snew_ref):
    kw = GLA_H * GLA_DK
    pg = pg_ref[...]
    n = pg.shape[0]
    q = pg[:, :kw] * (GLA_DK ** -0.5)
    k = pg[:, kw:2 * kw]
    v = pg[:, 2 * kw:2 * kw + GLA_W]
    decay = jnp.exp(_gla_gates(pg[:, 2 * kw + GLA_W:2 * kw + GLA_W + LANE], wgate_ref, bgate_ref))
    q_t, k_t, d_t = q.T, k.T, decay.T
    o_rows = []
    for b in range(n):
        vexp = jnp.concatenate(
            [jnp.broadcast_to(v[b:b + 1, h * GLA_DV:(h + 1) * GLA_DV], (GLA_DK, GLA_DV))
             for h in range(GLA_H)], axis=0)
        s_new = d_t[:, b:b + 1] * s_ref[b] + k_t[:, b:b + 1] * vexp
        snew_ref[b] = s_new
        w = q_t[:, b:b + 1] * s_new
        o_rows.append(jnp.concatenate(
            [jnp.sum(w[h * GLA_DK:(h + 1) * GLA_DK], axis=0, keepdims=True) for h in range(GLA_H)],
            axis=1))
    o = jnp.concatenate(o_rows, axis=0)
    o_ref[...] = _head_norm_gate(o, pg[:, 2 * kw + GLA_W + LANE:], g_ref, seg_ref)


def _gla_step(pg, state, gp):
    n = pg.shape[0]
    kw = GLA_H * GLA_DK
    return pl.pallas_call(
        _gla_step_kernel,
        out_shape=[jax.ShapeDtypeStruct((n, GLA_W), F32),
                   jax.ShapeDtypeStruct((n, kw, GLA_DV), F32)],
        compiler_params=pltpu.CompilerParams(vmem_limit_bytes=VMEM_LIMIT),
        name="gla_step",
    )(pg, state, gp["wgate"], gp["bgate"], gp["g"], gp["seg"])


FFN_TM = 512


def _mix_residual(x_ref, os5_ref, omla_ref, ogla_ref, wout_ref):
    return (x_ref[...]
            + jnp.dot(os5_ref[...].astype(BF16), wout_ref[:S5_W], preferred_element_type=F32)
            + jnp.dot(omla_ref[...].astype(BF16), wout_ref[S5_W:S5_W + MLA_W],
                      preferred_element_type=F32)
            + jnp.dot(ogla_ref[...].astype(BF16), wout_ref[S5_W + MLA_W:],
                      preferred_element_type=F32))


def _ffn_kernel(x_ref, os5_ref, omla_ref, ogla_ref, wout_ref, gf_ref, win_ref, cw_ref, cb_ref, wo_ref,
                *rest, final_norm):
    if final_norm:
        gfin_ref, y_ref, buf_ref, hn_sc, acc_sc, carry_sc = rest
    else:
        y_ref, buf_ref, hn_sc, acc_sc, carry_sc = rest
    tm = x_ref.shape[0]

    @pl.when(pl.program_id(1) == 0)
    def _():
        carry_sc[...] = jnp.zeros_like(carry_sc)

    x1 = _mix_residual(x_ref, os5_ref, omla_ref, ogla_ref, wout_ref)
    y_ref[...] = x1
    hn_sc[...] = _rms(x1, gf_ref[...]).astype(BF16)
    sub = 8
    row = lax.broadcasted_iota(jnp.int32, (sub, FF_CHUNK), 0)

    def up_proj(c):
        hn = hn_sc[...]
        return (jnp.dot(hn, win_ref[:, c * FF_CHUNK:(c + 1) * FF_CHUNK], preferred_element_type=F32),
                jnp.dot(hn, win_ref[:, D_FF + c * FF_CHUNK:D_FF + (c + 1) * FF_CHUNK],
                        preferred_element_type=F32))

    ahead = up_proj(0)
    for c in range(FF_NC):
        cols = slice(c * FF_CHUNK, (c + 1) * FF_CHUNK)
        val, gate = ahead
        if c + 1 < FF_NC:
            ahead = up_proj(c + 1)
        prev = carry_sc[:, cols]
        r1 = pltpu.roll(gate, 1, 0)
        r2 = pltpu.roll(gate, 2, 0)
        head1 = jnp.where(row == 0, prev[1:2], r1[:sub])
        head2 = jnp.where(row == 0, prev[0:1], jnp.where(row == 1, prev[1:2], r2[:sub]))
        g1 = jnp.concatenate([head1, r1[sub:]], axis=0)
        g2 = jnp.concatenate([head2, r2[sub:]], axis=0)
        conv = cb_ref[:, cols] + cw_ref[0:1, cols] * g2 + cw_ref[1:2, cols] * g1 + cw_ref[2:3, cols] * gate
        a = jax.nn.gelu(conv) * val
        part = jnp.dot(a.astype(BF16), wo_ref[cols, :], preferred_element_type=F32)
        if c == 0:
            acc_sc[...] = part
        else:
            acc_sc[...] += part
        carry_sc[0:2, cols] = gate[tm - 2:tm]

    y = y_ref[...] + acc_sc[...]
    if final_norm:
        y = _rms(y, gfin_ref[...])
    y_ref[...] = y
    buf_ref[0] = carry_sc[0:2, :]


def _ffn_prompt(x2d, os5_tm, omla, ogla, fp, *, n_seq, seq_len, final_g=None):
    tm = FFN_TM
    nj = seq_len // tm
    row = lambda b, j: (b * nj + j, 0)
    final_norm = final_g is not None
    in_specs = [pl.BlockSpec((tm, D_MODEL), row),
                pl.BlockSpec((tm, S5_W), lambda b, j: (j, b)),
                pl.BlockSpec((tm, MLA_W), row), pl.BlockSpec((tm, GLA_W), row),
                _resident_spec((D_MODEL, D_MODEL)), _const_spec((1, D_MODEL)),
                _resident_spec((D_MODEL, 2 * D_FF)), _const_spec((3, D_FF)), _const_spec((1, D_FF)),
                _resident_spec((D_FF, D_MODEL))]
    args = [x2d, os5_tm, omla, ogla, fp["wout"], fp["g"], fp["win"], fp["cw"], fp["cb"], fp["wo"]]
    if final_norm:
        in_specs.append(_const_spec((1, D_MODEL)))
        args.append(final_g)
    return pl.pallas_call(
        functools.partial(_ffn_kernel, final_norm=final_norm),
        grid=(n_seq, nj),
        in_specs=in_specs,
        out_specs=[pl.BlockSpec((tm, D_MODEL), row),
                   pl.BlockSpec((1, 2, D_FF), lambda b, j: (b, 0, 0))],
        out_shape=[jax.ShapeDtypeStruct((n_seq * seq_len, D_MODEL), F32),
                   jax.ShapeDtypeStruct((n_seq, 2, D_FF), F32)],
        scratch_shapes=[pltpu.VMEM((tm, D_MODEL), BF16), pltpu.VMEM((tm, D_MODEL), F32),
                        pltpu.VMEM((8, D_FF), F32)],
        compiler_params=_cparams(("parallel", "arbitrary")),
        name="ffn",
    )(*args)


def _ffn_step_kernel(x_ref, os5_ref, olat_ref, wuv_ref, ogla_ref, wout_ref, gf_ref, b0_ref, b1_ref,
                     wv_ref, wg_ref, cw_ref, cb_ref, wo_ref, *rest, final_norm):
    if final_norm:
        gfin_ref, y_ref, gate_ref, x1_sc, hn_sc, acc_sc = rest
    else:
        y_ref, gate_ref, x1_sc, hn_sc, acc_sc = rest
    c = pl.program_id(0)

    @pl.when(c == 0)
    def _():
        omla = jnp.dot(olat_ref[...].astype(BF16), wuv_ref[...], preferred_element_type=F32)
        x1 = (x_ref[...]
              + jnp.dot(os5_ref[...].astype(BF16), wout_ref[:S5_W], preferred_element_type=F32)
              + jnp.dot(omla.astype(BF16), wout_ref[S5_W:S5_W + MLA_W], preferred_element_type=F32)
              + jnp.dot(ogla_ref[...].astype(BF16), wout_ref[S5_W + MLA_W:],
                        preferred_element_type=F32))
        x1_sc[...] = x1
        hn_sc[...] = _rms(x1, gf_ref[...]).astype(BF16)
        acc_sc[...] = jnp.zeros_like(acc_sc)

    hn = hn_sc[...]
    val = jnp.dot(hn, wv_ref[...], preferred_element_type=F32)
    gate = jnp.dot(hn, wg_ref[...], preferred_element_type=F32)
    cw = cw_ref[...]
    conv = cb_ref[...] + cw[0:1] * b0_ref[...] + cw[1:2] * b1_ref[...] + cw[2:3] * gate
    a = jax.nn.gelu(conv) * val
    acc_sc[...] += jnp.dot(a.astype(BF16), wo_ref[...], preferred_element_type=F32)
    gate_ref[...] = gate

    @pl.when(c == FF_NC - 1)
    def _():
        y = x1_sc[...] + acc_sc[...]
        if final_norm:
            y = _rms(y, gfin_ref[...])
        y_ref[...] = y


def _ffn_step(x2d, os5, olat, wuv_bd, ogla, buf0, buf1, fp, *, final_g=None):
    n = x2d.shape[0]
    final_norm = final_g is not None
    col = lambda c: (0, c)
    in_specs = [_const_spec((n, D_MODEL)), _const_spec((n, S5_W)),
                _const_spec((n, MLA_H * MLA_L)), _const_spec((MLA_H * MLA_L, MLA_W)),
                _const_spec((n, GLA_W)), _const_spec((D_MODEL, D_MODEL)), _const_spec((1, D_MODEL)),
                pl.BlockSpec((n, FF_CHUNK), col), pl.BlockSpec((n, FF_CHUNK), col),
                pl.BlockSpec((D_MODEL, FF_CHUNK), col),
                pl.BlockSpec((D_MODEL, FF_CHUNK), lambda c: (0, FF_NC + c)),
                pl.BlockSpec((3, FF_CHUNK), col), pl.BlockSpec((1, FF_CHUNK), col),
                pl.BlockSpec((FF_CHUNK, D_MODEL), lambda c: (c, 0))]
    args = [x2d, os5, olat, wuv_bd, ogla, fp["wout"], fp["g"], buf0, buf1, fp["win"], fp["win"],
            fp["cw"], fp["cb"], fp["wo"]]
    if final_norm:
        in_specs.append(_const_spec((1, D_MODEL)))
        args.append(final_g)
    return pl.pallas_call(
        functools.partial(_ffn_step_kernel, final_norm=final_norm),
        grid=(FF_NC,),
        in_specs=in_specs,
        out_specs=[_const_spec((n, D_MODEL)), pl.BlockSpec((n, FF_CHUNK), lambda c: (0, c))],
        out_shape=[jax.ShapeDtypeStruct((n, D_MODEL), F32), jax.ShapeDtypeStruct((n, D_FF), F32)],
        scratch_shapes=[pltpu.VMEM((n, D_MODEL), F32), pltpu.VMEM((n, D_MODEL), BF16),
                        pltpu.VMEM((n, D_MODEL), F32)],
        compiler_params=_cparams(("arbitrary",)),
        name="ffn_step",
    )(*args)


def _rot_cols(w):
    half = MLA_ROPE // 2
    return jnp.concatenate([-w[..., half:], w[..., :half]], axis=-1)


def _pad_cols(w, width):
    return jnp.pad(w, [(0, 0)] * (w.ndim - 1) + [(0, width - w.shape[-1])])


def _layer_params(l, norm_mix_g, w_in, s5_a_re, s5_a_im, s5_log_dt, s5_b_re, s5_b_im, s5_c_re, s5_c_im,
                  s5_d, s5_w_glu, s5_b_glu, mla_q_norm_g, mla_w_qb, mla_kv_norm_g, mla_w_uk, mla_w_uv,
                  gla_w_gate, gla_b_gate, gla_norm_g, w_out, norm_ffn_g, w_ffn_in, ffn_conv_w, ffn_conv_b,
                  w_ffn_out):
    w = w_in[l]
    o = 0
    cols = {}
    for name, width in (("u", 256), ("cq", 256), ("ckv", 128), ("kpe", 32), ("gq", 128), ("gk", 128),
                        ("gv", 256), ("glow", 16), ("gr", 256)):
        cols[name] = w[:, o:o + width]
        o += width
    w_ext = jnp.concatenate(
        [cols["u"], cols["cq"], cols["ckv"], _pad_cols(cols["kpe"], LANE),
         _pad_cols(_rot_cols(cols["kpe"]), LANE), cols["gq"], cols["gk"], cols["gv"],
         _pad_cols(cols["glow"], LANE), cols["gr"]], axis=1).astype(BF16)
    inp = dict(g=norm_mix_g[l][None], w=w_ext)

    a_re, a_im = s5_a_re[l], s5_a_im[l]
    dt = jnp.exp(s5_log_dt[l])[:, None]
    mag = jnp.exp(a_re * dt)
    ab_re, ab_im = mag * jnp.cos(a_im * dt), mag * jnp.sin(a_im * dt)
    den = a_re * a_re + a_im * a_im
    k_re = ((ab_re - 1.0) * a_re + ab_im * a_im) / den
    k_im = (ab_im * a_re - (ab_re - 1.0) * a_im) / den
    bb_re = k_re[..., None] * s5_b_re[l] - k_im[..., None] * s5_b_im[l]
    bb_im = k_re[..., None] * s5_b_im[l] + k_im[..., None] * s5_b_re[l]
    eye = jnp.eye(S5_G, dtype=F32)
    bd_in = lambda m: jnp.einsum("gpc,gh->gchp", m, eye).reshape(S5_W, S5_N)
    bd_out = lambda m: jnp.einsum("gcp,gh->gphc", m, eye).reshape(S5_N, S5_W)
    s5p = dict(
        a=jnp.stack([ab_re.reshape(S5_N), ab_im.reshape(S5_N)]),
        bm=jnp.concatenate([bd_in(bb_re), bd_in(bb_im)], axis=1).astype(BF16),
        cm=jnp.concatenate([bd_out(s5_c_re[l]), -bd_out(s5_c_im[l])], axis=0).astype(BF16),
        d=s5_d[l][None], wglu=s5_w_glu[l].astype(BF16), bglu=s5_b_glu[l][None])

    wqb = mla_w_qb[l].reshape(MLA_QL, MLA_H, MLA_NOPE + MLA_ROPE)
    w_nope = wqb[:, :, :MLA_NOPE].reshape(MLA_QL, MLA_H * MLA_NOPE)
    w_rope = wqb[:, :, MLA_NOPE:]
    wq = jnp.concatenate(
        [w_nope, _pad_cols(w_rope, LANE).reshape(MLA_QL, MLA_H * LANE),
         _pad_cols(_rot_cols(w_rope), LANE).reshape(MLA_QL, MLA_H * LANE)], axis=1).astype(BF16)
    eye_h = jnp.eye(MLA_H, dtype=F32)
    wuk = jnp.einsum("lhn,hk->hnkl", mla_w_uk[l], eye_h).reshape(MLA_H * MLA_NOPE, MLA_H * MLA_L)
    wuv = jnp.einsum("lhv,hk->hlkv", mla_w_uv[l], eye_h).reshape(MLA_H, MLA_L, MLA_W)
    mlap = dict(gq=mla_q_norm_g[l][None], wq=wq, wuk=wuk.astype(BF16), gkv=mla_kv_norm_g[l][None],
                wuv=wuv.astype(BF16))

    seg = (jnp.arange(GLA_W)[:, None] // GLA_DV == jnp.arange(GLA_W)[None, :] // GLA_DV)
    tri = jnp.arange(GLA_CHUNK)[None, :] <= jnp.arange(GLA_CHUNK)[:, None]
    glap = dict(wgate=jnp.pad(gla_w_gate[l], ((0, LANE - GLA_R), (0, 0))).astype(BF16),
                bgate=gla_b_gate[l][None], g=gla_norm_g[l][None],
                seg=seg.astype(F32) / GLA_DV, tri=tri.astype(F32))

    ffnp = dict(wout=w_out[l].astype(BF16), g=norm_ffn_g[l][None], win=w_ffn_in[l].astype(BF16),
                cw=ffn_conv_w[l], cb=ffn_conv_b[l][None], wo=w_ffn_out[l].astype(BF16))
    return inp, s5p, mlap, glap, ffnp


def _rope_tables(pos):
    half = MLA_ROPE // 2
    inv = ROPE_THETA ** (-jnp.arange(half, dtype=F32) / half)
    ang = pos.astype(F32)[:, None] * inv[None, :]
    cos = jnp.cos(ang)
    sin = jnp.sin(ang)
    return (_pad_cols(jnp.concatenate([cos, cos], axis=1), LANE),
            _pad_cols(jnp.concatenate([sin, sin], axis=1), LANE))


def _s5_state_out(hfin):
    n = hfin.shape[0]
    return jnp.stack([hfin[:, :S5_N].reshape(n, S5_G, S5_P), hfin[:, S5_N:].reshape(n, S5_G, S5_P)],
                     axis=-1)


def kernel(x_prompt, x_sample, cache_mla_ckv, cache_mla_krope, page_table, state_s5, state_gla, state_ffn_conv, norm_mix_g, w_in, s5_a_re, s5_a_im, s5_log_dt, s5_b_re, s5_b_im, s5_c_re, s5_c_im, s5_d, s5_w_glu, s5_b_glu, mla_q_norm_g, mla_w_qb, mla_kv_norm_g, mla_w_uk, mla_w_uv, gla_w_gate, gla_b_gate, gla_norm_g, w_out, norm_ffn_g, w_ffn_in, ffn_conv_w, ffn_conv_b, w_ffn_out, norm_final_g):
    bp, sp = x_prompt.shape[:2]
    bs = x_sample.shape[0]
    n_pages = page_table.shape[1]
    past_len = n_pages * cache_mla_ckv.shape[2]
    kw = GLA_H * GLA_DK

    cos_p, sin_p = _rope_tables(jnp.arange(sp))
    cos_s, sin_s = _rope_tables(jnp.full((bs,), past_len))
    gfin = norm_final_g[None]
    krope_t = jnp.swapaxes(cache_mla_krope, 2, 3)
    xp = x_prompt.reshape(bp * sp, D_MODEL)
    xd = x_sample.reshape(bs, D_MODEL)
    outs = [[] for _ in range(10)]

    for l in range(DEPTH):
        inp, s5p, mlap, glap, ffnp = _layer_params(
            l, norm_mix_g, w_in, s5_a_re, s5_a_im, s5_log_dt, s5_b_re, s5_b_im, s5_c_re, s5_c_im, s5_d,
            s5_w_glu, s5_b_glu, mla_q_norm_g, mla_w_qb, mla_kv_norm_g, mla_w_uk, mla_w_uv, gla_w_gate,
            gla_b_gate, gla_norm_g, w_out, norm_ffn_g, w_ffn_in, ffn_conv_w, ffn_conv_b, w_ffn_out)
        last = l == DEPTH - 1

        u, pm, pg = _in_proj(xp, inp["g"], inp["w"], n_seq=bp, seq_len=sp, tm=512)
        o_s5, hfin = _s5(u.reshape(sp * bp, S5_W), jnp.zeros((bp, 2 * S5_N), F32), s5p,
                         r_rows=bp, t_total=sp, t_steps=128)
        qcat, kcat, ckv, kpe = _mla_prep(pm, cos_p, sin_p, mlap, n_seq=bp, seq_len=sp, tm=512,
                                         q_dtype=BF16)
        o_mla = _attn_prompt(qcat, kcat, mlap["wuv"], n_seq=bp, seq_len=sp)
        o_gla, st = _gla_prompt(pg, glap, n_seq=bp, seq_len=sp)
        xp, cbuf = _ffn_prompt(xp, o_s5.reshape(sp, bp * S5_W), o_mla, o_gla, ffnp, n_seq=bp,
                               seq_len=sp, final_g=gfin if last else None)
        outs[0].append(ckv.reshape(bp, sp // PAGE, PAGE, MLA_L))
        outs[1].append(kpe.reshape(bp, sp // PAGE, PAGE, MLA_ROPE))
        outs[2].append(_s5_state_out(hfin))
        st5 = st.reshape(bp, GLA_H, GLA_DV, GLA_H, GLA_DK)
        outs[3].append(jnp.stack([st5[:, h, :, h, :] for h in range(GLA_H)], axis=1).swapaxes(2, 3))
        outs[4].append(cbuf)

        u, pm, pg = _in_proj(xd, inp["g"], inp["w"], n_seq=1, seq_len=bs, tm=bs)
        h0 = jnp.concatenate([state_s5[l, ..., 0].reshape(bs, S5_N), state_s5[l, ..., 1].reshape(bs, S5_N)],
                             axis=1)
        o_s5, hfin = _s5(u, h0, s5p, r_rows=bs, t_total=1, t_steps=1)
        qcat, kcat, ckv, kpe = _mla_prep(pm, cos_s, sin_s, mlap, n_seq=1, seq_len=bs, tm=bs, q_dtype=F32)
        o_lat = _attn_decode(page_table, qcat.transpose(1, 0, 2), kcat[:, None, :], cache_mla_ckv,
                             krope_t, layer=l)
        o_gla, st = _gla_step(pg, state_gla[l].reshape(bs, kw, GLA_DV), glap)
        buf = state_ffn_conv[l]
        xd, gate = _ffn_step(xd, o_s5, o_lat.reshape(bs, MLA_H * MLA_L),
                             mlap["wuv"].reshape(MLA_H * MLA_L, MLA_W), o_gla, buf[:, 0], buf[:, 1], ffnp,
                             final_g=gfin if last else None)
        outs[5].append(ckv[:, None, :])
        outs[6].append(kpe[:, None, :])
        outs[7].append(_s5_state_out(hfin))
        outs[8].append(st.reshape(bs, GLA_H, GLA_DK, GLA_DV))
        outs[9].append(jnp.stack([buf[:, 1], gate], axis=1))

    return (xp.reshape(bp, sp, D_MODEL), xd.reshape(bs, 1, D_MODEL)) + tuple(jnp.stack(o) for o in outs)
```

```python
import functools
import math

import jax
import jax.numpy as jnp
from jax import lax
from jax.experimental import pallas as pl
from jax.experimental.pallas import tpu as pltpu

F32 = jnp.float32
BF16 = jnp.bfloat16

D_MODEL = 1024
DEPTH = 4
PAGE = 128
S5_W = 256
S5_G = 16
S5_C = 16
S5_P = 64
S5_N = S5_G * S5_P
MLA_H = 8
MLA_NOPE = 64
MLA_ROPE = 32
MLA_V = 64
MLA_W = MLA_H * MLA_V
MLA_QL = 256
MLA_L = 128
MLA_SCALE = (MLA_NOPE + MLA_ROPE) ** -0.5
Q_SCALE = MLA_SCALE * math.log2(math.e)
ROPE_THETA = 10000.0
GLA_H = 4
GLA_W = 256
GLA_DV = 64
GLA_DK = 32
GLA_R = 16
GLA_NORM = 16.0
GLA_CHUNK = 64
D_FF = 2816
RMS_EPS = 1e-6
NEG_INF = -1e30

LANE = 128
PM_W = 640
PG_W = 896
W_EXT = S5_W + PM_W + PG_W
QA_W = MLA_H * MLA_NOPE + 2 * MLA_H * LANE
KC_W = 2 * LANE

FF_CHUNK = 256
FF_NC = D_FF // FF_CHUNK
VMEM_LIMIT = 56 * 1024 * 1024


def _cparams(sem, vmem=VMEM_LIMIT):
    return pltpu.CompilerParams(dimension_semantics=sem, vmem_limit_bytes=vmem)


def _rms(x, g):
    ms = jnp.mean(x * x, axis=-1, keepdims=True)
    return x * lax.rsqrt(ms + RMS_EPS) * g


def _const_spec(shape):
    nd = len(shape)
    return pl.BlockSpec(shape, lambda *_: (0,) * nd)


def _layer_spec(layer, shape, resident=False):
    nd = len(shape)
    return pl.BlockSpec((None,) + tuple(shape), lambda *_: (layer,) + (0,) * nd,
                        pipeline_mode=pl.Buffered(1) if resident else None)


def _in_proj_kernel(x_ref, g_ref, w_ref, u_ref, pm_ref, pg_ref):
    h = _rms(x_ref[...], g_ref[...]).astype(BF16)
    u_ref[...] = jnp.dot(h, w_ref[:, :S5_W], preferred_element_type=F32)
    pm_ref[...] = jnp.dot(h, w_ref[:, S5_W:S5_W + PM_W], preferred_element_type=F32)
    pg_ref[...] = jnp.dot(h, w_ref[:, S5_W + PM_W:], preferred_element_type=F32)


def _in_proj(x2d, prm, layer, *, tm):
    n = x2d.shape[0]
    row = lambda i: (i, 0)
    return pl.pallas_call(
        _in_proj_kernel,
        grid=(n // tm,),
        in_specs=[pl.BlockSpec((tm, D_MODEL), row), _layer_spec(layer, (1, D_MODEL)),
                  _layer_spec(layer, (D_MODEL, W_EXT))],
        out_specs=[pl.BlockSpec((tm, S5_W), row), pl.BlockSpec((tm, PM_W), row),
                   pl.BlockSpec((tm, PG_W), row)],
        out_shape=[jax.ShapeDtypeStruct((n, S5_W), F32), jax.ShapeDtypeStruct((n, PM_W), F32),
                   jax.ShapeDtypeStruct((n, PG_W), F32)],
        compiler_params=_cparams(("parallel",)),
        name="in_proj",
    )(x2d, prm["in_g"], prm["in_w"])


def _s5_kernel(u_ref, h0_ref, a_ref, bm_ref, cm_ref, d_ref, wg_ref, bg_ref,
               o_ref, hfin_ref, h_sc, bu_sc, *io_sc, r_rows, t_steps):
    @pl.when(pl.program_id(0) == 0)
    def _():
        h_sc[...] = h0_ref[...]

    n_slab = S5_W // LANE
    if t_steps == 1:
        u = u_ref[...]
    else:
        u_sc, y_sc = io_sc
        for r in range(r_rows):
            for k in range(n_slab):
                u_sc[k, pl.ds(r, t_steps, stride=r_rows), :] = u_ref[r, :, k * LANE:(k + 1) * LANE]
        u = jnp.concatenate([u_sc[k] for k in range(n_slab)], axis=1)
    bu_sc[...] = jnp.dot(u.astype(BF16), bm_ref[...], preferred_element_type=F32)
    ar = jnp.broadcast_to(a_ref[0:1, :], (r_rows, S5_N))
    ai = jnp.broadcast_to(a_ref[1:2, :], (r_rows, S5_N))

    def step(t, carry):
        hr, hi = carry
        r0 = pl.multiple_of(t * r_rows, r_rows)
        nr = ar * hr - ai * hi + bu_sc[pl.ds(r0, r_rows), :S5_N]
        ni = ar * hi + ai * hr + bu_sc[pl.ds(r0, r_rows), S5_N:]
        bu_sc[pl.ds(r0, r_rows), :S5_N] = nr
        bu_sc[pl.ds(r0, r_rows), S5_N:] = ni
        return nr, ni

    hr, hi = lax.fori_loop(0, t_steps, step, (h_sc[:, :S5_N], h_sc[:, S5_N:]))
    h_sc[:, :S5_N] = hr
    h_sc[:, S5_N:] = hi
    hfin_ref[:, :S5_N] = hr
    hfin_ref[:, S5_N:] = hi

    y = jnp.dot(bu_sc[...].astype(BF16), cm_ref[...], preferred_element_type=F32) + d_ref[...] * u
    y = jax.nn.gelu(y)
    z = jnp.dot(y.astype(BF16), wg_ref[...], preferred_element_type=F32) + bg_ref[...]
    out = y * jax.nn.sigmoid(z)
    if t_steps == 1:
        o_ref[...] = out
    else:
        for k in range(n_slab):
            y_sc[k] = out[:, k * LANE:(k + 1) * LANE]
        for r in range(r_rows):
            for k in range(n_slab):
                o_ref[r, :, k * LANE:(k + 1) * LANE] = y_sc[k, pl.ds(r, t_steps, stride=r_rows), :]


def _s5(u, h0, prm, layer, *, r_rows, t_total, t_steps):
    rows = r_rows * t_steps
    kern = functools.partial(_s5_kernel, r_rows=r_rows, t_steps=t_steps)
    if t_steps == 1:
        io_spec = pl.BlockSpec((r_rows, S5_W), lambda i: (0, 0))
        io_shape = (r_rows, S5_W)
        io_scratch = []
    else:
        io_spec = pl.BlockSpec((r_rows, t_steps, S5_W), lambda i: (0, i, 0))
        io_shape = (r_rows, t_total, S5_W)
        io_scratch = [pltpu.VMEM((S5_W // LANE, rows, LANE), F32)] * 2
    o, hfin = pl.pallas_call(
        kern,
        grid=(t_total // t_steps,),
        in_specs=[io_spec, _const_spec((r_rows, 2 * S5_N)), _layer_spec(layer, (2, S5_N)),
                  _layer_spec(layer, (S5_W, 2 * S5_N)), _layer_spec(layer, (2 * S5_N, S5_W)),
                  _layer_spec(layer, (1, S5_W)), _layer_spec(layer, (S5_W, S5_W)),
                  _layer_spec(layer, (1, S5_W))],
        out_specs=[io_spec, _const_spec((r_rows, 2 * S5_N))],
        out_shape=[jax.ShapeDtypeStruct(io_shape, F32),
                   jax.ShapeDtypeStruct((r_rows, 2 * S5_N), F32)],
        scratch_shapes=[pltpu.VMEM((r_rows, 2 * S5_N), F32),
                        pltpu.VMEM((rows, 2 * S5_N), F32)] + io_scratch,
        compiler_params=_cparams(("arbitrary",)),
        name="s5",
    )(u.reshape(io_shape), h0, prm["s5_a"], prm["s5_bm"], prm["s5_cm"], prm["s5_d"], prm["s5_wglu"],
      prm["s5_bglu"])
    return o.reshape(r_rows * t_total, S5_W), hfin


def _mla_prep_kernel(pm_ref, cos_ref, sin_ref, gq_ref, wq_ref, wuk_ref, gkv_ref,
                     qcat_ref, kcat_ref, ckv_ref, kpe_ref):
    pm = pm_ref[...]
    cos = cos_ref[...]
    sin = sin_ref[...]
    qn = _rms(pm[:, :MLA_QL], gq_ref[...]).astype(BF16)
    qa = jnp.dot(qn, wq_ref[...], preferred_element_type=F32)
    n_nope = MLA_H * MLA_NOPE
    qlat = jnp.dot(qa[:, :n_nope].astype(BF16), wuk_ref[...], preferred_element_type=F32)
    for h in range(MLA_H):
        lo = n_nope + h * LANE
        qpe = qa[:, lo:lo + LANE] * cos + qa[:, lo + MLA_H * LANE:lo + (MLA_H + 1) * LANE] * sin
        qcat_ref[h, :, :LANE] = (qlat[:, h * LANE:(h + 1) * LANE] * Q_SCALE).astype(qcat_ref.dtype)
        qcat_ref[h, :, LANE:] = (qpe * Q_SCALE).astype(qcat_ref.dtype)
    ckv = _rms(pm[:, MLA_QL:MLA_QL + MLA_L], gkv_ref[...])
    kpe = pm[:, MLA_QL + MLA_L:MLA_QL + 2 * MLA_L] * cos + pm[:, MLA_QL + 2 * MLA_L:] * sin
    ckv_ref[...] = ckv
    kpe_ref[...] = kpe[:, :MLA_ROPE]
    kcat_ref[:, :LANE] = ckv.astype(kcat_ref.dtype)
    kcat_ref[:, LANE:] = kpe.astype(kcat_ref.dtype)


def _mla_prep(pm, cos_t, sin_t, prm, layer, *, n_seq, seq_len, tm, q_dtype):
    n = n_seq * seq_len
    nj = seq_len // tm
    row = lambda b, j: (b * nj + j, 0)
    pos = lambda b, j: (j, 0)
    return pl.pallas_call(
        _mla_prep_kernel,
        grid=(n_seq, nj),
        in_specs=[pl.BlockSpec((tm, PM_W), row),
                  pl.BlockSpec((tm, LANE), pos), pl.BlockSpec((tm, LANE), pos),
                  _layer_spec(layer, (1, MLA_QL)), _layer_spec(layer, (MLA_QL, QA_W)),
                  _layer_spec(layer, (MLA_H * MLA_NOPE, MLA_H * LANE)), _layer_spec(layer, (1, MLA_L))],
        out_specs=[pl.BlockSpec((MLA_H, tm, KC_W), lambda b, j: (0, b * nj + j, 0)),
                   pl.BlockSpec((tm, KC_W), row),
                   pl.BlockSpec((tm, MLA_L), row), pl.BlockSpec((tm, MLA_ROPE), row)],
        out_shape=[jax.ShapeDtypeStruct((MLA_H, n, KC_W), q_dtype),
                   jax.ShapeDtypeStruct((n, KC_W), q_dtype),
                   jax.ShapeDtypeStruct((n, MLA_L), F32),
                   jax.ShapeDtypeStruct((n, MLA_ROPE), F32)],
        compiler_params=_cparams(("parallel", "parallel")),
        name="mla_prep",
    )(pm, cos_t, sin_t, prm["mla_gq"], prm["mla_wq"], prm["mla_wuk"], prm["mla_gkv"])


ATT_TQ = 256
ATT_TK = 512
ATT_NG = 4


def _attn_kernel(q_ref, k_ref, wuv_ref, o_ref, m_sc, acc_sc):
    i = pl.program_id(1)
    rows = MLA_H * ATT_TQ
    hg = MLA_H // ATT_NG
    grows = hg * ATT_TQ
    m_sc[...] = jnp.full((rows, LANE), NEG_INF, F32)
    acc_sc[...] = jnp.zeros((rows, 2 * MLA_L), F32)
    ones = jnp.ones((ATT_TK, LANE), BF16)

    def tile(j, masked):
        kt = k_ref[pl.ds(pl.multiple_of(j * ATT_TK, ATT_TK), ATT_TK), :]
        v_ext = jnp.concatenate([kt[:, :MLA_L], ones], axis=1)
        if masked:
            qpos = i * ATT_TQ + lax.broadcasted_iota(jnp.int32, (ATT_TQ, ATT_TK), 0)
            kpos = j * ATT_TK + lax.broadcasted_iota(jnp.int32, (ATT_TQ, ATT_TK), 1)
            visible = (kpos <= qpos)[None]
        scores = []
        for g in range(ATT_NG):
            q = q_ref[g * hg:(g + 1) * hg].reshape(grows, KC_W)
            scores.append(lax.dot_general(q, kt, (((1,), (1,)), ((), ())), preferred_element_type=F32))
        for g in range(ATT_NG):
            r = slice(g * grows, (g + 1) * grows)
            s = scores[g]
            if masked:
                s = jnp.where(visible, s.reshape(hg, ATT_TQ, ATT_TK), NEG_INF).reshape(grows, ATT_TK)
            m_old = m_sc[r, :]
            m_new = jnp.maximum(m_old, jnp.max(s, axis=-1, keepdims=True))
            alpha = jnp.exp2(m_old - m_new)
            p = jnp.exp2(s - jnp.tile(m_new, (1, ATT_TK // LANE)))
            pv = jnp.dot(p.astype(kt.dtype), v_ext, preferred_element_type=F32)
            acc_sc[r, :] = jnp.tile(alpha, (1, 2)) * acc_sc[r, :] + pv
            m_sc[r, :] = m_new

    n_full = (i * ATT_TQ) // ATT_TK

    def body(j, c):
        tile(j, False)
        return c

    lax.fori_loop(0, n_full, body, 0)
    tile(n_full, True)
    acc = acc_sc[...]
    o_lat = (acc[:, :MLA_L] / acc[:, MLA_L:]).astype(wuv_ref.dtype)
    out = jnp.dot(o_lat[:ATT_TQ], wuv_ref[0], preferred_element_type=F32)
    for h in range(1, MLA_H):
        out += jnp.dot(o_lat[h * ATT_TQ:(h + 1) * ATT_TQ], wuv_ref[h], preferred_element_type=F32)
    o_ref[...] = out


def _attn_prompt(qcat, kcat, prm, layer, *, n_seq, seq_len):
    nq = seq_len // ATT_TQ
    rows = MLA_H * ATT_TQ
    return pl.pallas_call(
        _attn_kernel,
        grid=(n_seq, nq),
        in_specs=[pl.BlockSpec((MLA_H, ATT_TQ, KC_W), lambda b, i: (0, b * nq + i, 0)),
                  pl.BlockSpec((seq_len, KC_W), lambda b, i: (b, 0)),
                  _layer_spec(layer, (MLA_H, MLA_L, MLA_W))],
        out_specs=pl.BlockSpec((ATT_TQ, MLA_W), lambda b, i: (b * nq + i, 0)),
        out_shape=jax.ShapeDtypeStruct((n_seq * seq_len, MLA_W), F32),
        scratch_shapes=[pltpu.VMEM((rows, LANE), F32), pltpu.VMEM((rows, 2 * MLA_L), F32)],
        compiler_params=_cparams(("parallel", "parallel")),
        name="mla_attn",
    )(qcat, kcat, prm["mla_wuv"])


DEC_G = 64
DEC_NS = 8


def _page_copies(pt_ref, ckv_hbm, kpe_hbm, ckv_buf, kpe_buf, sems, layer, b, c, slot):
    copies = []
    for g in range(DEC_G):
        page = pt_ref[b, c * DEC_G + g]
        dst = pl.ds(g * PAGE, PAGE)
        copies.append(pltpu.make_async_copy(ckv_hbm.at[layer, page], ckv_buf.at[slot, dst, :],
                                            sems.at[0, slot]))
        copies.append(pltpu.make_async_copy(kpe_hbm.at[layer, page], kpe_buf.at[slot, :, dst],
                                            sems.at[1, slot]))
    return copies


def _start_pages(copies):
    for i, cp in enumerate(copies):
        cp.start(priority=(i // 2) % 2)


def _decode_kernel(pt_ref, q_ref, knew_ref, ckv_hbm, kpe_hbm, o_ref, ckv_buf, kpe_buf, sems,
                   *, layer, n_chunks):
    b = pl.program_id(0)
    n_seq = pl.num_programs(0)
    copies = functools.partial(_page_copies, pt_ref, ckv_hbm, kpe_hbm, ckv_buf, kpe_buf, sems, layer)

    @pl.when(b == 0)
    def _():
        _start_pages(copies(b, 0, 0))

    q = q_ref[0]
    nt = (((1,), (1,)), ((), ()))
    q_lat = q[:, :MLA_L].astype(BF16)
    q_pe = q[:, MLA_L:MLA_L + MLA_ROPE].astype(BF16)
    sub = DEC_G * PAGE // DEC_NS
    blocks = [pl.ds(i * sub, sub) for i in range(DEC_NS)]
    m = jnp.full((MLA_H, 1), NEG_INF, F32)
    l = jnp.zeros((MLA_H, 1), F32)
    acc = jnp.zeros((MLA_H, MLA_L), F32)
    for c in range(n_chunks):
        slot = c % 2
        for cp in copies(b, c, slot):
            cp.wait()
        if c + 1 < n_chunks:
            _start_pages(copies(b, c + 1, 1 - slot))
        else:
            _start_pages(copies(lax.rem(b + 1, n_seq), 0, 1 - slot))
        ck = [ckv_buf[slot, blk, :].astype(BF16) for blk in blocks]
        s_lat = [lax.dot_general(q_lat, k, nt, preferred_element_type=F32) for k in ck]
        s_pe = [jnp.dot(q_pe, kpe_buf[slot, :, blk].astype(BF16), preferred_element_type=F32)
                for blk in blocks]
        s = [x + y for x, y in zip(s_lat, s_pe)]
        m_new = m
        for x in s:
            m_new = jnp.maximum(m_new, jnp.max(x, axis=-1, keepdims=True))
        alpha = jnp.exp2(m - m_new)
        p = [jnp.exp2(x - m_new) for x in s]
        pv = [jnp.dot(x.astype(BF16), k, preferred_element_type=F32) for x, k in zip(p, ck)]
        l = alpha * l + sum(jnp.sum(x, axis=-1, keepdims=True) for x in p)
        acc = alpha * acc + sum(pv)
        m = m_new

    kn = knew_ref[0]
    s_new = jnp.sum(q * kn, axis=-1, keepdims=True)
    m_fin = jnp.maximum(m, s_new)
    a_fin = jnp.exp2(m - m_fin)
    p_new = jnp.exp2(s_new - m_fin)
    o_ref[0] = (a_fin * acc + p_new * kn[:, :MLA_L]) / (a_fin * l + p_new)

    @pl.when(b == n_seq - 1)
    def _():
        for cp in copies(0, 0, n_chunks % 2):
            cp.wait()


def _attn_decode(page_table, q_dec, knew, cache_ckv, cache_kpe, *, layer):
    n_seq, n_pages = page_table.shape
    n_chunks = n_pages // DEC_G
    assert n_chunks % 2 == 0, "buffer roles must repeat from one sequence to the next"
    keys = DEC_G * PAGE
    kern = functools.partial(_decode_kernel, layer=layer, n_chunks=n_chunks)
    grid_spec = pltpu.PrefetchScalarGridSpec(
        num_scalar_prefetch=1,
        grid=(n_seq,),
        in_specs=[pl.BlockSpec((1, MLA_H, KC_W), lambda b, pt: (b, 0, 0)),
                  pl.BlockSpec((1, 1, KC_W), lambda b, pt: (b, 0, 0)),
                  pl.BlockSpec(memory_space=pl.ANY), pl.BlockSpec(memory_space=pl.ANY)],
        out_specs=pl.BlockSpec((1, MLA_H, MLA_L), lambda b, pt: (b, 0, 0)),
        scratch_shapes=[pltpu.VMEM((2, keys, MLA_L), F32), pltpu.VMEM((2, MLA_ROPE, keys), F32),
                        pltpu.SemaphoreType.DMA((2, 2))])
    return pl.pallas_call(
        kern,
        grid_spec=grid_spec,
        out_shape=jax.ShapeDtypeStruct((n_seq, MLA_H, MLA_L), F32),
        compiler_params=_cparams(("arbitrary",)),
        name="mla_decode",
    )(page_table, q_dec, knew, cache_ckv, cache_kpe)


GLA_TG = 512


def _head_norm_gate(o, gr, g_ref, seg_ref):
    ms = jnp.dot(o * o, seg_ref[...], preferred_element_type=F32, precision=lax.Precision.HIGHEST)
    return o * lax.rsqrt(ms + RMS_EPS) * g_ref[...] * (gr * jax.nn.sigmoid(gr))


def _gla_gates(glow, wgate_ref, bgate_ref):
    z = jnp.dot(glow.astype(BF16), wgate_ref[...], preferred_element_type=F32) + bgate_ref[...]
    return jax.nn.log_sigmoid(z) / GLA_NORM


def _gla_kernel(pg_ref, wgate_ref, bgate_ref, g_ref, seg_ref, tri_ref, o_ref, st_ref, st_sc, o_sc):
    @pl.when(pl.program_id(1) == 0)
    def _():
        st_sc[...] = jnp.zeros_like(st_sc)

    kw = GLA_H * GLA_DK
    c = GLA_CHUNK
    pg = pg_ref[...]
    logg = _gla_gates(pg[:, 2 * kw + GLA_W:2 * kw + GLA_W + LANE], wgate_ref, bgate_ref)
    hm_q = (lax.broadcasted_iota(jnp.int32, (GLA_H, c, kw), 2) // GLA_DK
            == lax.broadcasted_iota(jnp.int32, (GLA_H, c, kw), 0))
    hm_o = (lax.broadcasted_iota(jnp.int32, (GLA_H, c, GLA_W), 2) // GLA_DV
            == lax.broadcasted_iota(jnp.int32, (GLA_H, c, GLA_W), 0))
    hm_s = (lax.broadcasted_iota(jnp.int32, (GLA_W, kw), 0) // GLA_DV
            == lax.broadcasted_iota(jnp.int32, (GLA_W, kw), 1) // GLA_DK)
    causal = (lax.broadcasted_iota(jnp.int32, (c, c), 1) <= lax.broadcasted_iota(jnp.int32, (c, c), 0))
    nt = (((1,), (1,)), ((), ()))
    chunks = [slice(ci * c, (ci + 1) * c) for ci in range(GLA_TG // c)]
    bcum = [jnp.dot(tri_ref[...], logg[r], preferred_element_type=F32, precision=lax.Precision.HIGHEST)
            for r in chunks]
    v16 = [pg[r, 2 * kw:2 * kw + GLA_W].astype(BF16) for r in chunks]
    qe, att, ut, decay = [], [], [], []
    for r, b in zip(chunks, bcum):
        b_last = b[c - 1:c, :]
        k = pg[r, kw:2 * kw]
        qe_c = pg[r, :kw] * (GLA_DK ** -0.5) * jnp.exp(b)
        ke = (k * jnp.exp(-b)).astype(BF16)
        kd = (k * jnp.exp(b_last - b)).astype(BF16)
        qe.append(qe_c.astype(BF16))
        decay.append(jnp.exp(b_last))
        qs = jnp.where(hm_q, qe_c[None], 0.0).reshape(GLA_H * c, kw).astype(BF16)
        att.append(lax.dot_general(qs, ke, nt, preferred_element_type=F32))
        ut.append((pg[r, 2 * kw:2 * kw + GLA_W].T.astype(BF16), kd))
    ut = [jnp.dot(vt, kd, preferred_element_type=F32) for vt, kd in ut]
    oi = [jnp.dot(jnp.where(causal[None], a.reshape(GLA_H, c, c), 0.0).reshape(GLA_H * c, c).astype(BF16),
                  v, preferred_element_type=F32) for a, v in zip(att, v16)]
    st = st_sc[...]
    for r, q, o, u, d in zip(chunks, qe, oi, ut, decay):
        o_intra = jnp.sum(jnp.where(hm_o, o.reshape(GLA_H, c, GLA_W), 0.0), axis=0)
        o_inter = lax.dot_general(q, st.astype(BF16), nt, preferred_element_type=F32)
        o_sc[r, :] = o_intra + o_inter
        st = st * d + jnp.where(hm_s, u, 0.0)
    st_sc[...] = st
    st_ref[0] = st
    gr = pg[:, 2 * kw + GLA_W + LANE:]
    o_ref[...] = _head_norm_gate(o_sc[...], gr, g_ref, seg_ref)


def _gla_prompt(pg, prm, layer, *, n_seq, seq_len):
    nj = seq_len // GLA_TG
    kw = GLA_H * GLA_DK
    row = lambda b, j: (b * nj + j, 0)
    return pl.pallas_call(
        _gla_kernel,
        grid=(n_seq, nj),
        in_specs=[pl.BlockSpec((GLA_TG, PG_W), row),
                  _layer_spec(layer, (LANE, kw)), _layer_spec(layer, (1, kw)),
                  _layer_spec(layer, (1, GLA_W)),
                  _const_spec((GLA_W, GLA_W)), _const_spec((GLA_CHUNK, GLA_CHUNK))],
        out_specs=[pl.BlockSpec((GLA_TG, GLA_W), row),
                   pl.BlockSpec((1, GLA_W, kw), lambda b, j: (b, 0, 0))],
        out_shape=[jax.ShapeDtypeStruct((n_seq * seq_len, GLA_W), F32),
                   jax.ShapeDtypeStruct((n_seq, GLA_W, kw), F32)],
        scratch_shapes=[pltpu.VMEM((GLA_W, kw), F32), pltpu.VMEM((GLA_TG, GLA_W), F32)],
        compiler_params=_cparams(("parallel", "arbitrary")),
        name="gla",
    )(pg, prm["gla_wgate"], prm["gla_bgate"], prm["gla_g"], prm["gla_seg"], prm["gla_tri"])


def _gla_step_kernel(pg_ref, s_ref, wgate_ref, bgate_ref, g_ref, seg_ref, o_ref, snew_ref):
    kw = GLA_H * GLA_DK
    pg = pg_ref[...]
    n = pg.shape[0]
    q = pg[:, :kw] * (GLA_DK ** -0.5)
    k = pg[:, kw:2 * kw]
    v = pg[:, 2 * kw:2 * kw + GLA_W]
    decay = jnp.exp(_gla_gates(pg[:, 2 * kw + GLA_W:2 * kw + GLA_W + LANE], wgate_ref, bgate_ref))
    q_t, k_t, d_t = q.T, k.T, decay.T
    o_rows = []
    for b in range(n):
        vexp = jnp.concatenate(
            [jnp.broadcast_to(v[b:b + 1, h * GLA_DV:(h + 1) * GLA_DV], (GLA_DK, GLA_DV))
             for h in range(GLA_H)], axis=0)
        s_new = d_t[:, b:b + 1] * s_ref[b] + k_t[:, b:b + 1] * vexp
        snew_ref[b] = s_new
        w = q_t[:, b:b + 1] * s_new
        o_rows.append(jnp.concatenate(
            [jnp.sum(w[h * GLA_DK:(h + 1) * GLA_DK], axis=0, keepdims=True) for h in range(GLA_H)],
            axis=1))
    o = jnp.concatenate(o_rows, axis=0)
    o_ref[...] = _head_norm_gate(o, pg[:, 2 * kw + GLA_W + LANE:], g_ref, seg_ref)


def _gla_step(pg, state, prm, layer):
    n = pg.shape[0]
    kw = GLA_H * GLA_DK
    return pl.pallas_call(
        _gla_step_kernel,
        grid=(1,),
        in_specs=[_const_spec((n, PG_W)), _layer_spec(layer, (n, kw, GLA_DV)),
                  _layer_spec(layer, (LANE, kw)), _layer_spec(layer, (1, kw)),
                  _layer_spec(layer, (1, GLA_W)), _const_spec((GLA_W, GLA_W))],
        out_specs=[_const_spec((n, GLA_W)), _const_spec((n, kw, GLA_DV))],
        out_shape=[jax.ShapeDtypeStruct((n, GLA_W), F32),
                   jax.ShapeDtypeStruct((n, kw, GLA_DV), F32)],
        compiler_params=_cparams(("arbitrary",)),
        name="gla_step",
    )(pg, state, prm["gla_wgate"], prm["gla_bgate"], prm["gla_g"], prm["gla_seg"])


FFN_TM = 512


def _mix_residual(x_ref, os5_ref, omla_ref, ogla_ref, wout_ref):
    return (x_ref[...]
            + jnp.dot(os5_ref[...].astype(BF16), wout_ref[:S5_W], preferred_element_type=F32)
            + jnp.dot(omla_ref[...].astype(BF16), wout_ref[S5_W:S5_W + MLA_W],
                      preferred_element_type=F32)
            + jnp.dot(ogla_ref[...].astype(BF16), wout_ref[S5_W + MLA_W:],
                      preferred_element_type=F32))


def _ffn_kernel(x_ref, os5_ref, omla_ref, ogla_ref, wout_ref, gf_ref, win_ref, cw_ref, cb_ref, wo_ref,
                *rest, final_norm):
    if final_norm:
        gfin_ref, y_ref, buf_ref, hn_sc, acc_sc, carry_sc = rest
    else:
        y_ref, buf_ref, hn_sc, acc_sc, carry_sc = rest
    tm = x_ref.shape[0]

    @pl.when(pl.program_id(1) == 0)
    def _():
        carry_sc[...] = jnp.zeros_like(carry_sc)

    x1 = _mix_residual(x_ref, os5_ref, omla_ref, ogla_ref, wout_ref)
    y_ref[...] = x1
    hn_sc[...] = _rms(x1, gf_ref[...]).astype(BF16)
    sub = 8
    row = lax.broadcasted_iota(jnp.int32, (sub, FF_CHUNK), 0)

    def up_proj(c):
        hn = hn_sc[...]
        return (jnp.dot(hn, win_ref[:, c * FF_CHUNK:(c + 1) * FF_CHUNK], preferred_element_type=F32),
                jnp.dot(hn, win_ref[:, D_FF + c * FF_CHUNK:D_FF + (c + 1) * FF_CHUNK],
                        preferred_element_type=F32))

    ahead = up_proj(0)
    for c in range(FF_NC):
        cols = slice(c * FF_CHUNK, (c + 1) * FF_CHUNK)
        val, gate = ahead
        if c + 1 < FF_NC:
            ahead = up_proj(c + 1)
        prev = carry_sc[:, cols]
        r1 = pltpu.roll(gate, 1, 0)
        r2 = pltpu.roll(gate, 2, 0)
        head1 = jnp.where(row == 0, prev[1:2], r1[:sub])
        head2 = jnp.where(row == 0, prev[0:1], jnp.where(row == 1, prev[1:2], r2[:sub]))
        g1 = jnp.concatenate([head1, r1[sub:]], axis=0)
        g2 = jnp.concatenate([head2, r2[sub:]], axis=0)
        conv = cb_ref[:, cols] + cw_ref[0:1, cols] * g2 + cw_ref[1:2, cols] * g1 + cw_ref[2:3, cols] * gate
        a = jax.nn.gelu(conv) * val
        part = jnp.dot(a.astype(BF16), wo_ref[cols, :], preferred_element_type=F32)
        if c == 0:
            acc_sc[...] = part
        else:
            acc_sc[...] += part
        carry_sc[0:2, cols] = gate[tm - 2:tm]

    y = y_ref[...] + acc_sc[...]
    if final_norm:
        y = _rms(y, gfin_ref[...])
    y_ref[...] = y
    buf_ref[0] = carry_sc[0:2, :]


def _ffn_prompt(x2d, os5, omla, ogla, prm, layer, *, n_seq, seq_len, final_g=None):
    tm = FFN_TM
    nj = seq_len // tm
    row = lambda b, j: (b * nj + j, 0)
    final_norm = final_g is not None
    in_specs = [pl.BlockSpec((tm, D_MODEL), row), pl.BlockSpec((tm, S5_W), row),
                pl.BlockSpec((tm, MLA_W), row), pl.BlockSpec((tm, GLA_W), row),
                _layer_spec(layer, (D_MODEL, D_MODEL), resident=True), _layer_spec(layer, (1, D_MODEL)),
                _layer_spec(layer, (D_MODEL, 2 * D_FF), resident=True), _layer_spec(layer, (3, D_FF)),
                _layer_spec(layer, (1, D_FF)), _layer_spec(layer, (D_FF, D_MODEL), resident=True)]
    args = [x2d, os5, omla, ogla, prm["wout"], prm["ffn_g"], prm["ffn_win"], prm["ffn_cw"],
            prm["ffn_cb"], prm["ffn_wo"]]
    if final_norm:
        in_specs.append(_const_spec((1, D_MODEL)))
        args.append(final_g)
    return pl.pallas_call(
        functools.partial(_ffn_kernel, final_norm=final_norm),
        grid=(n_seq, nj),
        in_specs=in_specs,
        out_specs=[pl.BlockSpec((tm, D_MODEL), row),
                   pl.BlockSpec((1, 2, D_FF), lambda b, j: (b, 0, 0))],
        out_shape=[jax.ShapeDtypeStruct((n_seq * seq_len, D_MODEL), F32),
                   jax.ShapeDtypeStruct((n_seq, 2, D_FF), F32)],
        scratch_shapes=[pltpu.VMEM((tm, D_MODEL), BF16), pltpu.VMEM((tm, D_MODEL), F32),
                        pltpu.VMEM((8, D_FF), F32)],
        compiler_params=_cparams(("parallel", "arbitrary")),
        name="ffn",
    )(*args)


def _ffn_step_kernel(x_ref, os5_ref, olat_ref, wuv_ref, ogla_ref, wout_ref, gf_ref, b0_ref, b1_ref,
                     wv_ref, wg_ref, cw_ref, cb_ref, wo_ref, *rest, final_norm):
    if final_norm:
        gfin_ref, y_ref, gate_ref, x1_sc, hn_sc, acc_sc = rest
    else:
        y_ref, gate_ref, x1_sc, hn_sc, acc_sc = rest
    c = pl.program_id(0)

    @pl.when(c == 0)
    def _():
        omla = jnp.dot(olat_ref[...].astype(BF16), wuv_ref[...], preferred_element_type=F32)
        x1 = (x_ref[...]
              + jnp.dot(os5_ref[...].astype(BF16), wout_ref[:S5_W], preferred_element_type=F32)
              + jnp.dot(omla.astype(BF16), wout_ref[S5_W:S5_W + MLA_W], preferred_element_type=F32)
              + jnp.dot(ogla_ref[...].astype(BF16), wout_ref[S5_W + MLA_W:],
                        preferred_element_type=F32))
        x1_sc[...] = x1
        hn_sc[...] = _rms(x1, gf_ref[...]).astype(BF16)
        acc_sc[...] = jnp.zeros_like(acc_sc)

    hn = hn_sc[...]
    val = jnp.dot(hn, wv_ref[...], preferred_element_type=F32)
    gate = jnp.dot(hn, wg_ref[...], preferred_element_type=F32)
    cw = cw_ref[...]
    conv = cb_ref[...] + cw[0:1] * b0_ref[...] + cw[1:2] * b1_ref[...] + cw[2:3] * gate
    a = jax.nn.gelu(conv) * val
    acc_sc[...] += jnp.dot(a.astype(BF16), wo_ref[...], preferred_element_type=F32)
    gate_ref[...] = gate

    @pl.when(c == FF_NC - 1)
    def _():
        y = x1_sc[...] + acc_sc[...]
        if final_norm:
            y = _rms(y, gfin_ref[...])
        y_ref[...] = y


def _ffn_step(x2d, os5, olat, ogla, conv_state, prm, layer, *, final_g=None):
    n = x2d.shape[0]
    final_norm = final_g is not None
    lcol = lambda c: (layer, 0, c)
    in_specs = [_const_spec((n, D_MODEL)), _const_spec((n, S5_W)),
                _const_spec((n, MLA_H * MLA_L)), _layer_spec(layer, (MLA_H * MLA_L, MLA_W)),
                _const_spec((n, GLA_W)), _layer_spec(layer, (D_MODEL, D_MODEL)),
                _layer_spec(layer, (1, D_MODEL)),
                pl.BlockSpec((None, n, FF_CHUNK), lcol),
                pl.BlockSpec((None, n, FF_CHUNK), lambda c: (layer, 0, FF_NC + c)),
                pl.BlockSpec((None, D_MODEL, FF_CHUNK), lcol),
                pl.BlockSpec((None, D_MODEL, FF_CHUNK), lambda c: (layer, 0, FF_NC + c)),
                pl.BlockSpec((None, 3, FF_CHUNK), lcol), pl.BlockSpec((None, 1, FF_CHUNK), lcol),
                pl.BlockSpec((None, FF_CHUNK, D_MODEL), lambda c: (layer, c, 0))]
    args = [x2d, os5, olat, prm["mla_wuv_bd"], ogla, prm["wout"], prm["ffn_g"], conv_state, conv_state,
            prm["ffn_win"], prm["ffn_win"], prm["ffn_cw"], prm["ffn_cb"], prm["ffn_wo"]]
    if final_norm:
        in_specs.append(_const_spec((1, D_MODEL)))
        args.append(final_g)
    return pl.pallas_call(
        functools.partial(_ffn_step_kernel, final_norm=final_norm),
        grid=(FF_NC,),
        in_specs=in_specs,
        out_specs=[_const_spec((n, D_MODEL)), pl.BlockSpec((n, FF_CHUNK), lambda c: (0, c))],
        out_shape=[jax.ShapeDtypeStruct((n, D_MODEL), F32), jax.ShapeDtypeStruct((n, D_FF), F32)],
        scratch_shapes=[pltpu.VMEM((n, D_MODEL), F32), pltpu.VMEM((n, D_MODEL), BF16),
                        pltpu.VMEM((n, D_MODEL), F32)],
        compiler_params=_cparams(("arbitrary",)),
        name="ffn_step",
    )(*args)


def _rot_cols(w):
    half = MLA_ROPE // 2
    return jnp.concatenate([-w[..., half:], w[..., :half]], axis=-1)


def _pad_cols(w, width):
    return jnp.pad(w, [(0, 0)] * (w.ndim - 1) + [(0, width - w.shape[-1])])


def _prepare_params(norm_mix_g, w_in, s5_a_re, s5_a_im, s5_log_dt, s5_b_re, s5_b_im, s5_c_re, s5_c_im,
                    s5_d, s5_w_glu, s5_b_glu, mla_q_norm_g, mla_w_qb, mla_kv_norm_g, mla_w_uk, mla_w_uv,
                    gla_w_gate, gla_b_gate, gla_norm_g, w_out, norm_ffn_g, w_ffn_in, ffn_conv_w, ffn_conv_b,
                    w_ffn_out):
    row = lambda v: v[:, None, :]
    o = 0
    cols = {}
    for name, width in (("u", 256), ("cq", 256), ("ckv", 128), ("kpe", 32), ("gq", 128), ("gk", 128),
                        ("gv", 256), ("glow", 16), ("gr", 256)):
        cols[name] = w_in[:, :, o:o + width]
        o += width
    in_w = jnp.concatenate(
        [cols["u"], cols["cq"], cols["ckv"], _pad_cols(cols["kpe"], LANE),
         _pad_cols(_rot_cols(cols["kpe"]), LANE), cols["gq"], cols["gk"], cols["gv"],
         _pad_cols(cols["glow"], LANE), cols["gr"]], axis=2).astype(BF16)

    dt = jnp.exp(s5_log_dt)[:, :, None]
    mag = jnp.exp(s5_a_re * dt)
    ab_re, ab_im = mag * jnp.cos(s5_a_im * dt), mag * jnp.sin(s5_a_im * dt)
    den = s5_a_re * s5_a_re + s5_a_im * s5_a_im
    k_re = ((ab_re - 1.0) * s5_a_re + ab_im * s5_a_im) / den
    k_im = (ab_im * s5_a_re - (ab_re - 1.0) * s5_a_im) / den
    bb_re = k_re[..., None] * s5_b_re - k_im[..., None] * s5_b_im
    bb_im = k_re[..., None] * s5_b_im + k_im[..., None] * s5_b_re
    eye = jnp.eye(S5_G, dtype=F32)
    bd_in = lambda m: jnp.einsum("lgpc,gh->lgchp", m, eye).reshape(DEPTH, S5_W, S5_N)
    bd_out = lambda m: jnp.einsum("lgcp,gh->lgphc", m, eye).reshape(DEPTH, S5_N, S5_W)

    wqb = mla_w_qb.reshape(DEPTH, MLA_QL, MLA_H, MLA_NOPE + MLA_ROPE)
    w_nope = wqb[..., :MLA_NOPE].reshape(DEPTH, MLA_QL, MLA_H * MLA_NOPE)
    w_rope = wqb[..., MLA_NOPE:]
    heads = lambda m: _pad_cols(m, LANE).reshape(DEPTH, MLA_QL, MLA_H * LANE)
    eye_h = jnp.eye(MLA_H, dtype=F32)
    wuk = jnp.einsum("dlhn,hk->dhnkl", mla_w_uk, eye_h).reshape(DEPTH, MLA_H * MLA_NOPE, MLA_H * MLA_L)
    wuv = jnp.einsum("dlhv,hk->dhlkv", mla_w_uv, eye_h).astype(BF16)

    seg = (jnp.arange(GLA_W)[:, None] // GLA_DV == jnp.arange(GLA_W)[None, :] // GLA_DV)
    tri = jnp.arange(GLA_CHUNK)[None, :] <= jnp.arange(GLA_CHUNK)[:, None]
    return dict(
        in_g=row(norm_mix_g), in_w=in_w,
        s5_a=jnp.stack([ab_re.reshape(DEPTH, S5_N), ab_im.reshape(DEPTH, S5_N)], axis=1),
        s5_bm=jnp.concatenate([bd_in(bb_re), bd_in(bb_im)], axis=2).astype(BF16),
        s5_cm=jnp.concatenate([bd_out(s5_c_re), -bd_out(s5_c_im)], axis=1).astype(BF16),
        s5_d=row(s5_d), s5_wglu=s5_w_glu.astype(BF16), s5_bglu=row(s5_b_glu),
        mla_gq=row(mla_q_norm_g), mla_gkv=row(mla_kv_norm_g),
        mla_wq=jnp.concatenate([w_nope, heads(w_rope), heads(_rot_cols(w_rope))], axis=2).astype(BF16),
        mla_wuk=wuk.astype(BF16),
        mla_wuv=wuv.reshape(DEPTH, MLA_H, MLA_L, MLA_W),
        mla_wuv_bd=wuv.reshape(DEPTH, MLA_H * MLA_L, MLA_W),
        gla_wgate=jnp.pad(gla_w_gate, ((0, 0), (0, LANE - GLA_R), (0, 0))).astype(BF16),
        gla_bgate=row(gla_b_gate), gla_g=row(gla_norm_g),
        gla_seg=seg.astype(F32) / GLA_DV, gla_tri=tri.astype(F32),
        wout=w_out.astype(BF16), ffn_g=row(norm_ffn_g), ffn_win=w_ffn_in.astype(BF16),
        ffn_cw=ffn_conv_w, ffn_cb=row(ffn_conv_b), ffn_wo=w_ffn_out.astype(BF16))


def _rope_tables(pos):
    half = MLA_ROPE // 2
    inv = ROPE_THETA ** (-jnp.arange(half, dtype=F32) / half)
    ang = pos.astype(F32)[:, None] * inv[None, :]
    cos = jnp.cos(ang)
    sin = jnp.sin(ang)
    return (_pad_cols(jnp.concatenate([cos, cos], axis=1), LANE),
            _pad_cols(jnp.concatenate([sin, sin], axis=1), LANE))


def _s5_state_out(hfin):
    shape = hfin.shape[:2] + (S5_G, S5_P)
    return jnp.stack([hfin[..., :S5_N].reshape(shape), hfin[..., S5_N:].reshape(shape)], axis=-1)


def kernel(x_prompt, x_sample, cache_mla_ckv, cache_mla_krope, page_table, state_s5, state_gla, state_ffn_conv, norm_mix_g, w_in, s5_a_re, s5_a_im, s5_log_dt, s5_b_re, s5_b_im, s5_c_re, s5_c_im, s5_d, s5_w_glu, s5_b_glu, mla_q_norm_g, mla_w_qb, mla_kv_norm_g, mla_w_uk, mla_w_uv, gla_w_gate, gla_b_gate, gla_norm_g, w_out, norm_ffn_g, w_ffn_in, ffn_conv_w, ffn_conv_b, w_ffn_out, norm_final_g):
    bp, sp = x_prompt.shape[:2]
    bs = x_sample.shape[0]
    n_pages = page_table.shape[1]
    past_len = n_pages * cache_mla_ckv.shape[2]
    kw = GLA_H * GLA_DK

    prm = _prepare_params(norm_mix_g, w_in, s5_a_re, s5_a_im, s5_log_dt, s5_b_re, s5_b_im, s5_c_re, s5_c_im,
                          s5_d, s5_w_glu, s5_b_glu, mla_q_norm_g, mla_w_qb, mla_kv_norm_g, mla_w_uk,
                          mla_w_uv, gla_w_gate, gla_b_gate, gla_norm_g, w_out, norm_ffn_g, w_ffn_in,
                          ffn_conv_w, ffn_conv_b, w_ffn_out)
    cos_p, sin_p = _rope_tables(jnp.arange(sp))
    cos_s, sin_s = _rope_tables(jnp.full((bs,), past_len))
    gfin = norm_final_g[None]
    krope_t = jnp.swapaxes(cache_mla_krope, 2, 3)
    h0_all = jnp.concatenate([state_s5[..., 0].reshape(DEPTH, bs, S5_N),
                              state_s5[..., 1].reshape(DEPTH, bs, S5_N)], axis=2)
    gla_all = state_gla.reshape(DEPTH, bs, kw, GLA_DV)
    conv_all = state_ffn_conv.reshape(DEPTH, bs, 2 * D_FF)
    h0_zero = jnp.zeros((bp, 2 * S5_N), F32)
    xp = x_prompt.reshape(bp * sp, D_MODEL)
    xd = x_sample.reshape(bs, D_MODEL)
    outs = [[] for _ in range(10)]

    for l in range(DEPTH):
        final_g = gfin if l == DEPTH - 1 else None

        u, pm, pg = _in_proj(xp, prm, l, tm=512)
        o_s5, hfin = _s5(u, h0_zero, prm, l, r_rows=bp, t_total=sp, t_steps=128)
        qcat, kcat, ckv, kpe = _mla_prep(pm, cos_p, sin_p, prm, l, n_seq=bp, seq_len=sp, tm=512,
                                         q_dtype=BF16)
        o_mla = _attn_prompt(qcat, kcat, prm, l, n_seq=bp, seq_len=sp)
        o_gla, st = _gla_prompt(pg, prm, l, n_seq=bp, seq_len=sp)
        xp, cbuf = _ffn_prompt(xp, o_s5, o_mla, o_gla, prm, l, n_seq=bp, seq_len=sp, final_g=final_g)
        for k, v in zip(range(5), (ckv, kpe, hfin, st, cbuf)):
            outs[k].append(v)

        u, pm, pg = _in_proj(xd, prm, l, tm=bs)
        o_s5, hfin = _s5(u, h0_all[l], prm, l, r_rows=bs, t_total=1, t_steps=1)
        qcat, kcat, ckv, kpe = _mla_prep(pm, cos_s, sin_s, prm, l, n_seq=1, seq_len=bs, tm=bs, q_dtype=F32)
        o_lat = _attn_decode(page_table, qcat.transpose(1, 0, 2), kcat[:, None, :], cache_mla_ckv,
                             krope_t, layer=l)
        o_gla, st = _gla_step(pg, gla_all, prm, l)
        xd, gate = _ffn_step(xd, o_s5, o_lat.reshape(bs, MLA_H * MLA_L), o_gla, conv_all, prm, l,
                             final_g=final_g)
        for k, v in zip(range(5, 10), (ckv, kpe, hfin, st, gate)):
            outs[k].append(v)

    outs = [jnp.stack(o) for o in outs]
    st5 = outs[3].reshape(DEPTH, bp, GLA_H, GLA_DV, GLA_H, GLA_DK)
    gla_p = jnp.stack([st5[:, :, h, :, h, :] for h in range(GLA_H)], axis=2).swapaxes(3, 4)
    conv_s = jnp.stack([state_ffn_conv[:, :, 1], outs[9]], axis=2)
    return (xp.reshape(bp, sp, D_MODEL), xd.reshape(bs, 1, D_MODEL),
            outs[0].reshape(DEPTH, bp, sp // PAGE, PAGE, MLA_L),
            outs[1].reshape(DEPTH, bp, sp // PAGE, PAGE, MLA_ROPE),
            _s5_state_out(outs[2]), gla_p, outs[4],
            outs[5][:, :, None, :], outs[6][:, :, None, :], _s5_state_out(outs[7]),
            outs[8].reshape(DEPTH, bs, GLA_H, GLA_DK, GLA_DV), conv_s)
```

```python
import functools
import math

import jax
import jax.numpy as jnp
from jax import lax
from jax.experimental import pallas as pl
from jax.experimental.pallas import tpu as pltpu

F32 = jnp.float32
BF16 = jnp.bfloat16

D_MODEL = 1024
DEPTH = 4
PAGE = 128
S5_W = 256
S5_G = 16
S5_C = 16
S5_P = 64
S5_N = S5_G * S5_P
MLA_H = 8
MLA_NOPE = 64
MLA_ROPE = 32
MLA_V = 64
MLA_W = MLA_H * MLA_V
MLA_QL = 256
MLA_L = 128
MLA_SCALE = (MLA_NOPE + MLA_ROPE) ** -0.5
Q_SCALE = MLA_SCALE * math.log2(math.e)
ROPE_THETA = 10000.0
GLA_H = 4
GLA_W = 256
GLA_DV = 64
GLA_DK = 32
GLA_R = 16
GLA_NORM = 16.0
GLA_CHUNK = 64
D_FF = 2816
RMS_EPS = 1e-6
NEG_INF = -1e30

LANE = 128
PM_W = 640
PG_W = 896
W_EXT = S5_W + PM_W + PG_W
QA_W = MLA_H * MLA_NOPE + 2 * MLA_H * LANE
KC_W = 2 * LANE

FF_CHUNK = 256
FF_NC = D_FF // FF_CHUNK
VMEM_LIMIT = 56 * 1024 * 1024


def _cparams(sem, vmem=VMEM_LIMIT):
    return pltpu.CompilerParams(dimension_semantics=sem, vmem_limit_bytes=vmem)


def _rms(x, g):
    ms = jnp.mean(x * x, axis=-1, keepdims=True)
    return x * lax.rsqrt(ms + RMS_EPS) * g


def _const_spec(shape):
    nd = len(shape)
    return pl.BlockSpec(shape, lambda *_: (0,) * nd)


def _layer_spec(layer, shape, resident=False):
    nd = len(shape)
    return pl.BlockSpec((None,) + tuple(shape), lambda *_: (layer,) + (0,) * nd,
                        pipeline_mode=pl.Buffered(1) if resident else None)


def _in_proj_kernel(x_ref, g_ref, w_ref, u_ref, pm_ref, pg_ref):
    h = _rms(x_ref[...], g_ref[...]).astype(BF16)
    u_ref[...] = jnp.dot(h, w_ref[:, :S5_W], preferred_element_type=F32)
    pm_ref[...] = jnp.dot(h, w_ref[:, S5_W:S5_W + PM_W], preferred_element_type=F32)
    pg_ref[...] = jnp.dot(h, w_ref[:, S5_W + PM_W:], preferred_element_type=F32)


def _in_proj(x2d, prm, layer, *, tm):
    n = x2d.shape[0]
    row = lambda i: (i, 0)
    return pl.pallas_call(
        _in_proj_kernel,
        grid=(n // tm,),
        in_specs=[pl.BlockSpec((tm, D_MODEL), row), _layer_spec(layer, (1, D_MODEL)),
                  _layer_spec(layer, (D_MODEL, W_EXT))],
        out_specs=[pl.BlockSpec((tm, S5_W), row), pl.BlockSpec((tm, PM_W), row),
                   pl.BlockSpec((tm, PG_W), row)],
        out_shape=[jax.ShapeDtypeStruct((n, S5_W), F32), jax.ShapeDtypeStruct((n, PM_W), F32),
                   jax.ShapeDtypeStruct((n, PG_W), F32)],
        compiler_params=_cparams(("parallel",)),
        name="in_proj",
    )(x2d, prm["in_g"], prm["in_w"])


def _s5_kernel(u_ref, h0_ref, a_ref, bm_ref, cm_ref, d_ref, wg_ref, bg_ref,
               o_ref, hfin_ref, h_sc, bu_sc, *io_sc, r_rows, t_steps):
    @pl.when(pl.program_id(0) == 0)
    def _():
        h_sc[...] = h0_ref[...]

    n_slab = S5_W // LANE
    if t_steps == 1:
        u = u_ref[...]
    else:
        u_sc, y_sc = io_sc
        for r in range(r_rows):
            for k in range(n_slab):
                u_sc[k, pl.ds(r, t_steps, stride=r_rows), :] = u_ref[r, :, k * LANE:(k + 1) * LANE]
        u = jnp.concatenate([u_sc[k] for k in range(n_slab)], axis=1)
    rows = r_rows * t_steps
    u16 = u.astype(BF16)
    ar = jnp.broadcast_to(a_ref[0:1, :], (r_rows, S5_N))
    ai = jnp.broadcast_to(a_ref[1:2, :], (r_rows, S5_N))

    n_half = 2 if t_steps > 1 else 1
    th = t_steps // n_half

    def pieces(h):
        lo, mid, hi = h * th * r_rows, (2 * h + 1) * th * r_rows // 2, (h + 1) * th * r_rows
        return [slice(lo, mid), slice(mid, hi)]

    def project_in(h):
        for p in pieces(h):
            bu_sc[p, :] = jnp.dot(u16[p], bm_ref[...], preferred_element_type=F32)

    def scan(h, hr, hi):
        for t in range(h * th, (h + 1) * th):
            r = slice(t * r_rows, (t + 1) * r_rows)
            hr, hi = (ar * hr - ai * hi + bu_sc[r, :S5_N], ar * hi + ai * hr + bu_sc[r, S5_N:])
            bu_sc[r, :S5_N] = hr
            bu_sc[r, S5_N:] = hi
        return hr, hi

    def project_out(h):
        ps = pieces(h)
        y = [jnp.dot(bu_sc[p, :].astype(BF16), cm_ref[...], preferred_element_type=F32) for p in ps]
        y = [jax.nn.gelu(v + d_ref[...] * u[p]) for v, p in zip(y, ps)]
        z = [jnp.dot(v.astype(BF16), wg_ref[...], preferred_element_type=F32) + bg_ref[...] for v in y]
        return [v * jax.nn.sigmoid(g) for v, g in zip(y, z)]

    for h in range(n_half):
        project_in(h)
    hr, hi = h_sc[:, :S5_N], h_sc[:, S5_N:]
    out = []
    for h in range(n_half):
        hr, hi = scan(h, hr, hi)
        out += project_out(h)
    out = jnp.concatenate(out, axis=0)
    h_sc[:, :S5_N] = hr
    h_sc[:, S5_N:] = hi
    hfin_ref[:, :S5_N] = hr
    hfin_ref[:, S5_N:] = hi
    if t_steps == 1:
        o_ref[...] = out
    else:
        for k in range(n_slab):
            y_sc[k] = out[:, k * LANE:(k + 1) * LANE]
        for r in range(r_rows):
            for k in range(n_slab):
                o_ref[r, :, k * LANE:(k + 1) * LANE] = y_sc[k, pl.ds(r, t_steps, stride=r_rows), :]


def _s5(u, h0, prm, layer, *, r_rows, t_total, t_steps):
    rows = r_rows * t_steps
    kern = functools.partial(_s5_kernel, r_rows=r_rows, t_steps=t_steps)
    if t_steps == 1:
        io_spec = pl.BlockSpec((r_rows, S5_W), lambda i: (0, 0))
        io_shape = (r_rows, S5_W)
        io_scratch = []
    else:
        io_spec = pl.BlockSpec((r_rows, t_steps, S5_W), lambda i: (0, i, 0))
        io_shape = (r_rows, t_total, S5_W)
        io_scratch = [pltpu.VMEM((S5_W // LANE, rows, LANE), F32)] * 2
    o, hfin = pl.pallas_call(
        kern,
        grid=(t_total // t_steps,),
        in_specs=[io_spec, _const_spec((r_rows, 2 * S5_N)), _layer_spec(layer, (2, S5_N)),
                  _layer_spec(layer, (S5_W, 2 * S5_N)), _layer_spec(layer, (2 * S5_N, S5_W)),
                  _layer_spec(layer, (1, S5_W)), _layer_spec(layer, (S5_W, S5_W)),
                  _layer_spec(layer, (1, S5_W))],
        out_specs=[io_spec, _const_spec((r_rows, 2 * S5_N))],
        out_shape=[jax.ShapeDtypeStruct(io_shape, F32),
                   jax.ShapeDtypeStruct((r_rows, 2 * S5_N), F32)],
        scratch_shapes=[pltpu.VMEM((r_rows, 2 * S5_N), F32),
                        pltpu.VMEM((rows, 2 * S5_N), F32)] + io_scratch,
        compiler_params=_cparams(("arbitrary",)),
        name="s5",
    )(u.reshape(io_shape), h0, prm["s5_a"], prm["s5_bm"], prm["s5_cm"], prm["s5_d"], prm["s5_wglu"],
      prm["s5_bglu"])
    return o.reshape(r_rows * t_total, S5_W), hfin


def _mla_prep_kernel(pm_ref, cos_ref, sin_ref, gq_ref, wq_ref, wuk_ref, gkv_ref,
                     qcat_ref, kcat_ref, ckv_ref, kpe_ref):
    pm = pm_ref[...]
    cos = cos_ref[...]
    sin = sin_ref[...]
    qn = _rms(pm[:, :MLA_QL], gq_ref[...]).astype(BF16)
    qa = jnp.dot(qn, wq_ref[...], preferred_element_type=F32)
    n_nope = MLA_H * MLA_NOPE
    qlat = jnp.dot(qa[:, :n_nope].astype(BF16), wuk_ref[...], preferred_element_type=F32)
    for h in range(MLA_H):
        lo = n_nope + h * LANE
        qpe = qa[:, lo:lo + LANE] * cos + qa[:, lo + MLA_H * LANE:lo + (MLA_H + 1) * LANE] * sin
        qcat_ref[h, :, :LANE] = (qlat[:, h * LANE:(h + 1) * LANE] * Q_SCALE).astype(qcat_ref.dtype)
        qcat_ref[h, :, LANE:] = (qpe * Q_SCALE).astype(qcat_ref.dtype)
    ckv = _rms(pm[:, MLA_QL:MLA_QL + MLA_L], gkv_ref[...])
    kpe = pm[:, MLA_QL + MLA_L:MLA_QL + 2 * MLA_L] * cos + pm[:, MLA_QL + 2 * MLA_L:] * sin
    ckv_ref[...] = ckv
    kpe_ref[...] = kpe[:, :MLA_ROPE]
    kcat_ref[:, :LANE] = ckv.astype(kcat_ref.dtype)
    kcat_ref[:, LANE:] = kpe.astype(kcat_ref.dtype)


def _mla_prep(pm, cos_t, sin_t, prm, layer, *, n_seq, seq_len, tm, q_dtype):
    n = n_seq * seq_len
    nj = seq_len // tm
    row = lambda b, j: (b * nj + j, 0)
    pos = lambda b, j: (j, 0)
    return pl.pallas_call(
        _mla_prep_kernel,
        grid=(n_seq, nj),
        in_specs=[pl.BlockSpec((tm, PM_W), row),
                  pl.BlockSpec((tm, LANE), pos), pl.BlockSpec((tm, LANE), pos),
                  _layer_spec(layer, (1, MLA_QL)), _layer_spec(layer, (MLA_QL, QA_W)),
                  _layer_spec(layer, (MLA_H * MLA_NOPE, MLA_H * LANE)), _layer_spec(layer, (1, MLA_L))],
        out_specs=[pl.BlockSpec((MLA_H, tm, KC_W), lambda b, j: (0, b * nj + j, 0)),
                   pl.BlockSpec((tm, KC_W), row),
                   pl.BlockSpec((tm, MLA_L), row), pl.BlockSpec((tm, MLA_ROPE), row)],
        out_shape=[jax.ShapeDtypeStruct((MLA_H, n, KC_W), q_dtype),
                   jax.ShapeDtypeStruct((n, KC_W), q_dtype),
                   jax.ShapeDtypeStruct((n, MLA_L), F32),
                   jax.ShapeDtypeStruct((n, MLA_ROPE), F32)],
        compiler_params=_cparams(("parallel", "parallel")),
        name="mla_prep",
    )(pm, cos_t, sin_t, prm["mla_gq"], prm["mla_wq"], prm["mla_wuk"], prm["mla_gkv"])


ATT_TQ = 512
ATT_TK = 512
ATT_NG = 4


def _attn_kernel(q_ref, k_ref, wuv_ref, o_ref, m_sc, acc_sc):
    i = pl.program_id(1)
    rows = MLA_H * ATT_TQ
    hg = MLA_H // ATT_NG
    grows = hg * ATT_TQ
    m_sc[...] = jnp.full((rows, LANE), NEG_INF, F32)
    acc_sc[...] = jnp.zeros((rows, 2 * MLA_L), F32)

    ones = jnp.ones((ATT_TK, LANE), BF16)

    def tile(j, masked):
        kt = k_ref[pl.ds(pl.multiple_of(j * ATT_TK, ATT_TK), ATT_TK), :]
        v_ext = jnp.concatenate([kt[:, :MLA_L], ones], axis=1)
        if masked:
            qpos = i * ATT_TQ + lax.broadcasted_iota(jnp.int32, (ATT_TQ, ATT_TK), 0)
            kpos = j * ATT_TK + lax.broadcasted_iota(jnp.int32, (ATT_TQ, ATT_TK), 1)
            visible = (kpos <= qpos)[None]
        scores = []
        for g in range(ATT_NG):
            q = q_ref[g * hg:(g + 1) * hg].reshape(grows, KC_W)
            scores.append(lax.dot_general(q, kt, (((1,), (1,)), ((), ())), preferred_element_type=F32))
        for g in range(ATT_NG):
            r = slice(g * grows, (g + 1) * grows)
            s = scores[g]
            if masked:
                s = jnp.where(visible, s.reshape(hg, ATT_TQ, ATT_TK), NEG_INF).reshape(grows, ATT_TK)
            m_old = m_sc[r, :]
            m_new = jnp.maximum(m_old, jnp.max(s, axis=-1, keepdims=True))
            alpha = jnp.exp2(m_old - m_new)
            p = jnp.exp2(s - jnp.tile(m_new, (1, ATT_TK // LANE)))
            pv = jnp.dot(p.astype(kt.dtype), v_ext, preferred_element_type=F32)
            acc_sc[r, :] = jnp.tile(alpha, (1, 2)) * acc_sc[r, :] + pv
            m_sc[r, :] = m_new

    n_full = (i * ATT_TQ) // ATT_TK

    def body(j, c):
        tile(j, False)
        return c

    lax.fori_loop(0, n_full, body, 0)
    tile(n_full, True)
    acc = acc_sc[...]
    o_lat = (acc[:, :MLA_L] / acc[:, MLA_L:]).astype(wuv_ref.dtype)
    out = jnp.dot(o_lat[:ATT_TQ], wuv_ref[0], preferred_element_type=F32)
    for h in range(1, MLA_H):
        out += jnp.dot(o_lat[h * ATT_TQ:(h + 1) * ATT_TQ], wuv_ref[h], preferred_element_type=F32)
    o_ref[...] = out


def _attn_prompt(qcat, kcat, prm, layer, *, n_seq, seq_len):
    nq = seq_len // ATT_TQ
    rows = MLA_H * ATT_TQ
    return pl.pallas_call(
        _attn_kernel,
        grid=(n_seq, nq),
        in_specs=[pl.BlockSpec((MLA_H, ATT_TQ, KC_W), lambda b, i: (0, b * nq + i, 0)),
                  pl.BlockSpec((seq_len, KC_W), lambda b, i: (b, 0)),
                  _layer_spec(layer, (MLA_H, MLA_L, MLA_W))],
        out_specs=pl.BlockSpec((ATT_TQ, MLA_W), lambda b, i: (b * nq + i, 0)),
        out_shape=jax.ShapeDtypeStruct((n_seq * seq_len, MLA_W), F32),
        scratch_shapes=[pltpu.VMEM((rows, LANE), F32), pltpu.VMEM((rows, 2 * MLA_L), F32)],
        compiler_params=_cparams(("parallel", "parallel")),
        name="mla_attn",
    )(qcat, kcat, prm["mla_wuv"])


DEC_G = 64
DEC_NS = 8


def _page_copies(pt_ref, ckv_hbm, kpe_hbm, ckv_buf, kpe_buf, sems, layer, b, c, slot):
    copies = []
    for g in range(DEC_G):
        page = pt_ref[b, c * DEC_G + g]
        dst = pl.ds(g * PAGE, PAGE)
        copies.append(pltpu.make_async_copy(ckv_hbm.at[layer, page], ckv_buf.at[slot, dst, :],
                                            sems.at[0, slot]))
        copies.append(pltpu.make_async_copy(kpe_hbm.at[layer, page], kpe_buf.at[slot, :, dst],
                                            sems.at[1, slot]))
    return copies


def _start_pages(copies):
    for i, cp in enumerate(copies):
        cp.start(priority=(i // 2) % 2)


def _decode_kernel(pt_ref, q_ref, knew_ref, ckv_hbm, kpe_hbm, o_ref, ckv_buf, kpe_buf, sems,
                   *, layer, n_chunks):
    b = pl.program_id(0)
    n_seq = pl.num_programs(0)
    copies = functools.partial(_page_copies, pt_ref, ckv_hbm, kpe_hbm, ckv_buf, kpe_buf, sems, layer)

    @pl.when(b == 0)
    def _():
        _start_pages(copies(b, 0, 0))

    q = q_ref[0]
    nt = (((1,), (1,)), ((), ()))
    q_lat = q[:, :MLA_L].astype(BF16)
    q_pe = q[:, MLA_L:MLA_L + MLA_ROPE].astype(BF16)
    sub = DEC_G * PAGE // DEC_NS
    blocks = [pl.ds(i * sub, sub) for i in range(DEC_NS)]
    m = jnp.full((MLA_H, 1), NEG_INF, F32)
    l = jnp.zeros((MLA_H, 1), F32)
    acc = jnp.zeros((MLA_H, MLA_L), F32)
    for c in range(n_chunks):
        slot = c % 2
        for cp in copies(b, c, slot):
            cp.wait()
        if c + 1 < n_chunks:
            _start_pages(copies(b, c + 1, 1 - slot))
        else:
            _start_pages(copies(lax.rem(b + 1, n_seq), 0, 1 - slot))
        ck = [ckv_buf[slot, blk, :].astype(BF16) for blk in blocks]
        s_lat = [lax.dot_general(q_lat, k, nt, preferred_element_type=F32) for k in ck]
        s_pe = [jnp.dot(q_pe, kpe_buf[slot, :, blk].astype(BF16), preferred_element_type=F32)
                for blk in blocks]
        s = [x + y for x, y in zip(s_lat, s_pe)]
        m_new = m
        for x in s:
            m_new = jnp.maximum(m_new, jnp.max(x, axis=-1, keepdims=True))
        alpha = jnp.exp2(m - m_new)
        p = [jnp.exp2(x - m_new) for x in s]
        pv = [jnp.dot(x.astype(BF16), k, preferred_element_type=F32) for x, k in zip(p, ck)]
        l = alpha * l + sum(jnp.sum(x, axis=-1, keepdims=True) for x in p)
        acc = alpha * acc + sum(pv)
        m = m_new

    kn = knew_ref[0]
    s_new = jnp.sum(q * kn, axis=-1, keepdims=True)
    m_fin = jnp.maximum(m, s_new)
    a_fin = jnp.exp2(m - m_fin)
    p_new = jnp.exp2(s_new - m_fin)
    o_ref[0] = (a_fin * acc + p_new * kn[:, :MLA_L]) / (a_fin * l + p_new)

    @pl.when(b == n_seq - 1)
    def _():
        for cp in copies(0, 0, n_chunks % 2):
            cp.wait()


def _attn_decode(page_table, q_dec, knew, cache_ckv, cache_kpe, *, layer):
    n_seq, n_pages = page_table.shape
    n_chunks = n_pages // DEC_G
    assert n_chunks % 2 == 0, "buffer roles must repeat from one sequence to the next"
    keys = DEC_G * PAGE
    kern = functools.partial(_decode_kernel, layer=layer, n_chunks=n_chunks)
    grid_spec = pltpu.PrefetchScalarGridSpec(
        num_scalar_prefetch=1,
        grid=(n_seq,),
        in_specs=[pl.BlockSpec((1, MLA_H, KC_W), lambda b, pt: (b, 0, 0)),
                  pl.BlockSpec((1, 1, KC_W), lambda b, pt: (b, 0, 0)),
                  pl.BlockSpec(memory_space=pl.ANY), pl.BlockSpec(memory_space=pl.ANY)],
        out_specs=pl.BlockSpec((1, MLA_H, MLA_L), lambda b, pt: (b, 0, 0)),
        scratch_shapes=[pltpu.VMEM((2, keys, MLA_L), F32), pltpu.VMEM((2, MLA_ROPE, keys), F32),
                        pltpu.SemaphoreType.DMA((2, 2))])
    return pl.pallas_call(
        kern,
        grid_spec=grid_spec,
        out_shape=jax.ShapeDtypeStruct((n_seq, MLA_H, MLA_L), F32),
        compiler_params=_cparams(("arbitrary",)),
        name="mla_decode",
    )(page_table, q_dec, knew, cache_ckv, cache_kpe)


GLA_TG = 512


def _head_norm_gate(o, gr, g_ref, seg_ref):
    ms = jnp.dot(o * o, seg_ref[...], preferred_element_type=F32, precision=lax.Precision.HIGHEST)
    return o * lax.rsqrt(ms + RMS_EPS) * g_ref[...] * (gr * jax.nn.sigmoid(gr))


def _gla_gates(glow, wgate_ref, bgate_ref):
    z = jnp.dot(glow.astype(BF16), wgate_ref[...], preferred_element_type=F32) + bgate_ref[...]
    return jax.nn.log_sigmoid(z) / GLA_NORM


def _gla_kernel(pg_ref, wgate_ref, bgate_ref, g_ref, seg_ref, tri_ref, o_ref, st_ref, st_sc, o_sc):
    @pl.when(pl.program_id(1) == 0)
    def _():
        st_sc[...] = jnp.zeros_like(st_sc)

    kw = GLA_H * GLA_DK
    c = GLA_CHUNK
    pg = pg_ref[...]
    logg = _gla_gates(pg[:, 2 * kw + GLA_W:2 * kw + GLA_W + LANE], wgate_ref, bgate_ref)
    hm_q = (lax.broadcasted_iota(jnp.int32, (GLA_H, c, kw), 2) // GLA_DK
            == lax.broadcasted_iota(jnp.int32, (GLA_H, c, kw), 0))
    hm_o = (lax.broadcasted_iota(jnp.int32, (GLA_H, c, GLA_W), 2) // GLA_DV
            == lax.broadcasted_iota(jnp.int32, (GLA_H, c, GLA_W), 0))
    hm_s = (lax.broadcasted_iota(jnp.int32, (GLA_W, kw), 0) // GLA_DV
            == lax.broadcasted_iota(jnp.int32, (GLA_W, kw), 1) // GLA_DK)
    causal = (lax.broadcasted_iota(jnp.int32, (c, c), 1) <= lax.broadcasted_iota(jnp.int32, (c, c), 0))
    nt = (((1,), (1,)), ((), ()))
    chunks = [slice(ci * c, (ci + 1) * c) for ci in range(GLA_TG // c)]
    bcum = [jnp.dot(tri_ref[...], logg[r], preferred_element_type=F32, precision=lax.Precision.HIGHEST)
            for r in chunks]
    v16 = [pg[r, 2 * kw:2 * kw + GLA_W].astype(BF16) for r in chunks]
    qe, att, ut, decay = [], [], [], []
    for r, b in zip(chunks, bcum):
        b_last = b[c - 1:c, :]
        k = pg[r, kw:2 * kw]
        qe_c = pg[r, :kw] * (GLA_DK ** -0.5) * jnp.exp(b)
        ke = (k * jnp.exp(-b)).astype(BF16)
        kd = (k * jnp.exp(b_last - b)).astype(BF16)
        qe.append(qe_c.astype(BF16))
        decay.append(jnp.exp(b_last))
        qs = jnp.where(hm_q, qe_c[None], 0.0).reshape(GLA_H * c, kw).astype(BF16)
        att.append(lax.dot_general(qs, ke, nt, preferred_element_type=F32))
        ut.append((pg[r, 2 * kw:2 * kw + GLA_W].T.astype(BF16), kd))
    ut = [jnp.dot(vt, kd, preferred_element_type=F32) for vt, kd in ut]
    oi = [jnp.dot(jnp.where(causal[None], a.reshape(GLA_H, c, c), 0.0).reshape(GLA_H * c, c).astype(BF16),
                  v, preferred_element_type=F32) for a, v in zip(att, v16)]
    st = st_sc[...]
    for r, q, o, u, d in zip(chunks, qe, oi, ut, decay):
        o_intra = jnp.sum(jnp.where(hm_o, o.reshape(GLA_H, c, GLA_W), 0.0), axis=0)
        o_inter = lax.dot_general(q, st.astype(BF16), nt, preferred_element_type=F32)
        o_sc[r, :] = o_intra + o_inter
        st = st * d + jnp.where(hm_s, u, 0.0)
    st_sc[...] = st
    st_ref[0] = st
    gr = pg[:, 2 * kw + GLA_W + LANE:]
    o_ref[...] = _head_norm_gate(o_sc[...], gr, g_ref, seg_ref)


def _gla_prompt(pg, prm, layer, *, n_seq, seq_len):
    nj = seq_len // GLA_TG
    kw = GLA_H * GLA_DK
    row = lambda b, j: (b * nj + j, 0)
    return pl.pallas_call(
        _gla_kernel,
        grid=(n_seq, nj),
        in_specs=[pl.BlockSpec((GLA_TG, PG_W), row),
                  _layer_spec(layer, (LANE, kw)), _layer_spec(layer, (1, kw)),
                  _layer_spec(layer, (1, GLA_W)),
                  _const_spec((GLA_W, GLA_W)), _const_spec((GLA_CHUNK, GLA_CHUNK))],
        out_specs=[pl.BlockSpec((GLA_TG, GLA_W), row),
                   pl.BlockSpec((1, GLA_W, kw), lambda b, j: (b, 0, 0))],
        out_shape=[jax.ShapeDtypeStruct((n_seq * seq_len, GLA_W), F32),
                   jax.ShapeDtypeStruct((n_seq, GLA_W, kw), F32)],
        scratch_shapes=[pltpu.VMEM((GLA_W, kw), F32), pltpu.VMEM((GLA_TG, GLA_W), F32)],
        compiler_params=_cparams(("parallel", "arbitrary")),
        name="gla",
    )(pg, prm["gla_wgate"], prm["gla_bgate"], prm["gla_g"], prm["gla_seg"], prm["gla_tri"])


def _gla_step_kernel(pg_ref, s_ref, wgate_ref, bgate_ref, g_ref, seg_ref, o_ref, snew_ref):
    kw = GLA_H * GLA_DK
    pg = pg_ref[...]
    n = pg.shape[0]
    q = pg[:, :kw] * (GLA_DK ** -0.5)
    k = pg[:, kw:2 * kw]
    v = pg[:, 2 * kw:2 * kw + GLA_W]
    decay = jnp.exp(_gla_gates(pg[:, 2 * kw + GLA_W:2 * kw + GLA_W + LANE], wgate_ref, bgate_ref))
    q_t, k_t, d_t = q.T, k.T, decay.T
    o_rows = []
    for b in range(n):
        vexp = jnp.concatenate(
            [jnp.broadcast_to(v[b:b + 1, h * GLA_DV:(h + 1) * GLA_DV], (GLA_DK, GLA_DV))
             for h in range(GLA_H)], axis=0)
        s_new = d_t[:, b:b + 1] * s_ref[b] + k_t[:, b:b + 1] * vexp
        snew_ref[b] = s_new
        w = q_t[:, b:b + 1] * s_new
        o_rows.append(jnp.concatenate(
            [jnp.sum(w[h * GLA_DK:(h + 1) * GLA_DK], axis=0, keepdims=True) for h in range(GLA_H)],
            axis=1))
    o = jnp.concatenate(o_rows, axis=0)
    o_ref[...] = _head_norm_gate(o, pg[:, 2 * kw + GLA_W + LANE:], g_ref, seg_ref)


def _gla_step(pg, state, prm, layer):
    n = pg.shape[0]
    kw = GLA_H * GLA_DK
    return pl.pallas_call(
        _gla_step_kernel,
        grid=(1,),
        in_specs=[_const_spec((n, PG_W)), _layer_spec(layer, (n, kw, GLA_DV)),
                  _layer_spec(layer, (LANE, kw)), _layer_spec(layer, (1, kw)),
                  _layer_spec(layer, (1, GLA_W)), _const_spec((GLA_W, GLA_W))],
        out_specs=[_const_spec((n, GLA_W)), _const_spec((n, kw, GLA_DV))],
        out_shape=[jax.ShapeDtypeStruct((n, GLA_W), F32),
                   jax.ShapeDtypeStruct((n, kw, GLA_DV), F32)],
        compiler_params=_cparams(("arbitrary",)),
        name="gla_step",
    )(pg, state, prm["gla_wgate"], prm["gla_bgate"], prm["gla_g"], prm["gla_seg"])


FFN_TM = 512


def _mix_residual(x_ref, os5_ref, omla_ref, ogla_ref, wout_ref):
    return (x_ref[...]
            + jnp.dot(os5_ref[...].astype(BF16), wout_ref[:S5_W], preferred_element_type=F32)
            + jnp.dot(omla_ref[...].astype(BF16), wout_ref[S5_W:S5_W + MLA_W],
                      preferred_element_type=F32)
            + jnp.dot(ogla_ref[...].astype(BF16), wout_ref[S5_W + MLA_W:],
                      preferred_element_type=F32))


def _ffn_kernel(x_ref, os5_ref, omla_ref, ogla_ref, wout_ref, gf_ref, win_ref, cw_ref, cb_ref, wo_ref,
                *rest, final_norm):
    if final_norm:
        gfin_ref, y_ref, buf_ref, hn_sc, acc_sc, carry_sc = rest
    else:
        y_ref, buf_ref, hn_sc, acc_sc, carry_sc = rest
    tm = x_ref.shape[0]

    @pl.when(pl.program_id(1) == 0)
    def _():
        carry_sc[...] = jnp.zeros_like(carry_sc)

    x1 = _mix_residual(x_ref, os5_ref, omla_ref, ogla_ref, wout_ref)
    y_ref[...] = x1
    hn_sc[...] = _rms(x1, gf_ref[...]).astype(BF16)
    sub = 8
    row = lax.broadcasted_iota(jnp.int32, (sub, FF_CHUNK), 0)

    def up_proj(c):
        hn = hn_sc[...]
        return (jnp.dot(hn, win_ref[:, c * FF_CHUNK:(c + 1) * FF_CHUNK], preferred_element_type=F32),
                jnp.dot(hn, win_ref[:, D_FF + c * FF_CHUNK:D_FF + (c + 1) * FF_CHUNK],
                        preferred_element_type=F32))

    ahead = up_proj(0)
    for c in range(FF_NC):
        cols = slice(c * FF_CHUNK, (c + 1) * FF_CHUNK)
        val, gate = ahead
        if c + 1 < FF_NC:
            ahead = up_proj(c + 1)
        prev = carry_sc[:, cols]
        r1 = pltpu.roll(gate, 1, 0)
        r2 = pltpu.roll(gate, 2, 0)
        head1 = jnp.where(row == 0, prev[1:2], r1[:sub])
        head2 = jnp.where(row == 0, prev[0:1], jnp.where(row == 1, prev[1:2], r2[:sub]))
        g1 = jnp.concatenate([head1, r1[sub:]], axis=0)
        g2 = jnp.concatenate([head2, r2[sub:]], axis=0)
        conv = cb_ref[:, cols] + cw_ref[0:1, cols] * g2 + cw_ref[1:2, cols] * g1 + cw_ref[2:3, cols] * gate
        a = jax.nn.gelu(conv) * val
        part = jnp.dot(a.astype(BF16), wo_ref[cols, :], preferred_element_type=F32)
        if c == 0:
            acc_sc[...] = part
        else:
            acc_sc[...] += part
        carry_sc[0:2, cols] = gate[tm - 2:tm]

    y = y_ref[...] + acc_sc[...]
    if final_norm:
        y = _rms(y, gfin_ref[...])
    y_ref[...] = y
    buf_ref[0] = carry_sc[0:2, :]


def _ffn_prompt(x2d, os5, omla, ogla, prm, layer, *, n_seq, seq_len, final_g=None):
    tm = FFN_TM
    nj = seq_len // tm
    row = lambda b, j: (b * nj + j, 0)
    final_norm = final_g is not None
    in_specs = [pl.BlockSpec((tm, D_MODEL), row), pl.BlockSpec((tm, S5_W), row),
                pl.BlockSpec((tm, MLA_W), row), pl.BlockSpec((tm, GLA_W), row),
                _layer_spec(layer, (D_MODEL, D_MODEL), resident=True), _layer_spec(layer, (1, D_MODEL)),
                _layer_spec(layer, (D_MODEL, 2 * D_FF), resident=True), _layer_spec(layer, (3, D_FF)),
                _layer_spec(layer, (1, D_FF)), _layer_spec(layer, (D_FF, D_MODEL), resident=True)]
    args = [x2d, os5, omla, ogla, prm["wout"], prm["ffn_g"], prm["ffn_win"], prm["ffn_cw"],
            prm["ffn_cb"], prm["ffn_wo"]]
    if final_norm:
        in_specs.append(_const_spec((1, D_MODEL)))
        args.append(final_g)
    return pl.pallas_call(
        functools.partial(_ffn_kernel, final_norm=final_norm),
        grid=(n_seq, nj),
        in_specs=in_specs,
        out_specs=[pl.BlockSpec((tm, D_MODEL), row),
                   pl.BlockSpec((1, 2, D_FF), lambda b, j: (b, 0, 0))],
        out_shape=[jax.ShapeDtypeStruct((n_seq * seq_len, D_MODEL), F32),
                   jax.ShapeDtypeStruct((n_seq, 2, D_FF), F32)],
        scratch_shapes=[pltpu.VMEM((tm, D_MODEL), BF16), pltpu.VMEM((tm, D_MODEL), F32),
                        pltpu.VMEM((8, D_FF), F32)],
        compiler_params=_cparams(("parallel", "arbitrary")),
        name="ffn",
    )(*args)


def _ffn_step_kernel(x_ref, os5_ref, olat_ref, wuv_ref, ogla_ref, wout_ref, gf_ref, b0_ref, b1_ref,
                     wv_ref, wg_ref, cw_ref, cb_ref, wo_ref, *rest, final_norm):
    if final_norm:
        gfin_ref, y_ref, gate_ref, x1_sc, hn_sc, acc_sc = rest
    else:
        y_ref, gate_ref, x1_sc, hn_sc, acc_sc = rest
    c = pl.program_id(0)

    @pl.when(c == 0)
    def _():
        omla = jnp.dot(olat_ref[...].astype(BF16), wuv_ref[...], preferred_element_type=F32)
        x1 = (x_ref[...]
              + jnp.dot(os5_ref[...].astype(BF16), wout_ref[:S5_W], preferred_element_type=F32)
              + jnp.dot(omla.astype(BF16), wout_ref[S5_W:S5_W + MLA_W], preferred_element_type=F32)
              + jnp.dot(ogla_ref[...].astype(BF16), wout_ref[S5_W + MLA_W:],
                        preferred_element_type=F32))
        x1_sc[...] = x1
        hn_sc[...] = _rms(x1, gf_ref[...]).astype(BF16)
        acc_sc[...] = jnp.zeros_like(acc_sc)

    hn = hn_sc[...]
    val = jnp.dot(hn, wv_ref[...], preferred_element_type=F32)
    gate = jnp.dot(hn, wg_ref[...], preferred_element_type=F32)
    cw = cw_ref[...]
    conv = cb_ref[...] + cw[0:1] * b0_ref[...] + cw[1:2] * b1_ref[...] + cw[2:3] * gate
    a = jax.nn.gelu(conv) * val
    acc_sc[...] += jnp.dot(a.astype(BF16), wo_ref[...], preferred_element_type=F32)
    gate_ref[...] = gate

    @pl.when(c == FF_NC - 1)
    def _():
        y = x1_sc[...] + acc_sc[...]
        if final_norm:
            y = _rms(y, gfin_ref[...])
        y_ref[...] = y


def _ffn_step(x2d, os5, olat, ogla, conv_state, prm, layer, *, final_g=None):
    n = x2d.shape[0]
    final_norm = final_g is not None
    lcol = lambda c: (layer, 0, c)
    in_specs = [_const_spec((n, D_MODEL)), _const_spec((n, S5_W)),
                _const_spec((n, MLA_H * MLA_L)), _layer_spec(layer, (MLA_H * MLA_L, MLA_W)),
                _const_spec((n, GLA_W)), _layer_spec(layer, (D_MODEL, D_MODEL)),
                _layer_spec(layer, (1, D_MODEL)),
                pl.BlockSpec((None, n, FF_CHUNK), lcol),
                pl.BlockSpec((None, n, FF_CHUNK), lambda c: (layer, 0, FF_NC + c)),
                pl.BlockSpec((None, D_MODEL, FF_CHUNK), lcol),
                pl.BlockSpec((None, D_MODEL, FF_CHUNK), lambda c: (layer, 0, FF_NC + c)),
                pl.BlockSpec((None, 3, FF_CHUNK), lcol), pl.BlockSpec((None, 1, FF_CHUNK), lcol),
                pl.BlockSpec((None, FF_CHUNK, D_MODEL), lambda c: (layer, c, 0))]
    args = [x2d, os5, olat, prm["mla_wuv_bd"], ogla, prm["wout"], prm["ffn_g"], conv_state, conv_state,
            prm["ffn_win"], prm["ffn_win"], prm["ffn_cw"], prm["ffn_cb"], prm["ffn_wo"]]
    if final_norm:
        in_specs.append(_const_spec((1, D_MODEL)))
        args.append(final_g)
    return pl.pallas_call(
        functools.partial(_ffn_step_kernel, final_norm=final_norm),
        grid=(FF_NC,),
        in_specs=in_specs,
        out_specs=[_const_spec((n, D_MODEL)), pl.BlockSpec((n, FF_CHUNK), lambda c: (0, c))],
        out_shape=[jax.ShapeDtypeStruct((n, D_MODEL), F32), jax.ShapeDtypeStruct((n, D_FF), F32)],
        scratch_shapes=[pltpu.VMEM((n, D_MODEL), F32), pltpu.VMEM((n, D_MODEL), BF16),
                        pltpu.VMEM((n, D_MODEL), F32)],
        compiler_params=_cparams(("arbitrary",)),
        name="ffn_step",
    )(*args)


def _rot_cols(w):
    half = MLA_ROPE // 2
    return jnp.concatenate([-w[..., half:], w[..., :half]], axis=-1)


def _pad_cols(w, width):
    return jnp.pad(w, [(0, 0)] * (w.ndim - 1) + [(0, width - w.shape[-1])])


def _prepare_params(norm_mix_g, w_in, s5_a_re, s5_a_im, s5_log_dt, s5_b_re, s5_b_im, s5_c_re, s5_c_im,
                    s5_d, s5_w_glu, s5_b_glu, mla_q_norm_g, mla_w_qb, mla_kv_norm_g, mla_w_uk, mla_w_uv,
                    gla_w_gate, gla_b_gate, gla_norm_g, w_out, norm_ffn_g, w_ffn_in, ffn_conv_w, ffn_conv_b,
                    w_ffn_out):
    row = lambda v: v[:, None, :]
    o = 0
    cols = {}
    for name, width in (("u", 256), ("cq", 256), ("ckv", 128), ("kpe", 32), ("gq", 128), ("gk", 128),
                        ("gv", 256), ("glow", 16), ("gr", 256)):
        cols[name] = w_in[:, :, o:o + width]
        o += width
    in_w = jnp.concatenate(
        [cols["u"], cols["cq"], cols["ckv"], _pad_cols(cols["kpe"], LANE),
         _pad_cols(_rot_cols(cols["kpe"]), LANE), cols["gq"], cols["gk"], cols["gv"],
         _pad_cols(cols["glow"], LANE), cols["gr"]], axis=2).astype(BF16)

    dt = jnp.exp(s5_log_dt)[:, :, None]
    mag = jnp.exp(s5_a_re * dt)
    ab_re, ab_im = mag * jnp.cos(s5_a_im * dt), mag * jnp.sin(s5_a_im * dt)
    den = s5_a_re * s5_a_re + s5_a_im * s5_a_im
    k_re = ((ab_re - 1.0) * s5_a_re + ab_im * s5_a_im) / den
    k_im = (ab_im * s5_a_re - (ab_re - 1.0) * s5_a_im) / den
    bb_re = k_re[..., None] * s5_b_re - k_im[..., None] * s5_b_im
    bb_im = k_re[..., None] * s5_b_im + k_im[..., None] * s5_b_re
    eye = jnp.eye(S5_G, dtype=F32)
    bd_in = lambda m: jnp.einsum("lgpc,gh->lgchp", m, eye).reshape(DEPTH, S5_W, S5_N)
    bd_out = lambda m: jnp.einsum("lgcp,gh->lgphc", m, eye).reshape(DEPTH, S5_N, S5_W)

    wqb = mla_w_qb.reshape(DEPTH, MLA_QL, MLA_H, MLA_NOPE + MLA_ROPE)
    w_nope = wqb[..., :MLA_NOPE].reshape(DEPTH, MLA_QL, MLA_H * MLA_NOPE)
    w_rope = wqb[..., MLA_NOPE:]
    heads = lambda m: _pad_cols(m, LANE).reshape(DEPTH, MLA_QL, MLA_H * LANE)
    eye_h = jnp.eye(MLA_H, dtype=F32)
    wuk = jnp.einsum("dlhn,hk->dhnkl", mla_w_uk, eye_h).reshape(DEPTH, MLA_H * MLA_NOPE, MLA_H * MLA_L)
    wuv = jnp.einsum("dlhv,hk->dhlkv", mla_w_uv, eye_h).astype(BF16)

    seg = (jnp.arange(GLA_W)[:, None] // GLA_DV == jnp.arange(GLA_W)[None, :] // GLA_DV)
    tri = jnp.arange(GLA_CHUNK)[None, :] <= jnp.arange(GLA_CHUNK)[:, None]
    return dict(
        in_g=row(norm_mix_g), in_w=in_w,
        s5_a=jnp.stack([ab_re.reshape(DEPTH, S5_N), ab_im.reshape(DEPTH, S5_N)], axis=1),
        s5_bm=jnp.concatenate([bd_in(bb_re), bd_in(bb_im)], axis=2).astype(BF16),
        s5_cm=jnp.concatenate([bd_out(s5_c_re), -bd_out(s5_c_im)], axis=1).astype(BF16),
        s5_d=row(s5_d), s5_wglu=s5_w_glu.astype(BF16), s5_bglu=row(s5_b_glu),
        mla_gq=row(mla_q_norm_g), mla_gkv=row(mla_kv_norm_g),
        mla_wq=jnp.concatenate([w_nope, heads(w_rope), heads(_rot_cols(w_rope))], axis=2).astype(BF16),
        mla_wuk=wuk.astype(BF16),
        mla_wuv=wuv.reshape(DEPTH, MLA_H, MLA_L, MLA_W),
        mla_wuv_bd=wuv.reshape(DEPTH, MLA_H * MLA_L, MLA_W),
        gla_wgate=jnp.pad(gla_w_gate, ((0, 0), (0, LANE - GLA_R), (0, 0))).astype(BF16),
        gla_bgate=row(gla_b_gate), gla_g=row(gla_norm_g),
        gla_seg=seg.astype(F32) / GLA_DV, gla_tri=tri.astype(F32),
        wout=w_out.astype(BF16), ffn_g=row(norm_ffn_g), ffn_win=w_ffn_in.astype(BF16),
        ffn_cw=ffn_conv_w, ffn_cb=row(ffn_conv_b), ffn_wo=w_ffn_out.astype(BF16))


def _rope_tables(pos):
    half = MLA_ROPE // 2
    inv = ROPE_THETA ** (-jnp.arange(half, dtype=F32) / half)
    ang = pos.astype(F32)[:, None] * inv[None, :]
    cos = jnp.cos(ang)
    sin = jnp.sin(ang)
    return (_pad_cols(jnp.concatenate([cos, cos], axis=1), LANE),
            _pad_cols(jnp.concatenate([sin, sin], axis=1), LANE))


def _s5_state_out(hfin):
    shape = hfin.shape[:2] + (S5_G, S5_P)
    return jnp.stack([hfin[..., :S5_N].reshape(shape), hfin[..., S5_N:].reshape(shape)], axis=-1)


def kernel(x_prompt, x_sample, cache_mla_ckv, cache_mla_krope, page_table, state_s5, state_gla, state_ffn_conv, norm_mix_g, w_in, s5_a_re, s5_a_im, s5_log_dt, s5_b_re, s5_b_im, s5_c_re, s5_c_im, s5_d, s5_w_glu, s5_b_glu, mla_q_norm_g, mla_w_qb, mla_kv_norm_g, mla_w_uk, mla_w_uv, gla_w_gate, gla_b_gate, gla_norm_g, w_out, norm_ffn_g, w_ffn_in, ffn_conv_w, ffn_conv_b, w_ffn_out, norm_final_g):
    bp, sp = x_prompt.shape[:2]
    bs = x_sample.shape[0]
    n_pages = page_table.shape[1]
    past_len = n_pages * cache_mla_ckv.shape[2]
    kw = GLA_H * GLA_DK

    prm = _prepare_params(norm_mix_g, w_in, s5_a_re, s5_a_im, s5_log_dt, s5_b_re, s5_b_im, s5_c_re, s5_c_im,
                          s5_d, s5_w_glu, s5_b_glu, mla_q_norm_g, mla_w_qb, mla_kv_norm_g, mla_w_uk,
                          mla_w_uv, gla_w_gate, gla_b_gate, gla_norm_g, w_out, norm_ffn_g, w_ffn_in,
                          ffn_conv_w, ffn_conv_b, w_ffn_out)
    cos_p, sin_p = _rope_tables(jnp.arange(sp))
    cos_s, sin_s = _rope_tables(jnp.full((bs,), past_len))
    gfin = norm_final_g[None]
    krope_t = jnp.swapaxes(cache_mla_krope, 2, 3)
    h0_all = jnp.concatenate([state_s5[..., 0].reshape(DEPTH, bs, S5_N),
                              state_s5[..., 1].reshape(DEPTH, bs, S5_N)], axis=2)
    gla_all = state_gla.reshape(DEPTH, bs, kw, GLA_DV)
    conv_all = state_ffn_conv.reshape(DEPTH, bs, 2 * D_FF)
    h0_zero = jnp.zeros((bp, 2 * S5_N), F32)
    xp = x_prompt.reshape(bp * sp, D_MODEL)
    xd = x_sample.reshape(bs, D_MODEL)
    outs = [[] for _ in range(10)]

    for l in range(DEPTH):
        final_g = gfin if l == DEPTH - 1 else None

        u, pm, pg = _in_proj(xp, prm, l, tm=512)
        o_s5, hfin = _s5(u, h0_zero, prm, l, r_rows=bp, t_total=sp, t_steps=128)
        qcat, kcat, ckv, kpe = _mla_prep(pm, cos_p, sin_p, prm, l, n_seq=bp, seq_len=sp, tm=512,
                                         q_dtype=BF16)
        o_mla = _attn_prompt(qcat, kcat, prm, l, n_seq=bp, seq_len=sp)
        o_gla, st = _gla_prompt(pg, prm, l, n_seq=bp, seq_len=sp)
        xp, cbuf = _ffn_prompt(xp, o_s5, o_mla, o_gla, prm, l, n_seq=bp, seq_len=sp, final_g=final_g)
        for k, v in zip(range(5), (ckv, kpe, hfin, st, cbuf)):
            outs[k].append(v)

        u, pm, pg = _in_proj(xd, prm, l, tm=bs)
        o_s5, hfin = _s5(u, h0_all[l], prm, l, r_rows=bs, t_total=1, t_steps=1)
        qcat, kcat, ckv, kpe = _mla_prep(pm, cos_s, sin_s, prm, l, n_seq=1, seq_len=bs, tm=bs, q_dtype=F32)
        o_lat = _attn_decode(page_table, qcat.transpose(1, 0, 2), kcat[:, None, :], cache_mla_ckv,
                             krope_t, layer=l)
        o_gla, st = _gla_step(pg, gla_all, prm, l)
        xd, gate = _ffn_step(xd, o_s5, o_lat.reshape(bs, MLA_H * MLA_L), o_gla, conv_all, prm, l,
                             final_g=final_g)
        for k, v in zip(range(5, 10), (ckv, kpe, hfin, st, gate)):
            outs[k].append(v)

    outs = [jnp.stack(o) for o in outs]
    st5 = outs[3].reshape(DEPTH, bp, GLA_H, GLA_DV, GLA_H, GLA_DK)
    gla_p = jnp.stack([st5[:, :, h, :, h, :] for h in range(GLA_H)], axis=2).swapaxes(3, 4)
    conv_s = jnp.stack([state_ffn_conv[:, :, 1], outs[9]], axis=2)
    return (xp.reshape(bp, sp, D_MODEL), xd.reshape(bs, 1, D_MODEL),
            outs[0].reshape(DEPTH, bp, sp // PAGE, PAGE, MLA_L),
            outs[1].reshape(DEPTH, bp, sp // PAGE, PAGE, MLA_ROPE),
            _s5_state_out(outs[2]), gla_p, outs[4],
            outs[5][:, :, None, :], outs[6][:, :, None, :], _s5_state_out(outs[7]),
            outs[8].reshape(DEPTH, bs, GLA_H, GLA_DK, GLA_DV), conv_s)
```

```python
import functools
import math

import jax
import jax.numpy as jnp
from jax import lax
from jax.experimental import pallas as pl
from jax.experimental.pallas import tpu as pltpu

F32 = jnp.float32
BF16 = jnp.bfloat16

D_MODEL = 1024
DEPTH = 4
PAGE = 128
S5_W = 256
S5_G = 16
S5_C = 16
S5_P = 64
S5_N = S5_G * S5_P
MLA_H = 8
MLA_NOPE = 64
MLA_ROPE = 32
MLA_V = 64
MLA_W = MLA_H * MLA_V
MLA_QL = 256
MLA_L = 128
MLA_SCALE = (MLA_NOPE + MLA_ROPE) ** -0.5
Q_SCALE = MLA_SCALE * math.log2(math.e)
ROPE_THETA = 10000.0
GLA_H = 4
GLA_W = 256
GLA_DV = 64
GLA_DK = 32
GLA_R = 16
GLA_NORM = 16.0
GLA_CHUNK = 64
D_FF = 2816
RMS_EPS = 1e-6
NEG_INF = -1e30

LANE = 128
PM_W = 640
PG_W = 896
W_EXT = S5_W + PM_W + PG_W
QA_W = MLA_H * MLA_NOPE + 2 * MLA_H * MLA_ROPE
KC_W = 2 * LANE

FF_CHUNK = 256
FF_NC = D_FF // FF_CHUNK
VMEM_LIMIT = 56 * 1024 * 1024


def _cparams(sem, vmem=VMEM_LIMIT):
    return pltpu.CompilerParams(dimension_semantics=sem, vmem_limit_bytes=vmem)


def _rms(x, g):
    ms = jnp.mean(x * x, axis=-1, keepdims=True)
    return x * lax.rsqrt(ms + RMS_EPS) * g


def _const_spec(shape):
    nd = len(shape)
    return pl.BlockSpec(shape, lambda *_: (0,) * nd)


def _layer_spec(layer, shape, resident=False):
    nd = len(shape)
    return pl.BlockSpec((None,) + tuple(shape), lambda *_: (layer,) + (0,) * nd,
                        pipeline_mode=pl.Buffered(1) if resident else None)


def _in_proj_kernel(x_ref, g_ref, w_ref, u_ref, pm_ref, pg_ref):
    h = _rms(x_ref[...], g_ref[...]).astype(BF16)
    u_ref[...] = jnp.dot(h, w_ref[:, :S5_W], preferred_element_type=F32)
    pm_ref[...] = jnp.dot(h, w_ref[:, S5_W:S5_W + PM_W], preferred_element_type=F32)
    pg_ref[...] = jnp.dot(h, w_ref[:, S5_W + PM_W:], preferred_element_type=F32)


def _in_proj(x2d, prm, layer, *, tm):
    n = x2d.shape[0]
    row = lambda i: (i, 0)
    return pl.pallas_call(
        _in_proj_kernel,
        grid=(n // tm,),
        in_specs=[pl.BlockSpec((tm, D_MODEL), row), _layer_spec(layer, (1, D_MODEL)),
                  _layer_spec(layer, (D_MODEL, W_EXT))],
        out_specs=[pl.BlockSpec((tm, S5_W), row), pl.BlockSpec((tm, PM_W), row),
                   pl.BlockSpec((tm, PG_W), row)],
        out_shape=[jax.ShapeDtypeStruct((n, S5_W), F32), jax.ShapeDtypeStruct((n, PM_W), F32),
                   jax.ShapeDtypeStruct((n, PG_W), F32)],
        compiler_params=_cparams(("parallel",)),
        name="in_proj",
    )(x2d, prm["in_g"], prm["in_w"])


def _s5_kernel(u_ref, h0_ref, a_ref, bm_ref, cm_ref, d_ref, wg_ref, bg_ref,
               o_ref, hfin_ref, h_sc, bu_sc, *io_sc, r_rows, t_steps):
    @pl.when(pl.program_id(0) == 0)
    def _():
        h_sc[...] = h0_ref[...]

    n_slab = S5_W // LANE
    if t_steps == 1:
        u = u_ref[...]
    else:
        u_sc, y_sc = io_sc
        for r in range(r_rows):
            for k in range(n_slab):
                u_sc[k, pl.ds(r, t_steps, stride=r_rows), :] = u_ref[r, :, k * LANE:(k + 1) * LANE]
        u = jnp.concatenate([u_sc[k] for k in range(n_slab)], axis=1)
    rows = r_rows * t_steps
    u16 = u.astype(BF16)
    ar = jnp.broadcast_to(a_ref[0:1, :], (r_rows, S5_N))
    ai = jnp.broadcast_to(a_ref[1:2, :], (r_rows, S5_N))

    n_half = 2 if t_steps > 1 else 1
    th = t_steps // n_half

    def pieces(h):
        lo, mid, hi = h * th * r_rows, (2 * h + 1) * th * r_rows // 2, (h + 1) * th * r_rows
        return [slice(lo, mid), slice(mid, hi)]

    def project_in(h):
        for p in pieces(h):
            bu_sc[p, :] = jnp.dot(u16[p], bm_ref[...], preferred_element_type=F32)

    def scan(h, hr, hi):
        for t in range(h * th, (h + 1) * th):
            r = slice(t * r_rows, (t + 1) * r_rows)
            hr, hi = (ar * hr - ai * hi + bu_sc[r, :S5_N], ar * hi + ai * hr + bu_sc[r, S5_N:])
            bu_sc[r, :S5_N] = hr
            bu_sc[r, S5_N:] = hi
        return hr, hi

    def project_out(h):
        ps = pieces(h)
        y = [jnp.dot(bu_sc[p, :].astype(BF16), cm_ref[...], preferred_element_type=F32) for p in ps]
        y = [jax.nn.gelu(v + d_ref[...] * u[p]) for v, p in zip(y, ps)]
        z = [jnp.dot(v.astype(BF16), wg_ref[...], preferred_element_type=F32) + bg_ref[...] for v in y]
        return [v * jax.nn.sigmoid(g) for v, g in zip(y, z)]

    for h in range(n_half):
        project_in(h)
    hr, hi = h_sc[:, :S5_N], h_sc[:, S5_N:]
    out = []
    for h in range(n_half):
        hr, hi = scan(h, hr, hi)
        out += project_out(h)
    out = jnp.concatenate(out, axis=0)
    h_sc[:, :S5_N] = hr
    h_sc[:, S5_N:] = hi
    hfin_ref[:, :S5_N] = hr
    hfin_ref[:, S5_N:] = hi
    if t_steps == 1:
        o_ref[...] = out
    else:
        for k in range(n_slab):
            y_sc[k] = out[:, k * LANE:(k + 1) * LANE]
        for r in range(r_rows):
            for k in range(n_slab):
                o_ref[r, :, k * LANE:(k + 1) * LANE] = y_sc[k, pl.ds(r, t_steps, stride=r_rows), :]


def _s5(u, h0, prm, layer, *, r_rows, t_total, t_steps):
    rows = r_rows * t_steps
    kern = functools.partial(_s5_kernel, r_rows=r_rows, t_steps=t_steps)
    if t_steps == 1:
        io_spec = pl.BlockSpec((r_rows, S5_W), lambda i: (0, 0))
        io_shape = (r_rows, S5_W)
        io_scratch = []
    else:
        io_spec = pl.BlockSpec((r_rows, t_steps, S5_W), lambda i: (0, i, 0))
        io_shape = (r_rows, t_total, S5_W)
        io_scratch = [pltpu.VMEM((S5_W // LANE, rows, LANE), F32)] * 2
    o, hfin = pl.pallas_call(
        kern,
        grid=(t_total // t_steps,),
        in_specs=[io_spec, _const_spec((r_rows, 2 * S5_N)), _layer_spec(layer, (2, S5_N)),
                  _layer_spec(layer, (S5_W, 2 * S5_N)), _layer_spec(layer, (2 * S5_N, S5_W)),
                  _layer_spec(layer, (1, S5_W)), _layer_spec(layer, (S5_W, S5_W)),
                  _layer_spec(layer, (1, S5_W))],
        out_specs=[io_spec, _const_spec((r_rows, 2 * S5_N))],
        out_shape=[jax.ShapeDtypeStruct(io_shape, F32),
                   jax.ShapeDtypeStruct((r_rows, 2 * S5_N), F32)],
        scratch_shapes=[pltpu.VMEM((r_rows, 2 * S5_N), F32),
                        pltpu.VMEM((rows, 2 * S5_N), F32)] + io_scratch,
        compiler_params=_cparams(("arbitrary",)),
        name="s5",
    )(u.reshape(io_shape), h0, prm["s5_a"], prm["s5_bm"], prm["s5_cm"], prm["s5_d"], prm["s5_wglu"],
      prm["s5_bglu"])
    return o.reshape(r_rows * t_total, S5_W), hfin


def _mla_prep_kernel(pm_ref, cos_ref, sin_ref, gq_ref, wq_ref, wuk_ref, gkv_ref,
                     qcat_ref, kcat_ref, ckv_ref, kpe_ref):
    pm = pm_ref[...]
    cos4 = cos_ref[...]
    sin4 = sin_ref[...]
    first = lax.broadcasted_iota(jnp.int32, cos4.shape, 1) < MLA_ROPE
    qn = _rms(pm[:, :MLA_QL], gq_ref[...]).astype(BF16)
    qa = jnp.dot(qn, wq_ref[...], preferred_element_type=F32)
    n_nope = MLA_H * MLA_NOPE
    n_rope = MLA_H * MLA_ROPE
    qlat = jnp.dot(qa[:, :n_nope].astype(BF16), wuk_ref[...], preferred_element_type=F32)
    per_blk = LANE // MLA_ROPE
    roped = [qa[:, n_nope + b * LANE:n_nope + (b + 1) * LANE] * cos4
             + qa[:, n_nope + n_rope + b * LANE:n_nope + n_rope + (b + 1) * LANE] * sin4
             for b in range(MLA_H // per_blk)]
    for h in range(MLA_H):
        v = roped[h // per_blk]
        shift = (h % per_blk) * MLA_ROPE
        if shift:
            v = pltpu.roll(v, LANE - shift, 1)
        qpe = jnp.where(first, v, 0.0)
        qcat_ref[h, :, :LANE] = (qlat[:, h * LANE:(h + 1) * LANE] * Q_SCALE).astype(qcat_ref.dtype)
        qcat_ref[h, :, LANE:] = (qpe * Q_SCALE).astype(qcat_ref.dtype)
    ckv = _rms(pm[:, MLA_QL:MLA_QL + MLA_L], gkv_ref[...])
    kpe = jnp.where(first, pm[:, MLA_QL + MLA_L:MLA_QL + 2 * MLA_L] * cos4
                    + pm[:, MLA_QL + 2 * MLA_L:] * sin4, 0.0)
    ckv_ref[...] = ckv
    kpe_ref[...] = kpe[:, :MLA_ROPE]
    kcat_ref[:, :LANE] = ckv.astype(kcat_ref.dtype)
    kcat_ref[:, LANE:] = kpe.astype(kcat_ref.dtype)


def _mla_prep(pm, cos_t, sin_t, prm, layer, *, n_seq, seq_len, tm, q_dtype):
    n = n_seq * seq_len
    nj = seq_len // tm
    row = lambda b, j: (b * nj + j, 0)
    pos = lambda b, j: (j, 0)
    return pl.pallas_call(
        _mla_prep_kernel,
        grid=(n_seq, nj),
        in_specs=[pl.BlockSpec((tm, PM_W), row),
                  pl.BlockSpec((tm, LANE), pos), pl.BlockSpec((tm, LANE), pos),
                  _layer_spec(layer, (1, MLA_QL)), _layer_spec(layer, (MLA_QL, QA_W)),
                  _layer_spec(layer, (MLA_H * MLA_NOPE, MLA_H * LANE)), _layer_spec(layer, (1, MLA_L))],
        out_specs=[pl.BlockSpec((MLA_H, tm, KC_W), lambda b, j: (0, b * nj + j, 0)),
                   pl.BlockSpec((tm, KC_W), row),
                   pl.BlockSpec((tm, MLA_L), row), pl.BlockSpec((tm, MLA_ROPE), row)],
        out_shape=[jax.ShapeDtypeStruct((MLA_H, n, KC_W), q_dtype),
                   jax.ShapeDtypeStruct((n, KC_W), q_dtype),
                   jax.ShapeDtypeStruct((n, MLA_L), F32),
                   jax.ShapeDtypeStruct((n, MLA_ROPE), F32)],
        compiler_params=_cparams(("parallel", "parallel")),
        name="mla_prep",
    )(pm, cos_t, sin_t, prm["mla_gq"], prm["mla_wq"], prm["mla_wuk"], prm["mla_gkv"])


ATT_TQ = 256
ATT_TK = 512
ATT_NG = 4


def _attn_kernel(q_ref, k_ref, wuv_ref, o_ref, m_sc, acc_sc, *, seq_len):
    rows = MLA_H * ATT_TQ
    hg = MLA_H // ATT_NG
    grows = hg * ATT_TQ
    nt = (((1,), (1,)), ((), ()))
    ones = jnp.ones((ATT_TK, LANE), BF16)
    pairs = [(i, j) for i in range(seq_len // ATT_TQ) for j in range(i * ATT_TQ // ATT_TK + 1)]

    def keys(j):
        return k_ref[j * ATT_TK:(j + 1) * ATT_TK, :]

    def scores(i, j):
        kt = keys(j)
        return [lax.dot_general(q_ref[g * hg:(g + 1) * hg, i * ATT_TQ:(i + 1) * ATT_TQ, :].reshape(grows, KC_W),
                                kt, nt, preferred_element_type=F32) for g in range(ATT_NG)]

    ahead = scores(*pairs[0])
    for n, (i, j) in enumerate(pairs):
        cur = ahead
        if n + 1 < len(pairs):
            ahead = scores(*pairs[n + 1])
        last = j == i * ATT_TQ // ATT_TK
        kt = keys(j)
        v_ext = jnp.concatenate([kt[:, :MLA_L], ones], axis=1)
        if last:
            qpos = i * ATT_TQ + lax.broadcasted_iota(jnp.int32, (ATT_TQ, ATT_TK), 0)
            kpos = j * ATT_TK + lax.broadcasted_iota(jnp.int32, (ATT_TQ, ATT_TK), 1)
            visible = (kpos <= qpos)[None]
        for g in range(ATT_NG):
            r = slice(g * grows, (g + 1) * grows)
            s = cur[g]
            if last:
                s = jnp.where(visible, s.reshape(hg, ATT_TQ, ATT_TK), NEG_INF).reshape(grows, ATT_TK)
            if j == 0:
                m_old = jnp.full((grows, LANE), NEG_INF, F32)
            else:
                m_old = m_sc[r, :]
            m_new = jnp.maximum(m_old, jnp.max(s, axis=-1, keepdims=True))
            p = jnp.exp2(s - jnp.tile(m_new, (1, ATT_TK // LANE)))
            pv = jnp.dot(p.astype(kt.dtype), v_ext, preferred_element_type=F32)
            if j == 0:
                acc_sc[r, :] = pv
            else:
                acc_sc[r, :] = jnp.tile(jnp.exp2(m_old - m_new), (1, 2)) * acc_sc[r, :] + pv
            m_sc[r, :] = m_new
        if last:
            acc = acc_sc[...]
            o_lat = (acc[:, :MLA_L] / acc[:, MLA_L:]).astype(wuv_ref.dtype)
            out = jnp.dot(o_lat[:ATT_TQ], wuv_ref[0], preferred_element_type=F32)
            for h in range(1, MLA_H):
                out += jnp.dot(o_lat[h * ATT_TQ:(h + 1) * ATT_TQ], wuv_ref[h], preferred_element_type=F32)
            o_ref[i * ATT_TQ:(i + 1) * ATT_TQ, :] = out


def _attn_prompt(qcat, kcat, prm, layer, *, n_seq, seq_len):
    rows = MLA_H * ATT_TQ
    return pl.pallas_call(
        functools.partial(_attn_kernel, seq_len=seq_len),
        grid=(n_seq,),
        in_specs=[pl.BlockSpec((MLA_H, seq_len, KC_W), lambda b: (0, b, 0)),
                  pl.BlockSpec((seq_len, KC_W), lambda b: (b, 0)),
                  _layer_spec(layer, (MLA_H, MLA_L, MLA_W))],
        out_specs=pl.BlockSpec((seq_len, MLA_W), lambda b: (b, 0)),
        out_shape=jax.ShapeDtypeStruct((n_seq * seq_len, MLA_W), F32),
        scratch_shapes=[pltpu.VMEM((rows, LANE), F32), pltpu.VMEM((rows, 2 * MLA_L), F32)],
        compiler_params=_cparams(("parallel",)),
        name="mla_attn",
    )(qcat, kcat, prm["mla_wuv"])


DEC_G = 64
DEC_NS = 8


def _page_copies(pt_ref, ckv_hbm, kpe_hbm, ckv_buf, kpe_buf, sems, layer, b, c, slot):
    copies = []
    for g in range(DEC_G):
        page = pt_ref[b, c * DEC_G + g]
        dst = pl.ds(g * PAGE, PAGE)
        copies.append(pltpu.make_async_copy(ckv_hbm.at[layer, page], ckv_buf.at[slot, dst, :],
                                            sems.at[0, slot]))
        copies.append(pltpu.make_async_copy(kpe_hbm.at[layer, page], kpe_buf.at[slot, :, dst],
                                            sems.at[1, slot]))
    return copies


def _start_pages(copies):
    for i, cp in enumerate(copies):
        cp.start(priority=(i // 2) % 2)


def _decode_kernel(pt_ref, q_ref, knew_ref, ckv_hbm, kpe_hbm, o_ref, ckv_buf, kpe_buf, sems,
                   *, layer, n_chunks):
    b = pl.program_id(0)
    n_seq = pl.num_programs(0)
    copies = functools.partial(_page_copies, pt_ref, ckv_hbm, kpe_hbm, ckv_buf, kpe_buf, sems, layer)

    @pl.when(b == 0)
    def _():
        _start_pages(copies(b, 0, 0))

    q = q_ref[0]
    nt = (((1,), (1,)), ((), ()))
    q_lat = q[:, :MLA_L].astype(BF16)
    q_pe = q[:, MLA_L:MLA_L + MLA_ROPE].astype(BF16)
    sub = DEC_G * PAGE // DEC_NS
    blocks = [pl.ds(i * sub, sub) for i in range(DEC_NS)]
    m = jnp.full((MLA_H, 1), NEG_INF, F32)
    l = jnp.zeros((MLA_H, 1), F32)
    acc = jnp.zeros((MLA_H, MLA_L), F32)
    for c in range(n_chunks):
        slot = c % 2
        for cp in copies(b, c, slot):
            cp.wait()
        if c + 1 < n_chunks:
            _start_pages(copies(b, c + 1, 1 - slot))
        else:
            _start_pages(copies(lax.rem(b + 1, n_seq), 0, 1 - slot))
        ck = [ckv_buf[slot, blk, :].astype(BF16) for blk in blocks]
        s_lat = [lax.dot_general(q_lat, k, nt, preferred_element_type=F32) for k in ck]
        s_pe = [jnp.dot(q_pe, kpe_buf[slot, :, blk].astype(BF16), preferred_element_type=F32)
                for blk in blocks]
        s = [x + y for x, y in zip(s_lat, s_pe)]
        m_new = m
        for x in s:
            m_new = jnp.maximum(m_new, jnp.max(x, axis=-1, keepdims=True))
        alpha = jnp.exp2(m - m_new)
        p = [jnp.exp2(x - m_new) for x in s]
        pv = [jnp.dot(x.astype(BF16), k, preferred_element_type=F32) for x, k in zip(p, ck)]
        l = alpha * l + sum(jnp.sum(x, axis=-1, keepdims=True) for x in p)
        acc = alpha * acc + sum(pv)
        m = m_new

    kn = knew_ref[0]
    s_new = jnp.sum(q * kn, axis=-1, keepdims=True)
    m_fin = jnp.maximum(m, s_new)
    a_fin = jnp.exp2(m - m_fin)
    p_new = jnp.exp2(s_new - m_fin)
    o_ref[0] = (a_fin * acc + p_new * kn[:, :MLA_L]) / (a_fin * l + p_new)

    @pl.when(b == n_seq - 1)
    def _():
        for cp in copies(0, 0, n_chunks % 2):
            cp.wait()


def _attn_decode(page_table, q_dec, knew, cache_ckv, cache_kpe, *, layer):
    n_seq, n_pages = page_table.shape
    n_chunks = n_pages // DEC_G
    assert n_chunks % 2 == 0, "buffer roles must repeat from one sequence to the next"
    keys = DEC_G * PAGE
    kern = functools.partial(_decode_kernel, layer=layer, n_chunks=n_chunks)
    grid_spec = pltpu.PrefetchScalarGridSpec(
        num_scalar_prefetch=1,
        grid=(n_seq,),
        in_specs=[pl.BlockSpec((1, MLA_H, KC_W), lambda b, pt: (b, 0, 0)),
                  pl.BlockSpec((1, 1, KC_W), lambda b, pt: (b, 0, 0)),
                  pl.BlockSpec(memory_space=pl.ANY), pl.BlockSpec(memory_space=pl.ANY)],
        out_specs=pl.BlockSpec((1, MLA_H, MLA_L), lambda b, pt: (b, 0, 0)),
        scratch_shapes=[pltpu.VMEM((2, keys, MLA_L), F32), pltpu.VMEM((2, MLA_ROPE, keys), F32),
                        pltpu.SemaphoreType.DMA((2, 2))])
    return pl.pallas_call(
        kern,
        grid_spec=grid_spec,
        out_shape=jax.ShapeDtypeStruct((n_seq, MLA_H, MLA_L), F32),
        compiler_params=_cparams(("arbitrary",)),
        name="mla_decode",
    )(page_table, q_dec, knew, cache_ckv, cache_kpe)


GLA_TG = 512


def _head_norm_gate(o, gr, g_ref, seg_ref):
    ms = jnp.dot(o * o, seg_ref[...], preferred_element_type=F32, precision=lax.Precision.HIGHEST)
    return o * lax.rsqrt(ms + RMS_EPS) * g_ref[...] * (gr * jax.nn.sigmoid(gr))


def _gla_gates(glow, wgate_ref, bgate_ref):
    z = jnp.dot(glow.astype(BF16), wgate_ref[...], preferred_element_type=F32) + bgate_ref[...]
    return jax.nn.log_sigmoid(z) / GLA_NORM


def _gla_kernel(pg_ref, wgate_ref, bgate_ref, g_ref, seg_ref, tri_ref, o_ref, st_ref, st_sc, o_sc):
    @pl.when(pl.program_id(1) == 0)
    def _():
        st_sc[...] = jnp.zeros_like(st_sc)

    kw = GLA_H * GLA_DK
    c = GLA_CHUNK
    pg = pg_ref[...]
    logg = _gla_gates(pg[:, 2 * kw + GLA_W:2 * kw + GLA_W + LANE], wgate_ref, bgate_ref)
    hm_q = (lax.broadcasted_iota(jnp.int32, (GLA_H, c, kw), 2) // GLA_DK
            == lax.broadcasted_iota(jnp.int32, (GLA_H, c, kw), 0))
    hm_o = (lax.broadcasted_iota(jnp.int32, (GLA_H, c, GLA_W), 2) // GLA_DV
            == lax.broadcasted_iota(jnp.int32, (GLA_H, c, GLA_W), 0))
    hm_s = (lax.broadcasted_iota(jnp.int32, (GLA_W, kw), 0) // GLA_DV
            == lax.broadcasted_iota(jnp.int32, (GLA_W, kw), 1) // GLA_DK)
    causal = (lax.broadcasted_iota(jnp.int32, (c, c), 1) <= lax.broadcasted_iota(jnp.int32, (c, c), 0))
    nt = (((1,), (1,)), ((), ()))
    chunks = [slice(ci * c, (ci + 1) * c) for ci in range(GLA_TG // c)]
    bcum = [jnp.dot(tri_ref[...], logg[r], preferred_element_type=F32, precision=lax.Precision.HIGHEST)
            for r in chunks]
    v16 = [pg[r, 2 * kw:2 * kw + GLA_W].astype(BF16) for r in chunks]
    qe, att, ut, decay = [], [], [], []
    for r, b in zip(chunks, bcum):
        b_last = b[c - 1:c, :]
        k = pg[r, kw:2 * kw]
        qe_c = pg[r, :kw] * (GLA_DK ** -0.5) * jnp.exp(b)
        ke = (k * jnp.exp(-b)).astype(BF16)
        kd = (k * jnp.exp(b_last - b)).astype(BF16)
        qe.append(qe_c.astype(BF16))
        decay.append(jnp.exp(b_last))
        qs = jnp.where(hm_q, qe_c[None], 0.0).reshape(GLA_H * c, kw).astype(BF16)
        att.append(lax.dot_general(qs, ke, nt, preferred_element_type=F32))
        ut.append((pg[r, 2 * kw:2 * kw + GLA_W].T.astype(BF16), kd))
    ut = [jnp.dot(vt, kd, preferred_element_type=F32) for vt, kd in ut]
    oi = [jnp.dot(jnp.where(causal[None], a.reshape(GLA_H, c, c), 0.0).reshape(GLA_H * c, c).astype(BF16),
                  v, preferred_element_type=F32) for a, v in zip(att, v16)]
    st = st_sc[...]
    for r, q, o, u, d in zip(chunks, qe, oi, ut, decay):
        o_intra = jnp.sum(jnp.where(hm_o, o.reshape(GLA_H, c, GLA_W), 0.0), axis=0)
        o_inter = lax.dot_general(q, st.astype(BF16), nt, preferred_element_type=F32)
        o_sc[r, :] = o_intra + o_inter
        st = st * d + jnp.where(hm_s, u, 0.0)
    st_sc[...] = st
    st_ref[0] = st
    gr = pg[:, 2 * kw + GLA_W + LANE:]
    o_ref[...] = _head_norm_gate(o_sc[...], gr, g_ref, seg_ref)


def _gla_prompt(pg, prm, layer, *, n_seq, seq_len):
    nj = seq_len // GLA_TG
    kw = GLA_H * GLA_DK
    row = lambda b, j: (b * nj + j, 0)
    return pl.pallas_call(
        _gla_kernel,
        grid=(n_seq, nj),
        in_specs=[pl.BlockSpec((GLA_TG, PG_W), row),
                  _layer_spec(layer, (LANE, kw)), _layer_spec(layer, (1, kw)),
                  _layer_spec(layer, (1, GLA_W)),
                  _const_spec((GLA_W, GLA_W)), _const_spec((GLA_CHUNK, GLA_CHUNK))],
        out_specs=[pl.BlockSpec((GLA_TG, GLA_W), row),
                   pl.BlockSpec((1, GLA_W, kw), lambda b, j: (b, 0, 0))],
        out_shape=[jax.ShapeDtypeStruct((n_seq * seq_len, GLA_W), F32),
                   jax.ShapeDtypeStruct((n_seq, GLA_W, kw), F32)],
        scratch_shapes=[pltpu.VMEM((GLA_W, kw), F32), pltpu.VMEM((GLA_TG, GLA_W), F32)],
        compiler_params=_cparams(("parallel", "arbitrary")),
        name="gla",
    )(pg, prm["gla_wgate"], prm["gla_bgate"], prm["gla_g"], prm["gla_seg"], prm["gla_tri"])


def _gla_step_kernel(pg_ref, s_ref, wgate_ref, bgate_ref, g_ref, seg_ref, o_ref, snew_ref):
    kw = GLA_H * GLA_DK
    pg = pg_ref[...]
    n = pg.shape[0]
    q = pg[:, :kw] * (GLA_DK ** -0.5)
    k = pg[:, kw:2 * kw]
    v = pg[:, 2 * kw:2 * kw + GLA_W]
    decay = jnp.exp(_gla_gates(pg[:, 2 * kw + GLA_W:2 * kw + GLA_W + LANE], wgate_ref, bgate_ref))
    q_t, k_t, d_t = q.T, k.T, decay.T
    o_rows = []
    for b in range(n):
        vexp = jnp.concatenate(
            [jnp.broadcast_to(v[b:b + 1, h * GLA_DV:(h + 1) * GLA_DV], (GLA_DK, GLA_DV))
             for h in range(GLA_H)], axis=0)
        s_new = d_t[:, b:b + 1] * s_ref[b] + k_t[:, b:b + 1] * vexp
        snew_ref[b] = s_new
        w = q_t[:, b:b + 1] * s_new
        o_rows.append(jnp.concatenate(
            [jnp.sum(w[h * GLA_DK:(h + 1) * GLA_DK], axis=0, keepdims=True) for h in range(GLA_H)],
            axis=1))
    o = jnp.concatenate(o_rows, axis=0)
    o_ref[...] = _head_norm_gate(o, pg[:, 2 * kw + GLA_W + LANE:], g_ref, seg_ref)


def _gla_step(pg, state, prm, layer):
    n = pg.shape[0]
    kw = GLA_H * GLA_DK
    return pl.pallas_call(
        _gla_step_kernel,
        grid=(1,),
        in_specs=[_const_spec((n, PG_W)), _layer_spec(layer, (n, kw, GLA_DV)),
                  _layer_spec(layer, (LANE, kw)), _layer_spec(layer, (1, kw)),
                  _layer_spec(layer, (1, GLA_W)), _const_spec((GLA_W, GLA_W))],
        out_specs=[_const_spec((n, GLA_W)), _const_spec((n, kw, GLA_DV))],
        out_shape=[jax.ShapeDtypeStruct((n, GLA_W), F32),
                   jax.ShapeDtypeStruct((n, kw, GLA_DV), F32)],
        compiler_params=_cparams(("arbitrary",)),
        name="gla_step",
    )(pg, state, prm["gla_wgate"], prm["gla_bgate"], prm["gla_g"], prm["gla_seg"])


FFN_TM = 512


def _mix_residual(x_ref, os5_ref, omla_ref, ogla_ref, wout_ref):
    return (x_ref[...]
            + jnp.dot(os5_ref[...].astype(BF16), wout_ref[:S5_W], preferred_element_type=F32)
            + jnp.dot(omla_ref[...].astype(BF16), wout_ref[S5_W:S5_W + MLA_W],
                      preferred_element_type=F32)
            + jnp.dot(ogla_ref[...].astype(BF16), wout_ref[S5_W + MLA_W:],
                      preferred_element_type=F32))


def _ffn_kernel(x_ref, os5_ref, omla_ref, ogla_ref, wout_ref, gf_ref, win_ref, cw_ref, cb_ref, wo_ref,
                *rest, final_norm):
    if final_norm:
        gfin_ref, y_ref, buf_ref, hn_sc, acc_sc, carry_sc = rest
    else:
        y_ref, buf_ref, hn_sc, acc_sc, carry_sc = rest
    tm = x_ref.shape[0]

    @pl.when(pl.program_id(1) == 0)
    def _():
        carry_sc[...] = jnp.zeros_like(carry_sc)

    x1 = _mix_residual(x_ref, os5_ref, omla_ref, ogla_ref, wout_ref)
    y_ref[...] = x1
    hn_sc[...] = _rms(x1, gf_ref[...]).astype(BF16)
    sub = 8
    row = lax.broadcasted_iota(jnp.int32, (sub, FF_CHUNK), 0)

    def up_proj(c):
        hn = hn_sc[...]
        return (jnp.dot(hn, win_ref[:, c * FF_CHUNK:(c + 1) * FF_CHUNK], preferred_element_type=F32),
                jnp.dot(hn, win_ref[:, D_FF + c * FF_CHUNK:D_FF + (c + 1) * FF_CHUNK],
                        preferred_element_type=F32))

    ahead = up_proj(0)
    for c in range(FF_NC):
        cols = slice(c * FF_CHUNK, (c + 1) * FF_CHUNK)
        val, gate = ahead
        if c + 1 < FF_NC:
            ahead = up_proj(c + 1)
        prev = carry_sc[:, cols]
        r1 = pltpu.roll(gate, 1, 0)
        r2 = pltpu.roll(gate, 2, 0)
        head1 = jnp.where(row == 0, prev[1:2], r1[:sub])
        head2 = jnp.where(row == 0, prev[0:1], jnp.where(row == 1, prev[1:2], r2[:sub]))
        g1 = jnp.concatenate([head1, r1[sub:]], axis=0)
        g2 = jnp.concatenate([head2, r2[sub:]], axis=0)
        conv = cb_ref[:, cols] + cw_ref[0:1, cols] * g2 + cw_ref[1:2, cols] * g1 + cw_ref[2:3, cols] * gate
        a = jax.nn.gelu(conv) * val
        part = jnp.dot(a.astype(BF16), wo_ref[cols, :], preferred_element_type=F32)
        if c == 0:
            acc_sc[...] = part
        else:
            acc_sc[...] += part
        carry_sc[0:2, cols] = gate[tm - 2:tm]

    y = y_ref[...] + acc_sc[...]
    if final_norm:
        y = _rms(y, gfin_ref[...])
    y_ref[...] = y
    buf_ref[0] = carry_sc[0:2, :]


def _ffn_prompt(x2d, os5, omla, ogla, prm, layer, *, n_seq, seq_len, final_g=None):
    tm = FFN_TM
    nj = seq_len // tm
    row = lambda b, j: (b * nj + j, 0)
    final_norm = final_g is not None
    in_specs = [pl.BlockSpec((tm, D_MODEL), row), pl.BlockSpec((tm, S5_W), row),
                pl.BlockSpec((tm, MLA_W), row), pl.BlockSpec((tm, GLA_W), row),
                _layer_spec(layer, (D_MODEL, D_MODEL), resident=True), _layer_spec(layer, (1, D_MODEL)),
                _layer_spec(layer, (D_MODEL, 2 * D_FF), resident=True), _layer_spec(layer, (3, D_FF)),
                _layer_spec(layer, (1, D_FF)), _layer_spec(layer, (D_FF, D_MODEL), resident=True)]
    args = [x2d, os5, omla, ogla, prm["wout"], prm["ffn_g"], prm["ffn_win"], prm["ffn_cw"],
            prm["ffn_cb"], prm["ffn_wo"]]
    if final_norm:
        in_specs.append(_const_spec((1, D_MODEL)))
        args.append(final_g)
    return pl.pallas_call(
        functools.partial(_ffn_kernel, final_norm=final_norm),
        grid=(n_seq, nj),
        in_specs=in_specs,
        out_specs=[pl.BlockSpec((tm, D_MODEL), row),
                   pl.BlockSpec((1, 2, D_FF), lambda b, j: (b, 0, 0))],
        out_shape=[jax.ShapeDtypeStruct((n_seq * seq_len, D_MODEL), F32),
                   jax.ShapeDtypeStruct((n_seq, 2, D_FF), F32)],
        scratch_shapes=[pltpu.VMEM((tm, D_MODEL), BF16), pltpu.VMEM((tm, D_MODEL), F32),
                        pltpu.VMEM((8, D_FF), F32)],
        compiler_params=_cparams(("parallel", "arbitrary")),
        name="ffn",
    )(*args)


def _ffn_step_kernel(x_ref, os5_ref, olat_ref, wuv_ref, ogla_ref, wout_ref, gf_ref, b0_ref, b1_ref,
                     wv_ref, wg_ref, cw_ref, cb_ref, wo_ref, *rest, final_norm):
    if final_norm:
        gfin_ref, y_ref, gate_ref, x1_sc, hn_sc, acc_sc = rest
    else:
        y_ref, gate_ref, x1_sc, hn_sc, acc_sc = rest
    c = pl.program_id(0)

    @pl.when(c == 0)
    def _():
        omla = jnp.dot(olat_ref[...].astype(BF16), wuv_ref[...], preferred_element_type=F32)
        x1 = (x_ref[...]
              + jnp.dot(os5_ref[...].astype(BF16), wout_ref[:S5_W], preferred_element_type=F32)
              + jnp.dot(omla.astype(BF16), wout_ref[S5_W:S5_W + MLA_W], preferred_element_type=F32)
              + jnp.dot(ogla_ref[...].astype(BF16), wout_ref[S5_W + MLA_W:],
                        preferred_element_type=F32))
        x1_sc[...] = x1
        hn_sc[...] = _rms(x1, gf_ref[...]).astype(BF16)
        acc_sc[...] = jnp.zeros_like(acc_sc)

    hn = hn_sc[...]
    val = jnp.dot(hn, wv_ref[...], preferred_element_type=F32)
    gate = jnp.dot(hn, wg_ref[...], preferred_element_type=F32)
    cw = cw_ref[...]
    conv = cb_ref[...] + cw[0:1] * b0_ref[...] + cw[1:2] * b1_ref[...] + cw[2:3] * gate
    a = jax.nn.gelu(conv) * val
    acc_sc[...] += jnp.dot(a.astype(BF16), wo_ref[...], preferred_element_type=F32)
    gate_ref[...] = gate

    @pl.when(c == FF_NC - 1)
    def _():
        y = x1_sc[...] + acc_sc[...]
        if final_norm:
            y = _rms(y, gfin_ref[...])
        y_ref[...] = y


def _ffn_step(x2d, os5, olat, ogla, conv_state, prm, layer, *, final_g=None):
    n = x2d.shape[0]
    final_norm = final_g is not None
    lcol = lambda c: (layer, 0, c)
    in_specs = [_const_spec((n, D_MODEL)), _const_spec((n, S5_W)),
                _const_spec((n, MLA_H * MLA_L)), _layer_spec(layer, (MLA_H * MLA_L, MLA_W)),
                _const_spec((n, GLA_W)), _layer_spec(layer, (D_MODEL, D_MODEL)),
                _layer_spec(layer, (1, D_MODEL)),
                pl.BlockSpec((None, n, FF_CHUNK), lcol),
                pl.BlockSpec((None, n, FF_CHUNK), lambda c: (layer, 0, FF_NC + c)),
                pl.BlockSpec((None, D_MODEL, FF_CHUNK), lcol),
                pl.BlockSpec((None, D_MODEL, FF_CHUNK), lambda c: (layer, 0, FF_NC + c)),
                pl.BlockSpec((None, 3, FF_CHUNK), lcol), pl.BlockSpec((None, 1, FF_CHUNK), lcol),
                pl.BlockSpec((None, FF_CHUNK, D_MODEL), lambda c: (layer, c, 0))]
    args = [x2d, os5, olat, prm["mla_wuv_bd"], ogla, prm["wout"], prm["ffn_g"], conv_state, conv_state,
            prm["ffn_win"], prm["ffn_win"], prm["ffn_cw"], prm["ffn_cb"], prm["ffn_wo"]]
    if final_norm:
        in_specs.append(_const_spec((1, D_MODEL)))
        args.append(final_g)
    return pl.pallas_call(
        functools.partial(_ffn_step_kernel, final_norm=final_norm),
        grid=(FF_NC,),
        in_specs=in_specs,
        out_specs=[_const_spec((n, D_MODEL)), pl.BlockSpec((n, FF_CHUNK), lambda c: (0, c))],
        out_shape=[jax.ShapeDtypeStruct((n, D_MODEL), F32), jax.ShapeDtypeStruct((n, D_FF), F32)],
        scratch_shapes=[pltpu.VMEM((n, D_MODEL), F32), pltpu.VMEM((n, D_MODEL), BF16),
                        pltpu.VMEM((n, D_MODEL), F32)],
        compiler_params=_cparams(("arbitrary",)),
        name="ffn_step",
    )(*args)


def _rot_cols(w):
    half = MLA_ROPE // 2
    return jnp.concatenate([-w[..., half:], w[..., :half]], axis=-1)


def _pad_cols(w, width):
    return jnp.pad(w, [(0, 0)] * (w.ndim - 1) + [(0, width - w.shape[-1])])


def _prepare_params(norm_mix_g, w_in, s5_a_re, s5_a_im, s5_log_dt, s5_b_re, s5_b_im, s5_c_re, s5_c_im,
                    s5_d, s5_w_glu, s5_b_glu, mla_q_norm_g, mla_w_qb, mla_kv_norm_g, mla_w_uk, mla_w_uv,
                    gla_w_gate, gla_b_gate, gla_norm_g, w_out, norm_ffn_g, w_ffn_in, ffn_conv_w, ffn_conv_b,
                    w_ffn_out):
    row = lambda v: v[:, None, :]
    o = 0
    cols = {}
    for name, width in (("u", 256), ("cq", 256), ("ckv", 128), ("kpe", 32), ("gq", 128), ("gk", 128),
                        ("gv", 256), ("glow", 16), ("gr", 256)):
        cols[name] = w_in[:, :, o:o + width]
        o += width
    in_w = jnp.concatenate(
        [cols["u"], cols["cq"], cols["ckv"], _pad_cols(cols["kpe"], LANE),
         _pad_cols(_rot_cols(cols["kpe"]), LANE), cols["gq"], cols["gk"], cols["gv"],
         _pad_cols(cols["glow"], LANE), cols["gr"]], axis=2).astype(BF16)

    dt = jnp.exp(s5_log_dt)[:, :, None]
    mag = jnp.exp(s5_a_re * dt)
    ab_re, ab_im = mag * jnp.cos(s5_a_im * dt), mag * jnp.sin(s5_a_im * dt)
    den = s5_a_re * s5_a_re + s5_a_im * s5_a_im
    k_re = ((ab_re - 1.0) * s5_a_re + ab_im * s5_a_im) / den
    k_im = (ab_im * s5_a_re - (ab_re - 1.0) * s5_a_im) / den
    bb_re = k_re[..., None] * s5_b_re - k_im[..., None] * s5_b_im
    bb_im = k_re[..., None] * s5_b_im + k_im[..., None] * s5_b_re
    eye = jnp.eye(S5_G, dtype=F32)
    bd_in = lambda m: jnp.einsum("lgpc,gh->lgchp", m, eye).reshape(DEPTH, S5_W, S5_N)
    bd_out = lambda m: jnp.einsum("lgcp,gh->lgphc", m, eye).reshape(DEPTH, S5_N, S5_W)

    wqb = mla_w_qb.reshape(DEPTH, MLA_QL, MLA_H, MLA_NOPE + MLA_ROPE)
    w_nope = wqb[..., :MLA_NOPE].reshape(DEPTH, MLA_QL, MLA_H * MLA_NOPE)
    w_rope = wqb[..., MLA_NOPE:]
    heads = lambda m: m.reshape(DEPTH, MLA_QL, MLA_H * MLA_ROPE)
    eye_h = jnp.eye(MLA_H, dtype=F32)
    wuk = jnp.einsum("dlhn,hk->dhnkl", mla_w_uk, eye_h).reshape(DEPTH, MLA_H * MLA_NOPE, MLA_H * MLA_L)
    wuv = jnp.einsum("dlhv,hk->dhlkv", mla_w_uv, eye_h).astype(BF16)

    seg = (jnp.arange(GLA_W)[:, None] // GLA_DV == jnp.arange(GLA_W)[None, :] // GLA_DV)
    tri = jnp.arange(GLA_CHUNK)[None, :] <= jnp.arange(GLA_CHUNK)[:, None]
    return dict(
        in_g=row(norm_mix_g), in_w=in_w,
        s5_a=jnp.stack([ab_re.reshape(DEPTH, S5_N), ab_im.reshape(DEPTH, S5_N)], axis=1),
        s5_bm=jnp.concatenate([bd_in(bb_re), bd_in(bb_im)], axis=2).astype(BF16),
        s5_cm=jnp.concatenate([bd_out(s5_c_re), -bd_out(s5_c_im)], axis=1).astype(BF16),
        s5_d=row(s5_d), s5_wglu=s5_w_glu.astype(BF16), s5_bglu=row(s5_b_glu),
        mla_gq=row(mla_q_norm_g), mla_gkv=row(mla_kv_norm_g),
        mla_wq=jnp.concatenate([w_nope, heads(w_rope), heads(_rot_cols(w_rope))], axis=2).astype(BF16),
        mla_wuk=wuk.astype(BF16),
        mla_wuv=wuv.reshape(DEPTH, MLA_H, MLA_L, MLA_W),
        mla_wuv_bd=wuv.reshape(DEPTH, MLA_H * MLA_L, MLA_W),
        gla_wgate=jnp.pad(gla_w_gate, ((0, 0), (0, LANE - GLA_R), (0, 0))).astype(BF16),
        gla_bgate=row(gla_b_gate), gla_g=row(gla_norm_g),
        gla_seg=seg.astype(F32) / GLA_DV, gla_tri=tri.astype(F32),
        wout=w_out.astype(BF16), ffn_g=row(norm_ffn_g), ffn_win=w_ffn_in.astype(BF16),
        ffn_cw=ffn_conv_w, ffn_cb=row(ffn_conv_b), ffn_wo=w_ffn_out.astype(BF16))


def _rope_tables(pos):
    half = MLA_ROPE // 2
    inv = ROPE_THETA ** (-jnp.arange(half, dtype=F32) / half)
    ang = pos.astype(F32)[:, None] * inv[None, :]
    reps = 2 * LANE // MLA_ROPE
    return jnp.tile(jnp.cos(ang), (1, reps)), jnp.tile(jnp.sin(ang), (1, reps))


def _s5_state_out(hfin):
    shape = hfin.shape[:2] + (S5_G, S5_P)
    return jnp.stack([hfin[..., :S5_N].reshape(shape), hfin[..., S5_N:].reshape(shape)], axis=-1)


def kernel(x_prompt, x_sample, cache_mla_ckv, cache_mla_krope, page_table, state_s5, state_gla, state_ffn_conv, norm_mix_g, w_in, s5_a_re, s5_a_im, s5_log_dt, s5_b_re, s5_b_im, s5_c_re, s5_c_im, s5_d, s5_w_glu, s5_b_glu, mla_q_norm_g, mla_w_qb, mla_kv_norm_g, mla_w_uk, mla_w_uv, gla_w_gate, gla_b_gate, gla_norm_g, w_out, norm_ffn_g, w_ffn_in, ffn_conv_w, ffn_conv_b, w_ffn_out, norm_final_g):
    bp, sp = x_prompt.shape[:2]
    bs = x_sample.shape[0]
    n_pages = page_table.shape[1]
    past_len = n_pages * cache_mla_ckv.shape[2]
    kw = GLA_H * GLA_DK

    prm = _prepare_params(norm_mix_g, w_in, s5_a_re, s5_a_im, s5_log_dt, s5_b_re, s5_b_im, s5_c_re, s5_c_im,
                          s5_d, s5_w_glu, s5_b_glu, mla_q_norm_g, mla_w_qb, mla_kv_norm_g, mla_w_uk,
                          mla_w_uv, gla_w_gate, gla_b_gate, gla_norm_g, w_out, norm_ffn_g, w_ffn_in,
                          ffn_conv_w, ffn_conv_b, w_ffn_out)
    cos_p, sin_p = _rope_tables(jnp.arange(sp))
    cos_s, sin_s = _rope_tables(jnp.full((bs,), past_len))
    gfin = norm_final_g[None]
    krope_t = jnp.swapaxes(cache_mla_krope, 2, 3)
    h0_all = jnp.concatenate([state_s5[..., 0].reshape(DEPTH, bs, S5_N),
                              state_s5[..., 1].reshape(DEPTH, bs, S5_N)], axis=2)
    gla_all = state_gla.reshape(DEPTH, bs, kw, GLA_DV)
    conv_all = state_ffn_conv.reshape(DEPTH, bs, 2 * D_FF)
    h0_zero = jnp.zeros((bp, 2 * S5_N), F32)
    xp = x_prompt.reshape(bp * sp, D_MODEL)
    xd = x_sample.reshape(bs, D_MODEL)
    outs = [[] for _ in range(10)]

    for l in range(DEPTH):
        final_g = gfin if l == DEPTH - 1 else None

        u, pm, pg = _in_proj(xp, prm, l, tm=512)
        o_s5, hfin = _s5(u, h0_zero, prm, l, r_rows=bp, t_total=sp, t_steps=128)
        qcat, kcat, ckv, kpe = _mla_prep(pm, cos_p, sin_p, prm, l, n_seq=bp, seq_len=sp, tm=512,
                                         q_dtype=BF16)
        o_mla = _attn_prompt(qcat, kcat, prm, l, n_seq=bp, seq_len=sp)
        o_gla, st = _gla_prompt(pg, prm, l, n_seq=bp, seq_len=sp)
        xp, cbuf = _ffn_prompt(xp, o_s5, o_mla, o_gla, prm, l, n_seq=bp, seq_len=sp, final_g=final_g)
        for k, v in zip(range(5), (ckv, kpe, hfin, st, cbuf)):
            outs[k].append(v)

        u, pm, pg = _in_proj(xd, prm, l, tm=bs)
        o_s5, hfin = _s5(u, h0_all[l], prm, l, r_rows=bs, t_total=1, t_steps=1)
        qcat, kcat, ckv, kpe = _mla_prep(pm, cos_s, sin_s, prm, l, n_seq=1, seq_len=bs, tm=bs, q_dtype=F32)
        o_lat = _attn_decode(page_table, qcat.transpose(1, 0, 2), kcat[:, None, :], cache_mla_ckv,
                             krope_t, layer=l)
        o_gla, st = _gla_step(pg, gla_all, prm, l)
        xd, gate = _ffn_step(xd, o_s5, o_lat.reshape(bs, MLA_H * MLA_L), o_gla, conv_all, prm, l,
                             final_g=final_g)
        for k, v in zip(range(5, 10), (ckv, kpe, hfin, st, gate)):
            outs[k].append(v)

    outs = [jnp.stack(o) for o in outs]
    st5 = outs[3].reshape(DEPTH, bp, GLA_H, GLA_DV, GLA_H, GLA_DK)
    gla_p = jnp.stack([st5[:, :, h, :, h, :] for h in range(GLA_H)], axis=2).swapaxes(3, 4)
    conv_s = jnp.stack([state_ffn_conv[:, :, 1], outs[9]], axis=2)
    return (xp.reshape(bp, sp, D_MODEL), xd.reshape(bs, 1, D_MODEL),
            outs[0].reshape(DEPTH, bp, sp // PAGE, PAGE, MLA_L),
            outs[1].reshape(DEPTH, bp, sp // PAGE, PAGE, MLA_ROPE),
            _s5_state_out(outs[2]), gla_p, outs[4],
            outs[5][:, :, None, :], outs[6][:, :, None, :], _s5_state_out(outs[7]),
            outs[8].reshape(DEPTH, bs, GLA_H, GLA_DK, GLA_DV), conv_s)
```

```python
import functools
import math

import jax
import jax.numpy as jnp
from jax import lax
from jax.experimental import pallas as pl
from jax.experimental.pallas import tpu as pltpu

F32 = jnp.float32
BF16 = jnp.bfloat16

D_MODEL = 1024
DEPTH = 4
PAGE = 128
S5_W = 256
S5_G = 16
S5_C = 16
S5_P = 64
S5_N = S5_G * S5_P
MLA_H = 8
MLA_NOPE = 64
MLA_ROPE = 32
MLA_V = 64
MLA_W = MLA_H * MLA_V
MLA_QL = 256
MLA_L = 128
MLA_SCALE = (MLA_NOPE + MLA_ROPE) ** -0.5
Q_SCALE = MLA_SCALE * math.log2(math.e)
ROPE_THETA = 10000.0
GLA_H = 4
GLA_W = 256
GLA_DV = 64
GLA_DK = 32
GLA_R = 16
GLA_NORM = 16.0
GLA_CHUNK = 64
D_FF = 2816
RMS_EPS = 1e-6
NEG_INF = -1e30

LANE = 128
PM_W = 640
PG_W = 896
W_EXT = S5_W + PM_W + PG_W
QA_W = MLA_H * MLA_NOPE + 2 * MLA_H * MLA_ROPE
KC_W = 2 * LANE

FF_CHUNK = 256
FF_NC = D_FF // FF_CHUNK
VMEM_LIMIT = 56 * 1024 * 1024


def _cparams(sem, vmem=VMEM_LIMIT):
    return pltpu.CompilerParams(dimension_semantics=sem, vmem_limit_bytes=vmem)


def _rms(x, g):
    ms = jnp.mean(x * x, axis=-1, keepdims=True)
    return x * lax.rsqrt(ms + RMS_EPS) * g


def _const_spec(shape):
    nd = len(shape)
    return pl.BlockSpec(shape, lambda *_: (0,) * nd)


def _layer_spec(layer, shape, resident=False):
    nd = len(shape)
    return pl.BlockSpec((None,) + tuple(shape), lambda *_: (layer,) + (0,) * nd,
                        pipeline_mode=pl.Buffered(1) if resident else None)


def _in_proj_kernel(x_ref, g_ref, w_ref, u_ref, pm_ref, pg_ref):
    h = _rms(x_ref[...], g_ref[...]).astype(BF16)
    u_ref[...] = jnp.dot(h, w_ref[:, :S5_W], preferred_element_type=F32)
    pm_ref[...] = jnp.dot(h, w_ref[:, S5_W:S5_W + PM_W], preferred_element_type=F32)
    pg_ref[...] = jnp.dot(h, w_ref[:, S5_W + PM_W:], preferred_element_type=F32)


def _in_proj(x2d, prm, layer, *, tm):
    n = x2d.shape[0]
    row = lambda i: (i, 0)
    return pl.pallas_call(
        _in_proj_kernel,
        grid=(n // tm,),
        in_specs=[pl.BlockSpec((tm, D_MODEL), row), _layer_spec(layer, (1, D_MODEL)),
                  _layer_spec(layer, (D_MODEL, W_EXT))],
        out_specs=[pl.BlockSpec((tm, S5_W), row), pl.BlockSpec((tm, PM_W), row),
                   pl.BlockSpec((tm, PG_W), row)],
        out_shape=[jax.ShapeDtypeStruct((n, S5_W), F32), jax.ShapeDtypeStruct((n, PM_W), F32),
                   jax.ShapeDtypeStruct((n, PG_W), F32)],
        compiler_params=_cparams(("parallel",)),
        name="in_proj",
    )(x2d, prm["in_g"], prm["in_w"])


def _s5_kernel(u_ref, h0_ref, a_ref, bm_ref, cm_ref, d_ref, wg_ref, bg_ref,
               o_ref, hfin_ref, h_sc, bu_sc, *io_sc, r_rows, t_steps):
    @pl.when(pl.program_id(0) == 0)
    def _():
        h_sc[...] = h0_ref[...]

    n_slab = S5_W // LANE
    if t_steps == 1:
        u = u_ref[...]
    else:
        u_sc, y_sc = io_sc
        for r in range(r_rows):
            for k in range(n_slab):
                u_sc[k, pl.ds(r, t_steps, stride=r_rows), :] = u_ref[r, :, k * LANE:(k + 1) * LANE]
        u = jnp.concatenate([u_sc[k] for k in range(n_slab)], axis=1)
    rows = r_rows * t_steps
    u16 = u.astype(BF16)
    ar = jnp.broadcast_to(a_ref[0:1, :], (r_rows, S5_N))
    ai = jnp.broadcast_to(a_ref[1:2, :], (r_rows, S5_N))

    n_half = 2 if t_steps > 1 else 1
    th = t_steps // n_half

    def pieces(h):
        lo, mid, hi = h * th * r_rows, (2 * h + 1) * th * r_rows // 2, (h + 1) * th * r_rows
        return [slice(lo, mid), slice(mid, hi)]

    def project_in(h):
        for p in pieces(h):
            bu_sc[p, :] = jnp.dot(u16[p], bm_ref[...], preferred_element_type=F32)

    def scan(h, hr, hi):
        for t in range(h * th, (h + 1) * th):
            r = slice(t * r_rows, (t + 1) * r_rows)
            hr, hi = (ar * hr - ai * hi + bu_sc[r, :S5_N], ar * hi + ai * hr + bu_sc[r, S5_N:])
            bu_sc[r, :S5_N] = hr
            bu_sc[r, S5_N:] = hi
        return hr, hi

    def project_out(h):
        ps = pieces(h)
        y = [jnp.dot(bu_sc[p, :].astype(BF16), cm_ref[...], preferred_element_type=F32) for p in ps]
        y = [jax.nn.gelu(v + d_ref[...] * u[p]) for v, p in zip(y, ps)]
        z = [jnp.dot(v.astype(BF16), wg_ref[...], preferred_element_type=F32) + bg_ref[...] for v in y]
        return [v * jax.nn.sigmoid(g) for v, g in zip(y, z)]

    for h in range(n_half):
        project_in(h)
    hr, hi = h_sc[:, :S5_N], h_sc[:, S5_N:]
    out = []
    for h in range(n_half):
        hr, hi = scan(h, hr, hi)
        out += project_out(h)
    out = jnp.concatenate(out, axis=0)
    h_sc[:, :S5_N] = hr
    h_sc[:, S5_N:] = hi
    hfin_ref[:, :S5_N] = hr
    hfin_ref[:, S5_N:] = hi
    if t_steps == 1:
        o_ref[...] = out
    else:
        for k in range(n_slab):
            y_sc[k] = out[:, k * LANE:(k + 1) * LANE]
        for r in range(r_rows):
            for k in range(n_slab):
                o_ref[r, :, k * LANE:(k + 1) * LANE] = y_sc[k, pl.ds(r, t_steps, stride=r_rows), :]


def _s5(u, h0, prm, layer, *, r_rows, t_total, t_steps):
    rows = r_rows * t_steps
    kern = functools.partial(_s5_kernel, r_rows=r_rows, t_steps=t_steps)
    if t_steps == 1:
        io_spec = pl.BlockSpec((r_rows, S5_W), lambda i: (0, 0))
        io_shape = (r_rows, S5_W)
        io_scratch = []
    else:
        io_spec = pl.BlockSpec((r_rows, t_steps, S5_W), lambda i: (0, i, 0))
        io_shape = (r_rows, t_total, S5_W)
        io_scratch = [pltpu.VMEM((S5_W // LANE, rows, LANE), F32)] * 2
    o, hfin = pl.pallas_call(
        kern,
        grid=(t_total // t_steps,),
        in_specs=[io_spec, _const_spec((r_rows, 2 * S5_N)), _layer_spec(layer, (2, S5_N)),
                  _layer_spec(layer, (S5_W, 2 * S5_N)), _layer_spec(layer, (2 * S5_N, S5_W)),
                  _layer_spec(layer, (1, S5_W)), _layer_spec(layer, (S5_W, S5_W)),
                  _layer_spec(layer, (1, S5_W))],
        out_specs=[io_spec, _const_spec((r_rows, 2 * S5_N))],
        out_shape=[jax.ShapeDtypeStruct(io_shape, F32),
                   jax.ShapeDtypeStruct((r_rows, 2 * S5_N), F32)],
        scratch_shapes=[pltpu.VMEM((r_rows, 2 * S5_N), F32),
                        pltpu.VMEM((rows, 2 * S5_N), F32)] + io_scratch,
        compiler_params=_cparams(("arbitrary",)),
        name="s5",
    )(u.reshape(io_shape), h0, prm["s5_a"], prm["s5_bm"], prm["s5_cm"], prm["s5_d"], prm["s5_wglu"],
      prm["s5_bglu"])
    return o.reshape(r_rows * t_total, S5_W), hfin


def _mla_prep_kernel(pm_ref, cos_ref, sin_ref, gq_ref, wq_ref, wuk_ref, gkv_ref,
                     qcat_ref, kcat_ref, ckv_ref, kpe_ref):
    pm = pm_ref[...]
    cos4 = cos_ref[...]
    sin4 = sin_ref[...]
    first = lax.broadcasted_iota(jnp.int32, cos4.shape, 1) < MLA_ROPE
    qn = _rms(pm[:, :MLA_QL], gq_ref[...]).astype(BF16)
    qa = jnp.dot(qn, wq_ref[...], preferred_element_type=F32)
    n_nope = MLA_H * MLA_NOPE
    n_rope = MLA_H * MLA_ROPE
    qlat = jnp.dot(qa[:, :n_nope].astype(BF16), wuk_ref[...], preferred_element_type=F32)
    per_blk = LANE // MLA_ROPE
    roped = [qa[:, n_nope + b * LANE:n_nope + (b + 1) * LANE] * cos4
             + qa[:, n_nope + n_rope + b * LANE:n_nope + n_rope + (b + 1) * LANE] * sin4
             for b in range(MLA_H // per_blk)]
    for h in range(MLA_H):
        v = roped[h // per_blk]
        shift = (h % per_blk) * MLA_ROPE
        if shift:
            v = pltpu.roll(v, LANE - shift, 1)
        qpe = jnp.where(first, v, 0.0)
        qcat_ref[h, :, :LANE] = (qlat[:, h * LANE:(h + 1) * LANE] * Q_SCALE).astype(qcat_ref.dtype)
        qcat_ref[h, :, LANE:] = (qpe * Q_SCALE).astype(qcat_ref.dtype)
    ckv = _rms(pm[:, MLA_QL:MLA_QL + MLA_L], gkv_ref[...])
    kpe = jnp.where(first, pm[:, MLA_QL + MLA_L:MLA_QL + 2 * MLA_L] * cos4
                    + pm[:, MLA_QL + 2 * MLA_L:] * sin4, 0.0)
    ckv_ref[...] = ckv
    kpe_ref[...] = kpe[:, :MLA_ROPE]
    kcat_ref[:, :LANE] = ckv.astype(kcat_ref.dtype)
    kcat_ref[:, LANE:] = kpe.astype(kcat_ref.dtype)


def _mla_prep(pm, cos_t, sin_t, prm, layer, *, n_seq, seq_len, tm, q_dtype):
    n = n_seq * seq_len
    nj = seq_len // tm
    row = lambda b, j: (b * nj + j, 0)
    pos = lambda b, j: (j, 0)
    return pl.pallas_call(
        _mla_prep_kernel,
        grid=(n_seq, nj),
        in_specs=[pl.BlockSpec((tm, PM_W), row),
                  pl.BlockSpec((tm, LANE), pos), pl.BlockSpec((tm, LANE), pos),
                  _layer_spec(layer, (1, MLA_QL)), _layer_spec(layer, (MLA_QL, QA_W)),
                  _layer_spec(layer, (MLA_H * MLA_NOPE, MLA_H * LANE)), _layer_spec(layer, (1, MLA_L))],
        out_specs=[pl.BlockSpec((MLA_H, tm, KC_W), lambda b, j: (0, b * nj + j, 0)),
                   pl.BlockSpec((tm, KC_W), row),
                   pl.BlockSpec((tm, MLA_L), row), pl.BlockSpec((tm, MLA_ROPE), row)],
        out_shape=[jax.ShapeDtypeStruct((MLA_H, n, KC_W), q_dtype),
                   jax.ShapeDtypeStruct((n, KC_W), q_dtype),
                   jax.ShapeDtypeStruct((n, MLA_L), F32),
                   jax.ShapeDtypeStruct((n, MLA_ROPE), F32)],
        compiler_params=_cparams(("parallel", "parallel")),
        name="mla_prep",
    )(pm, cos_t, sin_t, prm["mla_gq"], prm["mla_wq"], prm["mla_wuk"], prm["mla_gkv"])


ATT_TQ = 256
ATT_TK = 512
ATT_NG = 4


def _attn_kernel(q_ref, k_ref, wuv_ref, o_ref, m_sc, acc_sc, *, seq_len):
    rows = MLA_H * ATT_TQ
    hg = MLA_H // ATT_NG
    grows = hg * ATT_TQ
    nt = (((1,), (1,)), ((), ()))
    ones = jnp.ones((ATT_TK, LANE), BF16)
    pairs = [(i, j) for i in range(seq_len // ATT_TQ) for j in range(i * ATT_TQ // ATT_TK + 1)]

    def keys(j):
        return k_ref[j * ATT_TK:(j + 1) * ATT_TK, :]

    def scores(i, j):
        kt = keys(j)
        return [lax.dot_general(q_ref[g * hg:(g + 1) * hg, i * ATT_TQ:(i + 1) * ATT_TQ, :].reshape(grows, KC_W),
                                kt, nt, preferred_element_type=F32) for g in range(ATT_NG)]

    ahead = scores(*pairs[0])
    for n, (i, j) in enumerate(pairs):
        cur = ahead
        if n + 1 < len(pairs):
            ahead = scores(*pairs[n + 1])
        last = j == i * ATT_TQ // ATT_TK
        kt = keys(j)
        v_ext = jnp.concatenate([kt[:, :MLA_L], ones], axis=1)
        if last:
            qpos = i * ATT_TQ + lax.broadcasted_iota(jnp.int32, (ATT_TQ, ATT_TK), 0)
            kpos = j * ATT_TK + lax.broadcasted_iota(jnp.int32, (ATT_TQ, ATT_TK), 1)
            visible = (kpos <= qpos)[None]
        for g in range(ATT_NG):
            r = slice(g * grows, (g + 1) * grows)
            s = cur[g]
            if last:
                s = jnp.where(visible, s.reshape(hg, ATT_TQ, ATT_TK), NEG_INF).reshape(grows, ATT_TK)
            if j == 0:
                m_old = jnp.full((grows, LANE), NEG_INF, F32)
            else:
                m_old = m_sc[r, :]
            m_new = jnp.maximum(m_old, jnp.max(s, axis=-1, keepdims=True))
            p = jnp.exp2(s - jnp.tile(m_new, (1, ATT_TK // LANE)))
            pv = jnp.dot(p.astype(kt.dtype), v_ext, preferred_element_type=F32)
            if j == 0:
                acc_sc[r, :] = pv
            else:
                acc_sc[r, :] = jnp.tile(jnp.exp2(m_old - m_new), (1, 2)) * acc_sc[r, :] + pv
            m_sc[r, :] = m_new
        if last:
            acc = acc_sc[...]
            o_lat = (acc[:, :MLA_L] / acc[:, MLA_L:]).astype(wuv_ref.dtype)
            out = jnp.dot(o_lat[:ATT_TQ], wuv_ref[0], preferred_element_type=F32)
            for h in range(1, MLA_H):
                out += jnp.dot(o_lat[h * ATT_TQ:(h + 1) * ATT_TQ], wuv_ref[h], preferred_element_type=F32)
            o_ref[i * ATT_TQ:(i + 1) * ATT_TQ, :] = out


def _attn_prompt(qcat, kcat, prm, layer, *, n_seq, seq_len):
    rows = MLA_H * ATT_TQ
    return pl.pallas_call(
        functools.partial(_attn_kernel, seq_len=seq_len),
        grid=(n_seq,),
        in_specs=[pl.BlockSpec((MLA_H, seq_len, KC_W), lambda b: (0, b, 0)),
                  pl.BlockSpec((seq_len, KC_W), lambda b: (b, 0)),
                  _layer_spec(layer, (MLA_H, MLA_L, MLA_W))],
        out_specs=pl.BlockSpec((seq_len, MLA_W), lambda b: (b, 0)),
        out_shape=jax.ShapeDtypeStruct((n_seq * seq_len, MLA_W), F32),
        scratch_shapes=[pltpu.VMEM((rows, LANE), F32), pltpu.VMEM((rows, 2 * MLA_L), F32)],
        compiler_params=_cparams(("parallel",)),
        name="mla_attn",
    )(qcat, kcat, prm["mla_wuv"])


DEC_G = 64
DEC_NS = 8
DEC_SEQ = 2
DEC_AHEAD = 2


def _page_copies(pt_ref, ckv_hbm, kpe_hbm, ckv_buf, kpe_buf, sems, layer, b, c, slot):
    copies = []
    for g in range(DEC_G):
        page = pt_ref[b, c * DEC_G + g]
        dst = pl.ds(g * PAGE, PAGE)
        copies.append(pltpu.make_async_copy(ckv_hbm.at[layer, page], ckv_buf.at[slot, dst, :],
                                            sems.at[0, slot]))
        copies.append(pltpu.make_async_copy(kpe_hbm.at[layer, page], kpe_buf.at[slot, :, dst],
                                            sems.at[1, slot]))
    return copies


def _start_pages(copies):
    for i, cp in enumerate(copies):
        cp.start(priority=(i // 2) % 2)


def _decode_kernel(pt_ref, q_ref, knew_ref, ckv_hbm, kpe_hbm, o_ref, ckv_buf, kpe_buf, sems,
                   *, layer, n_chunks):
    g = pl.program_id(0)
    n_steps = pl.num_programs(0)
    n_pieces = DEC_SEQ * n_chunks
    page_copies = functools.partial(_page_copies, pt_ref, ckv_hbm, kpe_hbm, ckv_buf, kpe_buf, sems, layer)

    def copies(step, k):
        return page_copies(step * DEC_SEQ + k // n_chunks, k % n_chunks, k)

    @pl.when(g == 0)
    def _():
        for k in range(DEC_AHEAD):
            _start_pages(copies(g, k))

    nt = (((1,), (1,)), ((), ()))
    sub = DEC_G * PAGE // DEC_NS
    blocks = [pl.ds(i * sub, sub) for i in range(DEC_NS)]
    for k in range(n_pieces):
        seq, c = k // n_chunks, k % n_chunks
        if c == 0:
            q = q_ref[seq]
            q_lat = q[:, :MLA_L].astype(BF16)
            q_pe = q[:, MLA_L:MLA_L + MLA_ROPE].astype(BF16)
            m = jnp.full((MLA_H, 1), NEG_INF, F32)
            l = jnp.zeros((MLA_H, 1), F32)
            acc = jnp.zeros((MLA_H, MLA_L), F32)
        for cp in copies(g, k):
            cp.wait()
        if k + DEC_AHEAD < n_pieces:
            _start_pages(copies(g, k + DEC_AHEAD))
        else:
            _start_pages(copies(lax.rem(g + 1, n_steps), k + DEC_AHEAD - n_pieces))
        ck = [ckv_buf[k, blk, :].astype(BF16) for blk in blocks]
        s_lat = [lax.dot_general(q_lat, kk, nt, preferred_element_type=F32) for kk in ck]
        s_pe = [jnp.dot(q_pe, kpe_buf[k, :, blk].astype(BF16), preferred_element_type=F32)
                for blk in blocks]
        s = [x + y for x, y in zip(s_lat, s_pe)]
        m_new = m
        for x in s:
            m_new = jnp.maximum(m_new, jnp.max(x, axis=-1, keepdims=True))
        alpha = jnp.exp2(m - m_new)
        p = [jnp.exp2(x - m_new) for x in s]
        pv = [jnp.dot(x.astype(BF16), kk, preferred_element_type=F32) for x, kk in zip(p, ck)]
        l = alpha * l + sum(jnp.sum(x, axis=-1, keepdims=True) for x in p)
        acc = alpha * acc + sum(pv)
        m = m_new
        if c == n_chunks - 1:
            kn = knew_ref[seq]
            s_new = jnp.sum(q * kn, axis=-1, keepdims=True)
            m_fin = jnp.maximum(m, s_new)
            a_fin = jnp.exp2(m - m_fin)
            p_new = jnp.exp2(s_new - m_fin)
            o_ref[seq] = (a_fin * acc + p_new * kn[:, :MLA_L]) / (a_fin * l + p_new)

    @pl.when(g == n_steps - 1)
    def _():
        for k in range(DEC_AHEAD):
            for cp in copies(0, k):
                cp.wait()


def _attn_decode(page_table, q_dec, knew, cache_ckv, cache_kpe, *, layer):
    n_seq, n_pages = page_table.shape
    n_chunks = n_pages // DEC_G
    n_pieces = DEC_SEQ * n_chunks
    assert n_seq % DEC_SEQ == 0 and DEC_AHEAD < n_pieces
    keys = DEC_G * PAGE
    kern = functools.partial(_decode_kernel, layer=layer, n_chunks=n_chunks)
    grid_spec = pltpu.PrefetchScalarGridSpec(
        num_scalar_prefetch=1,
        grid=(n_seq // DEC_SEQ,),
        in_specs=[pl.BlockSpec((DEC_SEQ, MLA_H, KC_W), lambda g, pt: (g, 0, 0)),
                  pl.BlockSpec((DEC_SEQ, 1, KC_W), lambda g, pt: (g, 0, 0)),
                  pl.BlockSpec(memory_space=pl.ANY), pl.BlockSpec(memory_space=pl.ANY)],
        out_specs=pl.BlockSpec((DEC_SEQ, MLA_H, MLA_L), lambda g, pt: (g, 0, 0)),
        scratch_shapes=[pltpu.VMEM((n_pieces, keys, MLA_L), F32),
                        pltpu.VMEM((n_pieces, MLA_ROPE, keys), F32),
                        pltpu.SemaphoreType.DMA((2, n_pieces))])
    return pl.pallas_call(
        kern,
        grid_spec=grid_spec,
        out_shape=jax.ShapeDtypeStruct((n_seq, MLA_H, MLA_L), F32),
        compiler_params=_cparams(("arbitrary",)),
        name="mla_decode",
    )(page_table, q_dec, knew, cache_ckv, cache_kpe)


GLA_TG = 512


def _head_norm_gate(o, gr, g_ref, seg_ref):
    ms = jnp.dot(o * o, seg_ref[...], preferred_element_type=F32, precision=lax.Precision.HIGHEST)
    return o * lax.rsqrt(ms + RMS_EPS) * g_ref[...] * (gr * jax.nn.sigmoid(gr))


def _gla_gates(glow, wgate_ref, bgate_ref):
    z = jnp.dot(glow.astype(BF16), wgate_ref[...], preferred_element_type=F32) + bgate_ref[...]
    return jax.nn.log_sigmoid(z) / GLA_NORM


def _gla_kernel(pg_ref, wgate_ref, bgate_ref, g_ref, seg_ref, tri_ref, o_ref, st_ref, st_sc, o_sc):
    @pl.when(pl.program_id(1) == 0)
    def _():
        st_sc[...] = jnp.zeros_like(st_sc)

    kw = GLA_H * GLA_DK
    c = GLA_CHUNK
    pg = pg_ref[...]
    logg = _gla_gates(pg[:, 2 * kw + GLA_W:2 * kw + GLA_W + LANE], wgate_ref, bgate_ref)
    hm_q = (lax.broadcasted_iota(jnp.int32, (GLA_H, c, kw), 2) // GLA_DK
            == lax.broadcasted_iota(jnp.int32, (GLA_H, c, kw), 0))
    hm_o = (lax.broadcasted_iota(jnp.int32, (GLA_H, c, GLA_W), 2) // GLA_DV
            == lax.broadcasted_iota(jnp.int32, (GLA_H, c, GLA_W), 0))
    hm_s = (lax.broadcasted_iota(jnp.int32, (GLA_W, kw), 0) // GLA_DV
            == lax.broadcasted_iota(jnp.int32, (GLA_W, kw), 1) // GLA_DK)
    causal = (lax.broadcasted_iota(jnp.int32, (c, c), 1) <= lax.broadcasted_iota(jnp.int32, (c, c), 0))
    nt = (((1,), (1,)), ((), ()))
    chunks = [slice(ci * c, (ci + 1) * c) for ci in range(GLA_TG // c)]
    bcum = [jnp.dot(tri_ref[...], logg[r], preferred_element_type=F32, precision=lax.Precision.HIGHEST)
            for r in chunks]
    v16 = [pg[r, 2 * kw:2 * kw + GLA_W].astype(BF16) for r in chunks]
    qe, att, ut, decay = [], [], [], []
    for r, b in zip(chunks, bcum):
        b_last = b[c - 1:c, :]
        k = pg[r, kw:2 * kw]
        qe_c = pg[r, :kw] * (GLA_DK ** -0.5) * jnp.exp(b)
        ke = (k * jnp.exp(-b)).astype(BF16)
        kd = (k * jnp.exp(b_last - b)).astype(BF16)
        qe.append(qe_c.astype(BF16))
        decay.append(jnp.exp(b_last))
        qs = jnp.where(hm_q, qe_c[None], 0.0).reshape(GLA_H * c, kw).astype(BF16)
        att.append(lax.dot_general(qs, ke, nt, preferred_element_type=F32))
        ut.append((pg[r, 2 * kw:2 * kw + GLA_W].T.astype(BF16), kd))
    ut = [jnp.dot(vt, kd, preferred_element_type=F32) for vt, kd in ut]
    oi = [jnp.dot(jnp.where(causal[None], a.reshape(GLA_H, c, c), 0.0).reshape(GLA_H * c, c).astype(BF16),
                  v, preferred_element_type=F32) for a, v in zip(att, v16)]
    st = st_sc[...]
    for r, q, o, u, d in zip(chunks, qe, oi, ut, decay):
        o_intra = jnp.sum(jnp.where(hm_o, o.reshape(GLA_H, c, GLA_W), 0.0), axis=0)
        o_inter = lax.dot_general(q, st.astype(BF16), nt, preferred_element_type=F32)
        o_sc[r, :] = o_intra + o_inter
        st = st * d + jnp.where(hm_s, u, 0.0)
    st_sc[...] = st
    st_ref[0] = st
    gr = pg[:, 2 * kw + GLA_W + LANE:]
    o_ref[...] = _head_norm_gate(o_sc[...], gr, g_ref, seg_ref)


def _gla_prompt(pg, prm, layer, *, n_seq, seq_len):
    nj = seq_len // GLA_TG
    kw = GLA_H * GLA_DK
    row = lambda b, j: (b * nj + j, 0)
    return pl.pallas_call(
        _gla_kernel,
        grid=(n_seq, nj),
        in_specs=[pl.BlockSpec((GLA_TG, PG_W), row),
                  _layer_spec(layer, (LANE, kw)), _layer_spec(layer, (1, kw)),
                  _layer_spec(layer, (1, GLA_W)),
                  _const_spec((GLA_W, GLA_W)), _const_spec((GLA_CHUNK, GLA_CHUNK))],
        out_specs=[pl.BlockSpec((GLA_TG, GLA_W), row),
                   pl.BlockSpec((1, GLA_W, kw), lambda b, j: (b, 0, 0))],
        out_shape=[jax.ShapeDtypeStruct((n_seq * seq_len, GLA_W), F32),
                   jax.ShapeDtypeStruct((n_seq, GLA_W, kw), F32)],
        scratch_shapes=[pltpu.VMEM((GLA_W, kw), F32), pltpu.VMEM((GLA_TG, GLA_W), F32)],
        compiler_params=_cparams(("parallel", "arbitrary")),
        name="gla",
    )(pg, prm["gla_wgate"], prm["gla_bgate"], prm["gla_g"], prm["gla_seg"], prm["gla_tri"])


def _gla_step_kernel(pg_ref, s_ref, wgate_ref, bgate_ref, g_ref, seg_ref, o_ref, snew_ref):
    kw = GLA_H * GLA_DK
    pg = pg_ref[...]
    n = pg.shape[0]
    q = pg[:, :kw] * (GLA_DK ** -0.5)
    k = pg[:, kw:2 * kw]
    v = pg[:, 2 * kw:2 * kw + GLA_W]
    decay = jnp.exp(_gla_gates(pg[:, 2 * kw + GLA_W:2 * kw + GLA_W + LANE], wgate_ref, bgate_ref))
    q_t, k_t, d_t = q.T, k.T, decay.T
    o_rows = []
    for b in range(n):
        vexp = jnp.concatenate(
            [jnp.broadcast_to(v[b:b + 1, h * GLA_DV:(h + 1) * GLA_DV], (GLA_DK, GLA_DV))
             for h in range(GLA_H)], axis=0)
        s_new = d_t[:, b:b + 1] * s_ref[b] + k_t[:, b:b + 1] * vexp
        snew_ref[b] = s_new
        w = q_t[:, b:b + 1] * s_new
        o_rows.append(jnp.concatenate(
            [jnp.sum(w[h * GLA_DK:(h + 1) * GLA_DK], axis=0, keepdims=True) for h in range(GLA_H)],
            axis=1))
    o = jnp.concatenate(o_rows, axis=0)
    o_ref[...] = _head_norm_gate(o, pg[:, 2 * kw + GLA_W + LANE:], g_ref, seg_ref)


def _gla_step(pg, state, prm, layer):
    n = pg.shape[0]
    kw = GLA_H * GLA_DK
    return pl.pallas_call(
        _gla_step_kernel,
        grid=(1,),
        in_specs=[_const_spec((n, PG_W)), _layer_spec(layer, (n, kw, GLA_DV)),
                  _layer_spec(layer, (LANE, kw)), _layer_spec(layer, (1, kw)),
                  _layer_spec(layer, (1, GLA_W)), _const_spec((GLA_W, GLA_W))],
        out_specs=[_const_spec((n, GLA_W)), _const_spec((n, kw, GLA_DV))],
        out_shape=[jax.ShapeDtypeStruct((n, GLA_W), F32),
                   jax.ShapeDtypeStruct((n, kw, GLA_DV), F32)],
        compiler_params=_cparams(("arbitrary",)),
        name="gla_step",
    )(pg, state, prm["gla_wgate"], prm["gla_bgate"], prm["gla_g"], prm["gla_seg"])


FFN_TM = 512


def _mix_residual(x_ref, os5_ref, omla_ref, ogla_ref, wout_ref):
    return (x_ref[...]
            + jnp.dot(os5_ref[...].astype(BF16), wout_ref[:S5_W], preferred_element_type=F32)
            + jnp.dot(omla_ref[...].astype(BF16), wout_ref[S5_W:S5_W + MLA_W],
                      preferred_element_type=F32)
            + jnp.dot(ogla_ref[...].astype(BF16), wout_ref[S5_W + MLA_W:],
                      preferred_element_type=F32))


def _ffn_kernel(x_ref, os5_ref, omla_ref, ogla_ref, wout_ref, gf_ref, win_ref, cw_ref, cb_ref, wo_ref,
                *rest, final_norm):
    if final_norm:
        gfin_ref, y_ref, buf_ref, hn_sc, acc_sc, carry_sc = rest
    else:
        y_ref, buf_ref, hn_sc, acc_sc, carry_sc = rest
    tm = x_ref.shape[0]

    @pl.when(pl.program_id(1) == 0)
    def _():
        carry_sc[...] = jnp.zeros_like(carry_sc)

    x1 = _mix_residual(x_ref, os5_ref, omla_ref, ogla_ref, wout_ref)
    y_ref[...] = x1
    hn_sc[...] = _rms(x1, gf_ref[...]).astype(BF16)
    sub = 8
    row = lax.broadcasted_iota(jnp.int32, (sub, FF_CHUNK), 0)

    def up_proj(c):
        hn = hn_sc[...]
        return (jnp.dot(hn, win_ref[:, c * FF_CHUNK:(c + 1) * FF_CHUNK], preferred_element_type=F32),
                jnp.dot(hn, win_ref[:, D_FF + c * FF_CHUNK:D_FF + (c + 1) * FF_CHUNK],
                        preferred_element_type=F32))

    ahead = up_proj(0)
    for c in range(FF_NC):
        cols = slice(c * FF_CHUNK, (c + 1) * FF_CHUNK)
        val, gate = ahead
        if c + 1 < FF_NC:
            ahead = up_proj(c + 1)
        prev = carry_sc[:, cols]
        r1 = pltpu.roll(gate, 1, 0)
        r2 = pltpu.roll(gate, 2, 0)
        head1 = jnp.where(row == 0, prev[1:2], r1[:sub])
        head2 = jnp.where(row == 0, prev[0:1], jnp.where(row == 1, prev[1:2], r2[:sub]))
        g1 = jnp.concatenate([head1, r1[sub:]], axis=0)
        g2 = jnp.concatenate([head2, r2[sub:]], axis=0)
        conv = cb_ref[:, cols] + cw_ref[0:1, cols] * g2 + cw_ref[1:2, cols] * g1 + cw_ref[2:3, cols] * gate
        a = jax.nn.gelu(conv) * val
        part = jnp.dot(a.astype(BF16), wo_ref[cols, :], preferred_element_type=F32)
        if c == 0:
            acc_sc[...] = part
        else:
            acc_sc[...] += part
        carry_sc[0:2, cols] = gate[tm - 2:tm]

    y = y_ref[...] + acc_sc[...]
    if final_norm:
        y = _rms(y, gfin_ref[...])
    y_ref[...] = y
    buf_ref[0] = carry_sc[0:2, :]


def _ffn_prompt(x2d, os5, omla, ogla, prm, layer, *, n_seq, seq_len, final_g=None):
    tm = FFN_TM
    nj = seq_len // tm
    row = lambda b, j: (b * nj + j, 0)
    final_norm = final_g is not None
    in_specs = [pl.BlockSpec((tm, D_MODEL), row), pl.BlockSpec((tm, S5_W), row),
                pl.BlockSpec((tm, MLA_W), row), pl.BlockSpec((tm, GLA_W), row),
                _layer_spec(layer, (D_MODEL, D_MODEL), resident=True), _layer_spec(layer, (1, D_MODEL)),
                _layer_spec(layer, (D_MODEL, 2 * D_FF), resident=True), _layer_spec(layer, (3, D_FF)),
                _layer_spec(layer, (1, D_FF)), _layer_spec(layer, (D_FF, D_MODEL), resident=True)]
    args = [x2d, os5, omla, ogla, prm["wout"], prm["ffn_g"], prm["ffn_win"], prm["ffn_cw"],
            prm["ffn_cb"], prm["ffn_wo"]]
    if final_norm:
        in_specs.append(_const_spec((1, D_MODEL)))
        args.append(final_g)
    return pl.pallas_call(
        functools.partial(_ffn_kernel, final_norm=final_norm),
        grid=(n_seq, nj),
        in_specs=in_specs,
        out_specs=[pl.BlockSpec((tm, D_MODEL), row),
                   pl.BlockSpec((1, 2, D_FF), lambda b, j: (b, 0, 0))],
        out_shape=[jax.ShapeDtypeStruct((n_seq * seq_len, D_MODEL), F32),
                   jax.ShapeDtypeStruct((n_seq, 2, D_FF), F32)],
        scratch_shapes=[pltpu.VMEM((tm, D_MODEL), BF16), pltpu.VMEM((tm, D_MODEL), F32),
                        pltpu.VMEM((8, D_FF), F32)],
        compiler_params=_cparams(("parallel", "arbitrary")),
        name="ffn",
    )(*args)


def _ffn_step_kernel(x_ref, os5_ref, olat_ref, wuv_ref, ogla_ref, wout_ref, gf_ref, b0_ref, b1_ref,
                     wv_ref, wg_ref, cw_ref, cb_ref, wo_ref, *rest, final_norm):
    if final_norm:
        gfin_ref, y_ref, gate_ref, x1_sc, hn_sc, acc_sc = rest
    else:
        y_ref, gate_ref, x1_sc, hn_sc, acc_sc = rest
    c = pl.program_id(0)

    @pl.when(c == 0)
    def _():
        omla = jnp.dot(olat_ref[...].astype(BF16), wuv_ref[...], preferred_element_type=F32)
        x1 = (x_ref[...]
              + jnp.dot(os5_ref[...].astype(BF16), wout_ref[:S5_W], preferred_element_type=F32)
              + jnp.dot(omla.astype(BF16), wout_ref[S5_W:S5_W + MLA_W], preferred_element_type=F32)
              + jnp.dot(ogla_ref[...].astype(BF16), wout_ref[S5_W + MLA_W:],
                        preferred_element_type=F32))
        x1_sc[...] = x1
        hn_sc[...] = _rms(x1, gf_ref[...]).astype(BF16)
        acc_sc[...] = jnp.zeros_like(acc_sc)

    hn = hn_sc[...]
    val = jnp.dot(hn, wv_ref[...], preferred_element_type=F32)
    gate = jnp.dot(hn, wg_ref[...], preferred_element_type=F32)
    cw = cw_ref[...]
    conv = cb_ref[...] + cw[0:1] * b0_ref[...] + cw[1:2] * b1_ref[...] + cw[2:3] * gate
    a = jax.nn.gelu(conv) * val
    acc_sc[...] += jnp.dot(a.astype(BF16), wo_ref[...], preferred_element_type=F32)
    gate_ref[...] = gate

    @pl.when(c == FF_NC - 1)
    def _():
        y = x1_sc[...] + acc_sc[...]
        if final_norm:
            y = _rms(y, gfin_ref[...])
        y_ref[...] = y


def _ffn_step(x2d, os5, olat, ogla, conv_state, prm, layer, *, final_g=None):
    n = x2d.shape[0]
    final_norm = final_g is not None
    lcol = lambda c: (layer, 0, c)
    in_specs = [_const_spec((n, D_MODEL)), _const_spec((n, S5_W)),
                _const_spec((n, MLA_H * MLA_L)), _layer_spec(layer, (MLA_H * MLA_L, MLA_W)),
                _const_spec((n, GLA_W)), _layer_spec(layer, (D_MODEL, D_MODEL)),
                _layer_spec(layer, (1, D_MODEL)),
                pl.BlockSpec((None, n, FF_CHUNK), lcol),
                pl.BlockSpec((None, n, FF_CHUNK), lambda c: (layer, 0, FF_NC + c)),
                pl.BlockSpec((None, D_MODEL, FF_CHUNK), lcol),
                pl.BlockSpec((None, D_MODEL, FF_CHUNK), lambda c: (layer, 0, FF_NC + c)),
                pl.BlockSpec((None, 3, FF_CHUNK), lcol), pl.BlockSpec((None, 1, FF_CHUNK), lcol),
                pl.BlockSpec((None, FF_CHUNK, D_MODEL), lambda c: (layer, c, 0))]
    args = [x2d, os5, olat, prm["mla_wuv_bd"], ogla, prm["wout"], prm["ffn_g"], conv_state, conv_state,
            prm["ffn_win"], prm["ffn_win"], prm["ffn_cw"], prm["ffn_cb"], prm["ffn_wo"]]
    if final_norm:
        in_specs.append(_const_spec((1, D_MODEL)))
        args.append(final_g)
    return pl.pallas_call(
        functools.partial(_ffn_step_kernel, final_norm=final_norm),
        grid=(FF_NC,),
        in_specs=in_specs,
        out_specs=[_const_spec((n, D_MODEL)), pl.BlockSpec((n, FF_CHUNK), lambda c: (0, c))],
        out_shape=[jax.ShapeDtypeStruct((n, D_MODEL), F32), jax.ShapeDtypeStruct((n, D_FF), F32)],
        scratch_shapes=[pltpu.VMEM((n, D_MODEL), F32), pltpu.VMEM((n, D_MODEL), BF16),
                        pltpu.VMEM((n, D_MODEL), F32)],
        compiler_params=_cparams(("arbitrary",)),
        name="ffn_step",
    )(*args)


def _rot_cols(w):
    half = MLA_ROPE // 2
    return jnp.concatenate([-w[..., half:], w[..., :half]], axis=-1)


def _pad_cols(w, width):
    return jnp.pad(w, [(0, 0)] * (w.ndim - 1) + [(0, width - w.shape[-1])])


def _prepare_params(norm_mix_g, w_in, s5_a_re, s5_a_im, s5_log_dt, s5_b_re, s5_b_im, s5_c_re, s5_c_im,
                    s5_d, s5_w_glu, s5_b_glu, mla_q_norm_g, mla_w_qb, mla_kv_norm_g, mla_w_uk, mla_w_uv,
                    gla_w_gate, gla_b_gate, gla_norm_g, w_out, norm_ffn_g, w_ffn_in, ffn_conv_w, ffn_conv_b,
                    w_ffn_out):
    row = lambda v: v[:, None, :]
    o = 0
    cols = {}
    for name, width in (("u", 256), ("cq", 256), ("ckv", 128), ("kpe", 32), ("gq", 128), ("gk", 128),
                        ("gv", 256), ("glow", 16), ("gr", 256)):
        cols[name] = w_in[:, :, o:o + width]
        o += width
    in_w = jnp.concatenate(
        [cols["u"], cols["cq"], cols["ckv"], _pad_cols(cols["kpe"], LANE),
         _pad_cols(_rot_cols(cols["kpe"]), LANE), cols["gq"], cols["gk"], cols["gv"],
         _pad_cols(cols["glow"], LANE), cols["gr"]], axis=2).astype(BF16)

    dt = jnp.exp(s5_log_dt)[:, :, None]
    mag = jnp.exp(s5_a_re * dt)
    ab_re, ab_im = mag * jnp.cos(s5_a_im * dt), mag * jnp.sin(s5_a_im * dt)
    den = s5_a_re * s5_a_re + s5_a_im * s5_a_im
    k_re = ((ab_re - 1.0) * s5_a_re + ab_im * s5_a_im) / den
    k_im = (ab_im * s5_a_re - (ab_re - 1.0) * s5_a_im) / den
    bb_re = k_re[..., None] * s5_b_re - k_im[..., None] * s5_b_im
    bb_im = k_re[..., None] * s5_b_im + k_im[..., None] * s5_b_re
    eye = jnp.eye(S5_G, dtype=F32)
    bd_in = lambda m: jnp.einsum("lgpc,gh->lgchp", m, eye).reshape(DEPTH, S5_W, S5_N)
    bd_out = lambda m: jnp.einsum("lgcp,gh->lgphc", m, eye).reshape(DEPTH, S5_N, S5_W)

    wqb = mla_w_qb.reshape(DEPTH, MLA_QL, MLA_H, MLA_NOPE + MLA_ROPE)
    w_nope = wqb[..., :MLA_NOPE].reshape(DEPTH, MLA_QL, MLA_H * MLA_NOPE)
    w_rope = wqb[..., MLA_NOPE:]
    heads = lambda m: m.reshape(DEPTH, MLA_QL, MLA_H * MLA_ROPE)
    eye_h = jnp.eye(MLA_H, dtype=F32)
    wuk = jnp.einsum("dlhn,hk->dhnkl", mla_w_uk, eye_h).reshape(DEPTH, MLA_H * MLA_NOPE, MLA_H * MLA_L)
    wuv = jnp.einsum("dlhv,hk->dhlkv", mla_w_uv, eye_h).astype(BF16)

    seg = (jnp.arange(GLA_W)[:, None] // GLA_DV == jnp.arange(GLA_W)[None, :] // GLA_DV)
    tri = jnp.arange(GLA_CHUNK)[None, :] <= jnp.arange(GLA_CHUNK)[:, None]
    return dict(
        in_g=row(norm_mix_g), in_w=in_w,
        s5_a=jnp.stack([ab_re.reshape(DEPTH, S5_N), ab_im.reshape(DEPTH, S5_N)], axis=1),
        s5_bm=jnp.concatenate([bd_in(bb_re), bd_in(bb_im)], axis=2).astype(BF16),
        s5_cm=jnp.concatenate([bd_out(s5_c_re), -bd_out(s5_c_im)], axis=1).astype(BF16),
        s5_d=row(s5_d), s5_wglu=s5_w_glu.astype(BF16), s5_bglu=row(s5_b_glu),
        mla_gq=row(mla_q_norm_g), mla_gkv=row(mla_kv_norm_g),
        mla_wq=jnp.concatenate([w_nope, heads(w_rope), heads(_rot_cols(w_rope))], axis=2).astype(BF16),
        mla_wuk=wuk.astype(BF16),
        mla_wuv=wuv.reshape(DEPTH, MLA_H, MLA_L, MLA_W),
        mla_wuv_bd=wuv.reshape(DEPTH, MLA_H * MLA_L, MLA_W),
        gla_wgate=jnp.pad(gla_w_gate, ((0, 0), (0, LANE - GLA_R), (0, 0))).astype(BF16),
        gla_bgate=row(gla_b_gate), gla_g=row(gla_norm_g),
        gla_seg=seg.astype(F32) / GLA_DV, gla_tri=tri.astype(F32),
        wout=w_out.astype(BF16), ffn_g=row(norm_ffn_g), ffn_win=w_ffn_in.astype(BF16),
        ffn_cw=ffn_conv_w, ffn_cb=row(ffn_conv_b), ffn_wo=w_ffn_out.astype(BF16))


def _rope_tables(pos):
    half = MLA_ROPE // 2
    inv = ROPE_THETA ** (-jnp.arange(half, dtype=F32) / half)
    ang = pos.astype(F32)[:, None] * inv[None, :]
    reps = 2 * LANE // MLA_ROPE
    return jnp.tile(jnp.cos(ang), (1, reps)), jnp.tile(jnp.sin(ang), (1, reps))


def _s5_state_out(hfin):
    shape = hfin.shape[:2] + (S5_G, S5_P)
    return jnp.stack([hfin[..., :S5_N].reshape(shape), hfin[..., S5_N:].reshape(shape)], axis=-1)


def kernel(x_prompt, x_sample, cache_mla_ckv, cache_mla_krope, page_table, state_s5, state_gla, state_ffn_conv, norm_mix_g, w_in, s5_a_re, s5_a_im, s5_log_dt, s5_b_re, s5_b_im, s5_c_re, s5_c_im, s5_d, s5_w_glu, s5_b_glu, mla_q_norm_g, mla_w_qb, mla_kv_norm_g, mla_w_uk, mla_w_uv, gla_w_gate, gla_b_gate, gla_norm_g, w_out, norm_ffn_g, w_ffn_in, ffn_conv_w, ffn_conv_b, w_ffn_out, norm_final_g):
    bp, sp = x_prompt.shape[:2]
    bs = x_sample.shape[0]
    n_pages = page_table.shape[1]
    past_len = n_pages * cache_mla_ckv.shape[2]
    kw = GLA_H * GLA_DK

    prm = _prepare_params(norm_mix_g, w_in, s5_a_re, s5_a_im, s5_log_dt, s5_b_re, s5_b_im, s5_c_re, s5_c_im,
                          s5_d, s5_w_glu, s5_b_glu, mla_q_norm_g, mla_w_qb, mla_kv_norm_g, mla_w_uk,
                          mla_w_uv, gla_w_gate, gla_b_gate, gla_norm_g, w_out, norm_ffn_g, w_ffn_in,
                          ffn_conv_w, ffn_conv_b, w_ffn_out)
    cos_p, sin_p = _rope_tables(jnp.arange(sp))
    cos_s, sin_s = _rope_tables(jnp.full((bs,), past_len))
    gfin = norm_final_g[None]
    krope_t = jnp.swapaxes(cache_mla_krope, 2, 3)
    h0_all = jnp.concatenate([state_s5[..., 0].reshape(DEPTH, bs, S5_N),
                              state_s5[..., 1].reshape(DEPTH, bs, S5_N)], axis=2)
    gla_all = state_gla.reshape(DEPTH, bs, kw, GLA_DV)
    conv_all = state_ffn_conv.reshape(DEPTH, bs, 2 * D_FF)
    h0_zero = jnp.zeros((bp, 2 * S5_N), F32)
    xp = x_prompt.reshape(bp * sp, D_MODEL)
    xd = x_sample.reshape(bs, D_MODEL)
    outs = [[] for _ in range(10)]

    for l in range(DEPTH):
        final_g = gfin if l == DEPTH - 1 else None

        u, pm, pg = _in_proj(xp, prm, l, tm=512)
        o_s5, hfin = _s5(u, h0_zero, prm, l, r_rows=bp, t_total=sp, t_steps=128)
        qcat, kcat, ckv, kpe = _mla_prep(pm, cos_p, sin_p, prm, l, n_seq=bp, seq_len=sp, tm=512,
                                         q_dtype=BF16)
        o_mla = _attn_prompt(qcat, kcat, prm, l, n_seq=bp, seq_len=sp)
        o_gla, st = _gla_prompt(pg, prm, l, n_seq=bp, seq_len=sp)
        xp, cbuf = _ffn_prompt(xp, o_s5, o_mla, o_gla, prm, l, n_seq=bp, seq_len=sp, final_g=final_g)
        for k, v in zip(range(5), (ckv, kpe, hfin, st, cbuf)):
            outs[k].append(v)

        u, pm, pg = _in_proj(xd, prm, l, tm=bs)
        o_s5, hfin = _s5(u, h0_all[l], prm, l, r_rows=bs, t_total=1, t_steps=1)
        qcat, kcat, ckv, kpe = _mla_prep(pm, cos_s, sin_s, prm, l, n_seq=1, seq_len=bs, tm=bs, q_dtype=F32)
        o_lat = _attn_decode(page_table, qcat.transpose(1, 0, 2), kcat[:, None, :], cache_mla_ckv,
                             krope_t, layer=l)
        o_gla, st = _gla_step(pg, gla_all, prm, l)
        xd, gate = _ffn_step(xd, o_s5, o_lat.reshape(bs, MLA_H * MLA_L), o_gla, conv_all, prm, l,
                             final_g=final_g)
        for k, v in zip(range(5, 10), (ckv, kpe, hfin, st, gate)):
            outs[k].append(v)

    outs = [jnp.stack(o) for o in outs]
    st5 = outs[3].reshape(DEPTH, bp, GLA_H, GLA_DV, GLA_H, GLA_DK)
    gla_p = jnp.stack([st5[:, :, h, :, h, :] for h in range(GLA_H)], axis=2).swapaxes(3, 4)
    conv_s = jnp.stack([state_ffn_conv[:, :, 1], outs[9]], axis=2)
    return (xp.reshape(bp, sp, D_MODEL), xd.reshape(bs, 1, D_MODEL),
            outs[0].reshape(DEPTH, bp, sp // PAGE, PAGE, MLA_L),
            outs[1].reshape(DEPTH, bp, sp // PAGE, PAGE, MLA_ROPE),
            _s5_state_out(outs[2]), gla_p, outs[4],
            outs[5][:, :, None, :], outs[6][:, :, None, :], _s5_state_out(outs[7]),
            outs[8].reshape(DEPTH, bs, GLA_H, GLA_DK, GLA_DV), conv_s)
```

```python
import functools
import math

import jax
import jax.numpy as jnp
from jax import lax
from jax.experimental import pallas as pl
from jax.experimental.pallas import tpu as pltpu

F32 = jnp.float32
BF16 = jnp.bfloat16

D_MODEL = 1024
DEPTH = 4
PAGE = 128
S5_W = 256
S5_G = 16
S5_C = 16
S5_P = 64
S5_N = S5_G * S5_P
MLA_H = 8
MLA_NOPE = 64
MLA_ROPE = 32
MLA_V = 64
MLA_W = MLA_H * MLA_V
MLA_QL = 256
MLA_L = 128
MLA_SCALE = (MLA_NOPE + MLA_ROPE) ** -0.5
Q_SCALE = MLA_SCALE * math.log2(math.e)
ROPE_THETA = 10000.0
GLA_H = 4
GLA_W = 256
GLA_DV = 64
GLA_DK = 32
GLA_R = 16
GLA_NORM = 16.0
GLA_CHUNK = 64
D_FF = 2816
RMS_EPS = 1e-6
NEG_INF = -1e30

LANE = 128
PM_W = 640
PG_W = 896
W_EXT = S5_W + PM_W + PG_W
QA_W = MLA_H * MLA_NOPE + 2 * MLA_H * MLA_ROPE
KC_W = 2 * LANE

FF_CHUNK = 256
FF_NC = D_FF // FF_CHUNK
VMEM_LIMIT = 56 * 1024 * 1024


def _cparams(sem, vmem=VMEM_LIMIT):
    return pltpu.CompilerParams(dimension_semantics=sem, vmem_limit_bytes=vmem)


def _rms(x, g):
    ms = jnp.mean(x * x, axis=-1, keepdims=True)
    return x * lax.rsqrt(ms + RMS_EPS) * g


def _const_spec(shape):
    nd = len(shape)
    return pl.BlockSpec(shape, lambda *_: (0,) * nd)


def _layer_spec(layer, shape, resident=False):
    nd = len(shape)
    return pl.BlockSpec((None,) + tuple(shape), lambda *_: (layer,) + (0,) * nd,
                        pipeline_mode=pl.Buffered(1) if resident else None)


def _in_proj_kernel(x_ref, g_ref, w_ref, u_ref, pm_ref, pg_ref):
    h = _rms(x_ref[...], g_ref[...]).astype(BF16)
    u_ref[...] = jnp.dot(h, w_ref[:, :S5_W], preferred_element_type=F32)
    pm_ref[...] = jnp.dot(h, w_ref[:, S5_W:S5_W + PM_W], preferred_element_type=F32)
    pg_ref[...] = jnp.dot(h, w_ref[:, S5_W + PM_W:], preferred_element_type=F32)


def _in_proj(x2d, prm, layer, *, tm):
    n = x2d.shape[0]
    row = lambda i: (i, 0)
    return pl.pallas_call(
        _in_proj_kernel,
        grid=(n // tm,),
        in_specs=[pl.BlockSpec((tm, D_MODEL), row), _layer_spec(layer, (1, D_MODEL)),
                  _layer_spec(layer, (D_MODEL, W_EXT))],
        out_specs=[pl.BlockSpec((tm, S5_W), row), pl.BlockSpec((tm, PM_W), row),
                   pl.BlockSpec((tm, PG_W), row)],
        out_shape=[jax.ShapeDtypeStruct((n, S5_W), F32), jax.ShapeDtypeStruct((n, PM_W), F32),
                   jax.ShapeDtypeStruct((n, PG_W), F32)],
        compiler_params=_cparams(("parallel",)),
        name="in_proj",
    )(x2d, prm["in_g"], prm["in_w"])


def _s5_kernel(u_ref, h0_ref, a_ref, bm_ref, cm_ref, d_ref, wg_ref, bg_ref,
               o_ref, hfin_ref, h_sc, bu_sc, *io_sc, r_rows, t_steps):
    @pl.when(pl.program_id(0) == 0)
    def _():
        h_sc[...] = h0_ref[...]

    n_slab = S5_W // LANE
    if t_steps == 1:
        u = u_ref[...]
    else:
        u_sc, y_sc = io_sc
        for r in range(r_rows):
            for k in range(n_slab):
                u_sc[k, pl.ds(r, t_steps, stride=r_rows), :] = u_ref[r, :, k * LANE:(k + 1) * LANE]
        u = jnp.concatenate([u_sc[k] for k in range(n_slab)], axis=1)
    rows = r_rows * t_steps
    u16 = u.astype(BF16)
    ar = jnp.broadcast_to(a_ref[0:1, :], (r_rows, S5_N))
    ai = jnp.broadcast_to(a_ref[1:2, :], (r_rows, S5_N))

    n_half = 2 if t_steps > 1 else 1
    th = t_steps // n_half

    def pieces(h):
        lo, mid, hi = h * th * r_rows, (2 * h + 1) * th * r_rows // 2, (h + 1) * th * r_rows
        return [slice(lo, mid), slice(mid, hi)]

    def project_in(h):
        for p in pieces(h):
            bu_sc[p, :] = jnp.dot(u16[p], bm_ref[...], preferred_element_type=F32)

    def scan(h, hr, hi):
        for t in range(h * th, (h + 1) * th):
            r = slice(t * r_rows, (t + 1) * r_rows)
            hr, hi = (ar * hr - ai * hi + bu_sc[r, :S5_N], ar * hi + ai * hr + bu_sc[r, S5_N:])
            bu_sc[r, :S5_N] = hr
            bu_sc[r, S5_N:] = hi
        return hr, hi

    def project_out(h):
        ps = pieces(h)
        y = [jnp.dot(bu_sc[p, :].astype(BF16), cm_ref[...], preferred_element_type=F32) for p in ps]
        y = [jax.nn.gelu(v + d_ref[...] * u[p]) for v, p in zip(y, ps)]
        z = [jnp.dot(v.astype(BF16), wg_ref[...], preferred_element_type=F32) + bg_ref[...] for v in y]
        return [v * jax.nn.sigmoid(g) for v, g in zip(y, z)]

    for h in range(n_half):
        project_in(h)
    hr, hi = h_sc[:, :S5_N], h_sc[:, S5_N:]
    out = []
    for h in range(n_half):
        hr, hi = scan(h, hr, hi)
        out += project_out(h)
    out = jnp.concatenate(out, axis=0)
    h_sc[:, :S5_N] = hr
    h_sc[:, S5_N:] = hi
    hfin_ref[:, :S5_N] = hr
    hfin_ref[:, S5_N:] = hi
    if t_steps == 1:
        o_ref[...] = out
    else:
        for k in range(n_slab):
            y_sc[k] = out[:, k * LANE:(k + 1) * LANE]
        for r in range(r_rows):
            for k in range(n_slab):
                o_ref[r, :, k * LANE:(k + 1) * LANE] = y_sc[k, pl.ds(r, t_steps, stride=r_rows), :]


def _s5(u, h0, prm, layer, *, r_rows, t_total, t_steps):
    rows = r_rows * t_steps
    kern = functools.partial(_s5_kernel, r_rows=r_rows, t_steps=t_steps)
    if t_steps == 1:
        io_spec = pl.BlockSpec((r_rows, S5_W), lambda i: (0, 0))
        io_shape = (r_rows, S5_W)
        io_scratch = []
    else:
        io_spec = pl.BlockSpec((r_rows, t_steps, S5_W), lambda i: (0, i, 0))
        io_shape = (r_rows, t_total, S5_W)
        io_scratch = [pltpu.VMEM((S5_W // LANE, rows, LANE), F32)] * 2
    o, hfin = pl.pallas_call(
        kern,
        grid=(t_total // t_steps,),
        in_specs=[io_spec, _const_spec((r_rows, 2 * S5_N)), _layer_spec(layer, (2, S5_N)),
                  _layer_spec(layer, (S5_W, 2 * S5_N)), _layer_spec(layer, (2 * S5_N, S5_W)),
                  _layer_spec(layer, (1, S5_W)), _layer_spec(layer, (S5_W, S5_W)),
                  _layer_spec(layer, (1, S5_W))],
        out_specs=[io_spec, _const_spec((r_rows, 2 * S5_N))],
        out_shape=[jax.ShapeDtypeStruct(io_shape, F32),
                   jax.ShapeDtypeStruct((r_rows, 2 * S5_N), F32)],
        scratch_shapes=[pltpu.VMEM((r_rows, 2 * S5_N), F32),
                        pltpu.VMEM((rows, 2 * S5_N), F32)] + io_scratch,
        compiler_params=_cparams(("arbitrary",)),
        name="s5",
    )(u.reshape(io_shape), h0, prm["s5_a"], prm["s5_bm"], prm["s5_cm"], prm["s5_d"], prm["s5_wglu"],
      prm["s5_bglu"])
    return o.reshape(r_rows * t_total, S5_W), hfin


def _mla_prep_kernel(pm_ref, cos_ref, sin_ref, gq_ref, wq_ref, wuk_ref, gkv_ref,
                     qcat_ref, kcat_ref, ckv_ref, kpe_ref):
    pm = pm_ref[...]
    cos4 = cos_ref[...]
    sin4 = sin_ref[...]
    first = lax.broadcasted_iota(jnp.int32, cos4.shape, 1) < MLA_ROPE
    qn = _rms(pm[:, :MLA_QL], gq_ref[...]).astype(BF16)
    qa = jnp.dot(qn, wq_ref[...], preferred_element_type=F32)
    n_nope = MLA_H * MLA_NOPE
    n_rope = MLA_H * MLA_ROPE
    qlat = jnp.dot(qa[:, :n_nope].astype(BF16), wuk_ref[...], preferred_element_type=F32)
    per_blk = LANE // MLA_ROPE
    roped = [qa[:, n_nope + b * LANE:n_nope + (b + 1) * LANE] * cos4
             + qa[:, n_nope + n_rope + b * LANE:n_nope + n_rope + (b + 1) * LANE] * sin4
             for b in range(MLA_H // per_blk)]
    for h in range(MLA_H):
        v = roped[h // per_blk]
        shift = (h % per_blk) * MLA_ROPE
        if shift:
            v = pltpu.roll(v, LANE - shift, 1)
        qpe = jnp.where(first, v, 0.0)
        qcat_ref[h, :, :LANE] = (qlat[:, h * LANE:(h + 1) * LANE] * Q_SCALE).astype(qcat_ref.dtype)
        qcat_ref[h, :, LANE:] = (qpe * Q_SCALE).astype(qcat_ref.dtype)
    ckv = _rms(pm[:, MLA_QL:MLA_QL + MLA_L], gkv_ref[...])
    kpe = jnp.where(first, pm[:, MLA_QL + MLA_L:MLA_QL + 2 * MLA_L] * cos4
                    + pm[:, MLA_QL + 2 * MLA_L:] * sin4, 0.0)
    ckv_ref[...] = ckv
    kpe_ref[...] = kpe[:, :MLA_ROPE]
    kcat_ref[:, :LANE] = ckv.astype(kcat_ref.dtype)
    kcat_ref[:, LANE:] = kpe.astype(kcat_ref.dtype)


def _mla_prep(pm, cos_t, sin_t, prm, layer, *, n_seq, seq_len, tm, q_dtype):
    n = n_seq * seq_len
    nj = seq_len // tm
    row = lambda b, j: (b * nj + j, 0)
    pos = lambda b, j: (j, 0)
    return pl.pallas_call(
        _mla_prep_kernel,
        grid=(n_seq, nj),
        in_specs=[pl.BlockSpec((tm, PM_W), row),
                  pl.BlockSpec((tm, LANE), pos), pl.BlockSpec((tm, LANE), pos),
                  _layer_spec(layer, (1, MLA_QL)), _layer_spec(layer, (MLA_QL, QA_W)),
                  _layer_spec(layer, (MLA_H * MLA_NOPE, MLA_H * LANE)), _layer_spec(layer, (1, MLA_L))],
        out_specs=[pl.BlockSpec((MLA_H, tm, KC_W), lambda b, j: (0, b * nj + j, 0)),
                   pl.BlockSpec((tm, KC_W), row),
                   pl.BlockSpec((tm, MLA_L), row), pl.BlockSpec((tm, MLA_ROPE), row)],
        out_shape=[jax.ShapeDtypeStruct((MLA_H, n, KC_W), q_dtype),
                   jax.ShapeDtypeStruct((n, KC_W), q_dtype),
                   jax.ShapeDtypeStruct((n, MLA_L), F32),
                   jax.ShapeDtypeStruct((n, MLA_ROPE), F32)],
        compiler_params=_cparams(("parallel", "parallel")),
        name="mla_prep",
    )(pm, cos_t, sin_t, prm["mla_gq"], prm["mla_wq"], prm["mla_wuk"], prm["mla_gkv"])


ATT_TQ = 256
ATT_TK = 512
ATT_NG = 4


def _attn_kernel(q_ref, k_ref, wuv_ref, o_ref, m_sc, acc_sc, *, seq_len):
    rows = MLA_H * ATT_TQ
    hg = MLA_H // ATT_NG
    grows = hg * ATT_TQ
    nt = (((1,), (1,)), ((), ()))
    ones = jnp.ones((ATT_TK, LANE), BF16)
    pairs = [(i, j) for i in range(seq_len // ATT_TQ) for j in range(i * ATT_TQ // ATT_TK + 1)]

    def keys(j):
        return k_ref[j * ATT_TK:(j + 1) * ATT_TK, :]

    def scores(i, j):
        kt = keys(j)
        return [lax.dot_general(q_ref[g * hg:(g + 1) * hg, i * ATT_TQ:(i + 1) * ATT_TQ, :].reshape(grows, KC_W),
                                kt, nt, preferred_element_type=F32) for g in range(ATT_NG)]

    ahead = scores(*pairs[0])
    for n, (i, j) in enumerate(pairs):
        cur = ahead
        if n + 1 < len(pairs):
            ahead = scores(*pairs[n + 1])
        last = j == i * ATT_TQ // ATT_TK
        kt = keys(j)
        v_ext = jnp.concatenate([kt[:, :MLA_L], ones], axis=1)
        if last:
            qpos = i * ATT_TQ + lax.broadcasted_iota(jnp.int32, (ATT_TQ, ATT_TK), 0)
            kpos = j * ATT_TK + lax.broadcasted_iota(jnp.int32, (ATT_TQ, ATT_TK), 1)
            visible = (kpos <= qpos)[None]
        for g in range(ATT_NG):
            r = slice(g * grows, (g + 1) * grows)
            s = cur[g]
            if last:
                s = jnp.where(visible, s.reshape(hg, ATT_TQ, ATT_TK), NEG_INF).reshape(grows, ATT_TK)
            if j == 0:
                m_old = jnp.full((grows, LANE), NEG_INF, F32)
            else:
                m_old = m_sc[r, :]
            m_new = jnp.maximum(m_old, jnp.max(s, axis=-1, keepdims=True))
            p = jnp.exp2(s - jnp.tile(m_new, (1, ATT_TK // LANE)))
            pv = jnp.dot(p.astype(kt.dtype), v_ext, preferred_element_type=F32)
            if j == 0:
                acc_sc[r, :] = pv
            else:
                acc_sc[r, :] = jnp.tile(jnp.exp2(m_old - m_new), (1, 2)) * acc_sc[r, :] + pv
            m_sc[r, :] = m_new
        if last:
            acc = acc_sc[...]
            o_lat = (acc[:, :MLA_L] / acc[:, MLA_L:]).astype(wuv_ref.dtype)
            out = jnp.dot(o_lat[:ATT_TQ], wuv_ref[0], preferred_element_type=F32)
            for h in range(1, MLA_H):
                out += jnp.dot(o_lat[h * ATT_TQ:(h + 1) * ATT_TQ], wuv_ref[h], preferred_element_type=F32)
            o_ref[i * ATT_TQ:(i + 1) * ATT_TQ, :] = out


def _attn_prompt(qcat, kcat, prm, layer, *, n_seq, seq_len):
    rows = MLA_H * ATT_TQ
    return pl.pallas_call(
        functools.partial(_attn_kernel, seq_len=seq_len),
        grid=(n_seq,),
        in_specs=[pl.BlockSpec((MLA_H, seq_len, KC_W), lambda b: (0, b, 0)),
                  pl.BlockSpec((seq_len, KC_W), lambda b: (b, 0)),
                  _layer_spec(layer, (MLA_H, MLA_L, MLA_W))],
        out_specs=pl.BlockSpec((seq_len, MLA_W), lambda b: (b, 0)),
        out_shape=jax.ShapeDtypeStruct((n_seq * seq_len, MLA_W), F32),
        scratch_shapes=[pltpu.VMEM((rows, LANE), F32), pltpu.VMEM((rows, 2 * MLA_L), F32)],
        compiler_params=_cparams(("parallel",)),
        name="mla_attn",
    )(qcat, kcat, prm["mla_wuv"])


DEC_G = 64
DEC_NS = 8
DEC_SEQ = 2
DEC_AHEAD = 2


def _page_copies(pt_ref, ckv_hbm, kpe_hbm, ckv_buf, kpe_buf, sems, layer, b, c, slot):
    copies = []
    for g in range(DEC_G):
        page = pt_ref[b, c * DEC_G + g]
        dst = pl.ds(g * PAGE, PAGE)
        copies.append(pltpu.make_async_copy(ckv_hbm.at[layer, page], ckv_buf.at[slot, dst, :],
                                            sems.at[0, slot]))
        copies.append(pltpu.make_async_copy(kpe_hbm.at[layer, page], kpe_buf.at[slot, :, dst],
                                            sems.at[1, slot]))
    return copies


def _start_pages(copies):
    for i, cp in enumerate(copies):
        cp.start(priority=(i // 2) % 2)


def _decode_kernel(pt_ref, q_ref, knew_ref, ckv_hbm, kpe_hbm, o_ref, ckv_buf, kpe_buf, sems,
                   *, layer, n_chunks):
    g = pl.program_id(0)
    n_steps = pl.num_programs(0)
    n_pieces = DEC_SEQ * n_chunks
    page_copies = functools.partial(_page_copies, pt_ref, ckv_hbm, kpe_hbm, ckv_buf, kpe_buf, sems, layer)

    def copies(step, k):
        return page_copies(step * DEC_SEQ + k // n_chunks, k % n_chunks, k)

    @pl.when(g == 0)
    def _():
        for k in range(DEC_AHEAD):
            _start_pages(copies(g, k))

    nt = (((1,), (1,)), ((), ()))
    sub = DEC_G * PAGE // DEC_NS
    blocks = [pl.ds(i * sub, sub) for i in range(DEC_NS)]
    for k in range(n_pieces):
        seq, c = k // n_chunks, k % n_chunks
        if c == 0:
            q = q_ref[seq]
            q_lat = q[:, :MLA_L].astype(BF16)
            q_pe = q[:, MLA_L:MLA_L + MLA_ROPE].astype(BF16)
            m = jnp.full((MLA_H, 1), NEG_INF, F32)
            l = jnp.zeros((MLA_H, 1), F32)
            acc = jnp.zeros((MLA_H, MLA_L), F32)
        for cp in copies(g, k):
            cp.wait()
        if k + DEC_AHEAD < n_pieces:
            _start_pages(copies(g, k + DEC_AHEAD))
        else:
            _start_pages(copies(lax.rem(g + 1, n_steps), k + DEC_AHEAD - n_pieces))
        ck = [ckv_buf[k, blk, :].astype(BF16) for blk in blocks]
        s_lat = [lax.dot_general(q_lat, kk, nt, preferred_element_type=F32) for kk in ck]
        s_pe = [jnp.dot(q_pe, kpe_buf[k, :, blk].astype(BF16), preferred_element_type=F32)
                for blk in blocks]
        s = [x + y for x, y in zip(s_lat, s_pe)]
        m_new = m
        for x in s:
            m_new = jnp.maximum(m_new, jnp.max(x, axis=-1, keepdims=True))
        alpha = jnp.exp2(m - m_new)
        p = [jnp.exp2(x - m_new) for x in s]
        pv = [jnp.dot(x.astype(BF16), kk, preferred_element_type=F32) for x, kk in zip(p, ck)]
        l = alpha * l + sum(jnp.sum(x, axis=-1, keepdims=True) for x in p)
        acc = alpha * acc + sum(pv)
        m = m_new
        if c == n_chunks - 1:
            kn = knew_ref[seq]
            s_new = jnp.sum(q * kn, axis=-1, keepdims=True)
            m_fin = jnp.maximum(m, s_new)
            a_fin = jnp.exp2(m - m_fin)
            p_new = jnp.exp2(s_new - m_fin)
            o_ref[seq] = (a_fin * acc + p_new * kn[:, :MLA_L]) / (a_fin * l + p_new)

    @pl.when(g == n_steps - 1)
    def _():
        for k in range(DEC_AHEAD):
            for cp in copies(0, k):
                cp.wait()


def _attn_decode(page_table, q_dec, knew, cache_ckv, cache_kpe, *, layer):
    n_seq, n_pages = page_table.shape
    n_chunks = n_pages // DEC_G
    n_pieces = DEC_SEQ * n_chunks
    assert n_seq % DEC_SEQ == 0 and DEC_AHEAD < n_pieces
    keys = DEC_G * PAGE
    kern = functools.partial(_decode_kernel, layer=layer, n_chunks=n_chunks)
    grid_spec = pltpu.PrefetchScalarGridSpec(
        num_scalar_prefetch=1,
        grid=(n_seq // DEC_SEQ,),
        in_specs=[pl.BlockSpec((DEC_SEQ, MLA_H, KC_W), lambda g, pt: (g, 0, 0)),
                  pl.BlockSpec((DEC_SEQ, 1, KC_W), lambda g, pt: (g, 0, 0)),
                  pl.BlockSpec(memory_space=pl.ANY), pl.BlockSpec(memory_space=pl.ANY)],
        out_specs=pl.BlockSpec((DEC_SEQ, MLA_H, MLA_L), lambda g, pt: (g, 0, 0)),
        scratch_shapes=[pltpu.VMEM((n_pieces, keys, MLA_L), F32),
                        pltpu.VMEM((n_pieces, MLA_ROPE, keys), F32),
                        pltpu.SemaphoreType.DMA((2, n_pieces))])
    return pl.pallas_call(
        kern,
        grid_spec=grid_spec,
        out_shape=jax.ShapeDtypeStruct((n_seq, MLA_H, MLA_L), F32),
        compiler_params=_cparams(("arbitrary",)),
        name="mla_decode",
    )(page_table, q_dec, knew, cache_ckv, cache_kpe)


GLA_TG = 512


def _head_norm_gate(o, gr, g_ref, seg_ref):
    ms = jnp.dot(o * o, seg_ref[...], preferred_element_type=F32, precision=lax.Precision.HIGHEST)
    return o * lax.rsqrt(ms + RMS_EPS) * g_ref[...] * (gr * jax.nn.sigmoid(gr))


def _gla_gates(glow, wgate_ref, bgate_ref):
    z = jnp.dot(glow.astype(BF16), wgate_ref[...], preferred_element_type=F32) + bgate_ref[...]
    return jax.nn.log_sigmoid(z) / GLA_NORM


def _gla_kernel(pg_ref, wgate_ref, bgate_ref, g_ref, seg_ref, tri_ref, o_ref, st_ref, st_sc, o_sc):
    @pl.when(pl.program_id(1) == 0)
    def _():
        st_sc[...] = jnp.zeros_like(st_sc)

    kw = GLA_H * GLA_DK
    c = GLA_CHUNK
    pg = pg_ref[...]
    logg = _gla_gates(pg[:, 2 * kw + GLA_W:2 * kw + GLA_W + LANE], wgate_ref, bgate_ref)
    hm_q = (lax.broadcasted_iota(jnp.int32, (GLA_H, c, kw), 2) // GLA_DK
            == lax.broadcasted_iota(jnp.int32, (GLA_H, c, kw), 0))
    hm_o = (lax.broadcasted_iota(jnp.int32, (GLA_H, c, GLA_W), 2) // GLA_DV
            == lax.broadcasted_iota(jnp.int32, (GLA_H, c, GLA_W), 0))
    hm_s = (lax.broadcasted_iota(jnp.int32, (GLA_W, kw), 0) // GLA_DV
            == lax.broadcasted_iota(jnp.int32, (GLA_W, kw), 1) // GLA_DK)
    causal = (lax.broadcasted_iota(jnp.int32, (c, c), 1) <= lax.broadcasted_iota(jnp.int32, (c, c), 0))
    nt = (((1,), (1,)), ((), ()))
    chunks = [slice(ci * c, (ci + 1) * c) for ci in range(GLA_TG // c)]
    bcum = [jnp.dot(tri_ref[...], logg[r], preferred_element_type=F32, precision=lax.Precision.HIGHEST)
            for r in chunks]
    v16 = [pg[r, 2 * kw:2 * kw + GLA_W].astype(BF16) for r in chunks]
    qe, att, ut, decay = [], [], [], []
    for r, b in zip(chunks, bcum):
        b_last = b[c - 1:c, :]
        k = pg[r, kw:2 * kw]
        qe_c = pg[r, :kw] * (GLA_DK ** -0.5) * jnp.exp(b)
        ke = (k * jnp.exp(-b)).astype(BF16)
        kd = (k * jnp.exp(b_last - b)).astype(BF16)
        qe.append(qe_c.astype(BF16))
        decay.append(jnp.exp(b_last))
        qs = jnp.where(hm_q, qe_c[None], 0.0).reshape(GLA_H * c, kw).astype(BF16)
        att.append(lax.dot_general(qs, ke, nt, preferred_element_type=F32))
        ut.append((pg[r, 2 * kw:2 * kw + GLA_W].T.astype(BF16), kd))
    ut = [jnp.dot(vt, kd, preferred_element_type=F32) for vt, kd in ut]
    oi = [jnp.dot(jnp.where(causal[None], a.reshape(GLA_H, c, c), 0.0).reshape(GLA_H * c, c).astype(BF16),
                  v, preferred_element_type=F32) for a, v in zip(att, v16)]
    st = st_sc[...]
    for r, q, o, u, d in zip(chunks, qe, oi, ut, decay):
        o_intra = jnp.sum(jnp.where(hm_o, o.reshape(GLA_H, c, GLA_W), 0.0), axis=0)
        o_inter = lax.dot_general(q, st.astype(BF16), nt, preferred_element_type=F32)
        o_sc[r, :] = o_intra + o_inter
        st = st * d + jnp.where(hm_s, u, 0.0)
    st_sc[...] = st
    st_ref[0] = st
    gr = pg[:, 2 * kw + GLA_W + LANE:]
    o_ref[...] = _head_norm_gate(o_sc[...], gr, g_ref, seg_ref)


def _gla_prompt(pg, prm, layer, *, n_seq, seq_len):
    nj = seq_len // GLA_TG
    kw = GLA_H * GLA_DK
    row = lambda b, j: (b * nj + j, 0)
    return pl.pallas_call(
        _gla_kernel,
        grid=(n_seq, nj),
        in_specs=[pl.BlockSpec((GLA_TG, PG_W), row),
                  _layer_spec(layer, (LANE, kw)), _layer_spec(layer, (1, kw)),
                  _layer_spec(layer, (1, GLA_W)),
                  _const_spec((GLA_W, GLA_W)), _const_spec((GLA_CHUNK, GLA_CHUNK))],
        out_specs=[pl.BlockSpec((GLA_TG, GLA_W), row),
                   pl.BlockSpec((1, GLA_W, kw), lambda b, j: (b, 0, 0))],
        out_shape=[jax.ShapeDtypeStruct((n_seq * seq_len, GLA_W), F32),
                   jax.ShapeDtypeStruct((n_seq, GLA_W, kw), F32)],
        scratch_shapes=[pltpu.VMEM((GLA_W, kw), F32), pltpu.VMEM((GLA_TG, GLA_W), F32)],
        compiler_params=_cparams(("parallel", "arbitrary")),
        name="gla",
    )(pg, prm["gla_wgate"], prm["gla_bgate"], prm["gla_g"], prm["gla_seg"], prm["gla_tri"])


def _gla_step_kernel(pg_ref, s_ref, wgate_ref, bgate_ref, g_ref, seg_ref, o_ref, snew_ref):
    kw = GLA_H * GLA_DK
    pg = pg_ref[...]
    n = pg.shape[0]
    q = pg[:, :kw] * (GLA_DK ** -0.5)
    k = pg[:, kw:2 * kw]
    v = pg[:, 2 * kw:2 * kw + GLA_W]
    decay = jnp.exp(_gla_gates(pg[:, 2 * kw + GLA_W:2 * kw + GLA_W + LANE], wgate_ref, bgate_ref))
    q_t, k_t, d_t = q.T, k.T, decay.T
    o_rows = []
    for b in range(n):
        vexp = jnp.concatenate(
            [jnp.broadcast_to(v[b:b + 1, h * GLA_DV:(h + 1) * GLA_DV], (GLA_DK, GLA_DV))
             for h in range(GLA_H)], axis=0)
        s_new = d_t[:, b:b + 1] * s_ref[b] + k_t[:, b:b + 1] * vexp
        snew_ref[b] = s_new
        w = q_t[:, b:b + 1] * s_new
        o_rows.append(jnp.concatenate(
            [jnp.sum(w[h * GLA_DK:(h + 1) * GLA_DK], axis=0, keepdims=True) for h in range(GLA_H)],
            axis=1))
    o = jnp.concatenate(o_rows, axis=0)
    o_ref[...] = _head_norm_gate(o, pg[:, 2 * kw + GLA_W + LANE:], g_ref, seg_ref)


def _gla_step(pg, state, prm, layer):
    n = pg.shape[0]
    kw = GLA_H * GLA_DK
    return pl.pallas_call(
        _gla_step_kernel,
        grid=(1,),
        in_specs=[_const_spec((n, PG_W)), _layer_spec(layer, (n, kw, GLA_DV)),
                  _layer_spec(layer, (LANE, kw)), _layer_spec(layer, (1, kw)),
                  _layer_spec(layer, (1, GLA_W)), _const_spec((GLA_W, GLA_W))],
        out_specs=[_const_spec((n, GLA_W)), _const_spec((n, kw, GLA_DV))],
        out_shape=[jax.ShapeDtypeStruct((n, GLA_W), F32),
                   jax.ShapeDtypeStruct((n, kw, GLA_DV), F32)],
        compiler_params=_cparams(("arbitrary",)),
        name="gla_step",
    )(pg, state, prm["gla_wgate"], prm["gla_bgate"], prm["gla_g"], prm["gla_seg"])


FFN_TM = 512


def _mix_residual(x_ref, os5_ref, omla_ref, ogla_ref, wout_ref):
    return (x_ref[...]
            + jnp.dot(os5_ref[...].astype(BF16), wout_ref[:S5_W], preferred_element_type=F32)
            + jnp.dot(omla_ref[...].astype(BF16), wout_ref[S5_W:S5_W + MLA_W],
                      preferred_element_type=F32)
            + jnp.dot(ogla_ref[...].astype(BF16), wout_ref[S5_W + MLA_W:],
                      preferred_element_type=F32))


def _ffn_kernel(x_ref, os5_ref, omla_ref, ogla_ref, wout_ref, gf_ref, win_ref, cw_ref, cb_ref, wo_ref,
                *rest, final_norm):
    if final_norm:
        gfin_ref, y_ref, buf_ref, hn_sc, acc_sc, carry_sc = rest
    else:
        y_ref, buf_ref, hn_sc, acc_sc, carry_sc = rest
    tm = x_ref.shape[0]

    @pl.when(pl.program_id(1) == 0)
    def _():
        carry_sc[...] = jnp.zeros_like(carry_sc)

    x1 = _mix_residual(x_ref, os5_ref, omla_ref, ogla_ref, wout_ref)
    y_ref[...] = x1
    hn_sc[...] = _rms(x1, gf_ref[...]).astype(BF16)
    sub = 8
    row = lax.broadcasted_iota(jnp.int32, (sub, FF_CHUNK), 0)

    def up_proj(c):
        hn = hn_sc[...]
        return (jnp.dot(hn, win_ref[:, c * FF_CHUNK:(c + 1) * FF_CHUNK], preferred_element_type=F32),
                jnp.dot(hn, win_ref[:, D_FF + c * FF_CHUNK:D_FF + (c + 1) * FF_CHUNK],
                        preferred_element_type=F32))

    ahead = up_proj(0)
    for c in range(FF_NC):
        cols = slice(c * FF_CHUNK, (c + 1) * FF_CHUNK)
        val, gate = ahead
        if c + 1 < FF_NC:
            ahead = up_proj(c + 1)
        prev = carry_sc[:, cols]
        r1 = pltpu.roll(gate, 1, 0)
        r2 = pltpu.roll(gate, 2, 0)
        head1 = jnp.where(row == 0, prev[1:2], r1[:sub])
        head2 = jnp.where(row == 0, prev[0:1], jnp.where(row == 1, prev[1:2], r2[:sub]))
        g1 = jnp.concatenate([head1, r1[sub:]], axis=0)
        g2 = jnp.concatenate([head2, r2[sub:]], axis=0)
        conv = cb_ref[:, cols] + cw_ref[0:1, cols] * g2 + cw_ref[1:2, cols] * g1 + cw_ref[2:3, cols] * gate
        a = (jax.nn.gelu(conv) * val).astype(BF16)
        carry_sc[0:2, cols] = gate[tm - 2:tm]
        if c % 2 == 0 and c + 1 < FF_NC:
            held = a
            continue
        if c % 2 == 1:
            a = jnp.concatenate([held, a], axis=1)
            rows = slice((c - 1) * FF_CHUNK, (c + 1) * FF_CHUNK)
        else:
            rows = cols
        part = jnp.dot(a, wo_ref[rows, :], preferred_element_type=F32)
        if c <= 1:
            acc_sc[...] = part
        else:
            acc_sc[...] += part

    y = y_ref[...] + acc_sc[...]
    if final_norm:
        y = _rms(y, gfin_ref[...])
    y_ref[...] = y
    buf_ref[0] = carry_sc[0:2, :]


def _ffn_prompt(x2d, os5, omla, ogla, prm, layer, *, n_seq, seq_len, final_g=None):
    tm = FFN_TM
    nj = seq_len // tm
    row = lambda b, j: (b * nj + j, 0)
    final_norm = final_g is not None
    in_specs = [pl.BlockSpec((tm, D_MODEL), row), pl.BlockSpec((tm, S5_W), row),
                pl.BlockSpec((tm, MLA_W), row), pl.BlockSpec((tm, GLA_W), row),
                _layer_spec(layer, (D_MODEL, D_MODEL), resident=True), _layer_spec(layer, (1, D_MODEL)),
                _layer_spec(layer, (D_MODEL, 2 * D_FF), resident=True), _layer_spec(layer, (3, D_FF)),
                _layer_spec(layer, (1, D_FF)), _layer_spec(layer, (D_FF, D_MODEL), resident=True)]
    args = [x2d, os5, omla, ogla, prm["wout"], prm["ffn_g"], prm["ffn_win"], prm["ffn_cw"],
            prm["ffn_cb"], prm["ffn_wo"]]
    if final_norm:
        in_specs.append(_const_spec((1, D_MODEL)))
        args.append(final_g)
    return pl.pallas_call(
        functools.partial(_ffn_kernel, final_norm=final_norm),
        grid=(n_seq, nj),
        in_specs=in_specs,
        out_specs=[pl.BlockSpec((tm, D_MODEL), row),
                   pl.BlockSpec((1, 2, D_FF), lambda b, j: (b, 0, 0))],
        out_shape=[jax.ShapeDtypeStruct((n_seq * seq_len, D_MODEL), F32),
                   jax.ShapeDtypeStruct((n_seq, 2, D_FF), F32)],
        scratch_shapes=[pltpu.VMEM((tm, D_MODEL), BF16), pltpu.VMEM((tm, D_MODEL), F32),
                        pltpu.VMEM((8, D_FF), F32)],
        compiler_params=_cparams(("parallel", "arbitrary")),
        name="ffn",
    )(*args)


def _ffn_step_kernel(x_ref, os5_ref, olat_ref, wuv_ref, ogla_ref, wout_ref, gf_ref, b0_ref, b1_ref,
                     wv_ref, wg_ref, cw_ref, cb_ref, wo_ref, *rest, final_norm):
    if final_norm:
        gfin_ref, y_ref, gate_ref, x1_sc, hn_sc, acc_sc = rest
    else:
        y_ref, gate_ref, x1_sc, hn_sc, acc_sc = rest
    c = pl.program_id(0)

    @pl.when(c == 0)
    def _():
        omla = jnp.dot(olat_ref[...].astype(BF16), wuv_ref[...], preferred_element_type=F32)
        x1 = (x_ref[...]
              + jnp.dot(os5_ref[...].astype(BF16), wout_ref[:S5_W], preferred_element_type=F32)
              + jnp.dot(omla.astype(BF16), wout_ref[S5_W:S5_W + MLA_W], preferred_element_type=F32)
              + jnp.dot(ogla_ref[...].astype(BF16), wout_ref[S5_W + MLA_W:],
                        preferred_element_type=F32))
        x1_sc[...] = x1
        hn_sc[...] = _rms(x1, gf_ref[...]).astype(BF16)
        acc_sc[...] = jnp.zeros_like(acc_sc)

    hn = hn_sc[...]
    val = jnp.dot(hn, wv_ref[...], preferred_element_type=F32)
    gate = jnp.dot(hn, wg_ref[...], preferred_element_type=F32)
    cw = cw_ref[...]
    conv = cb_ref[...] + cw[0:1] * b0_ref[...] + cw[1:2] * b1_ref[...] + cw[2:3] * gate
    a = jax.nn.gelu(conv) * val
    acc_sc[...] += jnp.dot(a.astype(BF16), wo_ref[...], preferred_element_type=F32)
    gate_ref[...] = gate

    @pl.when(c == FF_NC - 1)
    def _():
        y = x1_sc[...] + acc_sc[...]
        if final_norm:
            y = _rms(y, gfin_ref[...])
        y_ref[...] = y


def _ffn_step(x2d, os5, olat, ogla, conv_state, prm, layer, *, final_g=None):
    n = x2d.shape[0]
    final_norm = final_g is not None
    lcol = lambda c: (layer, 0, c)
    in_specs = [_const_spec((n, D_MODEL)), _const_spec((n, S5_W)),
                _const_spec((n, MLA_H * MLA_L)), _layer_spec(layer, (MLA_H * MLA_L, MLA_W)),
                _const_spec((n, GLA_W)), _layer_spec(layer, (D_MODEL, D_MODEL)),
                _layer_spec(layer, (1, D_MODEL)),
                pl.BlockSpec((None, n, FF_CHUNK), lcol),
                pl.BlockSpec((None, n, FF_CHUNK), lambda c: (layer, 0, FF_NC + c)),
                pl.BlockSpec((None, D_MODEL, FF_CHUNK), lcol),
                pl.BlockSpec((None, D_MODEL, FF_CHUNK), lambda c: (layer, 0, FF_NC + c)),
                pl.BlockSpec((None, 3, FF_CHUNK), lcol), pl.BlockSpec((None, 1, FF_CHUNK), lcol),
                pl.BlockSpec((None, FF_CHUNK, D_MODEL), lambda c: (layer, c, 0))]
    args = [x2d, os5, olat, prm["mla_wuv_bd"], ogla, prm["wout"], prm["ffn_g"], conv_state, conv_state,
            prm["ffn_win"], prm["ffn_win"], prm["ffn_cw"], prm["ffn_cb"], prm["ffn_wo"]]
    if final_norm:
        in_specs.append(_const_spec((1, D_MODEL)))
        args.append(final_g)
    return pl.pallas_call(
        functools.partial(_ffn_step_kernel, final_norm=final_norm),
        grid=(FF_NC,),
        in_specs=in_specs,
        out_specs=[_const_spec((n, D_MODEL)), pl.BlockSpec((n, FF_CHUNK), lambda c: (0, c))],
        out_shape=[jax.ShapeDtypeStruct((n, D_MODEL), F32), jax.ShapeDtypeStruct((n, D_FF), F32)],
        scratch_shapes=[pltpu.VMEM((n, D_MODEL), F32), pltpu.VMEM((n, D_MODEL), BF16),
                        pltpu.VMEM((n, D_MODEL), F32)],
        compiler_params=_cparams(("arbitrary",)),
        name="ffn_step",
    )(*args)


def _rot_cols(w):
    half = MLA_ROPE // 2
    return jnp.concatenate([-w[..., half:], w[..., :half]], axis=-1)


def _pad_cols(w, width):
    return jnp.pad(w, [(0, 0)] * (w.ndim - 1) + [(0, width - w.shape[-1])])


def _prepare_params(norm_mix_g, w_in, s5_a_re, s5_a_im, s5_log_dt, s5_b_re, s5_b_im, s5_c_re, s5_c_im,
                    s5_d, s5_w_glu, s5_b_glu, mla_q_norm_g, mla_w_qb, mla_kv_norm_g, mla_w_uk, mla_w_uv,
                    gla_w_gate, gla_b_gate, gla_norm_g, w_out, norm_ffn_g, w_ffn_in, ffn_conv_w, ffn_conv_b,
                    w_ffn_out):
    row = lambda v: v[:, None, :]
    o = 0
    cols = {}
    for name, width in (("u", 256), ("cq", 256), ("ckv", 128), ("kpe", 32), ("gq", 128), ("gk", 128),
                        ("gv", 256), ("glow", 16), ("gr", 256)):
        cols[name] = w_in[:, :, o:o + width]
        o += width
    in_w = jnp.concatenate(
        [cols["u"], cols["cq"], cols["ckv"], _pad_cols(cols["kpe"], LANE),
         _pad_cols(_rot_cols(cols["kpe"]), LANE), cols["gq"], cols["gk"], cols["gv"],
         _pad_cols(cols["glow"], LANE), cols["gr"]], axis=2).astype(BF16)

    dt = jnp.exp(s5_log_dt)[:, :, None]
    mag = jnp.exp(s5_a_re * dt)
    ab_re, ab_im = mag * jnp.cos(s5_a_im * dt), mag * jnp.sin(s5_a_im * dt)
    den = s5_a_re * s5_a_re + s5_a_im * s5_a_im
    k_re = ((ab_re - 1.0) * s5_a_re + ab_im * s5_a_im) / den
    k_im = (ab_im * s5_a_re - (ab_re - 1.0) * s5_a_im) / den
    bb_re = k_re[..., None] * s5_b_re - k_im[..., None] * s5_b_im
    bb_im = k_re[..., None] * s5_b_im + k_im[..., None] * s5_b_re
    eye = jnp.eye(S5_G, dtype=F32)
    bd_in = lambda m: jnp.einsum("lgpc,gh->lgchp", m, eye).reshape(DEPTH, S5_W, S5_N)
    bd_out = lambda m: jnp.einsum("lgcp,gh->lgphc", m, eye).reshape(DEPTH, S5_N, S5_W)

    wqb = mla_w_qb.reshape(DEPTH, MLA_QL, MLA_H, MLA_NOPE + MLA_ROPE)
    w_nope = wqb[..., :MLA_NOPE].reshape(DEPTH, MLA_QL, MLA_H * MLA_NOPE)
    w_rope = wqb[..., MLA_NOPE:]
    heads = lambda m: m.reshape(DEPTH, MLA_QL, MLA_H * MLA_ROPE)
    eye_h = jnp.eye(MLA_H, dtype=F32)
    wuk = jnp.einsum("dlhn,hk->dhnkl", mla_w_uk, eye_h).reshape(DEPTH, MLA_H * MLA_NOPE, MLA_H * MLA_L)
    wuv = jnp.einsum("dlhv,hk->dhlkv", mla_w_uv, eye_h).astype(BF16)

    seg = (jnp.arange(GLA_W)[:, None] // GLA_DV == jnp.arange(GLA_W)[None, :] // GLA_DV)
    tri = jnp.arange(GLA_CHUNK)[None, :] <= jnp.arange(GLA_CHUNK)[:, None]
    return dict(
        in_g=row(norm_mix_g), in_w=in_w,
        s5_a=jnp.stack([ab_re.reshape(DEPTH, S5_N), ab_im.reshape(DEPTH, S5_N)], axis=1),
        s5_bm=jnp.concatenate([bd_in(bb_re), bd_in(bb_im)], axis=2).astype(BF16),
        s5_cm=jnp.concatenate([bd_out(s5_c_re), -bd_out(s5_c_im)], axis=1).astype(BF16),
        s5_d=row(s5_d), s5_wglu=s5_w_glu.astype(BF16), s5_bglu=row(s5_b_glu),
        mla_gq=row(mla_q_norm_g), mla_gkv=row(mla_kv_norm_g),
        mla_wq=jnp.concatenate([w_nope, heads(w_rope), heads(_rot_cols(w_rope))], axis=2).astype(BF16),
        mla_wuk=wuk.astype(BF16),
        mla_wuv=wuv.reshape(DEPTH, MLA_H, MLA_L, MLA_W),
        mla_wuv_bd=wuv.reshape(DEPTH, MLA_H * MLA_L, MLA_W),
        gla_wgate=jnp.pad(gla_w_gate, ((0, 0), (0, LANE - GLA_R), (0, 0))).astype(BF16),
        gla_bgate=row(gla_b_gate), gla_g=row(gla_norm_g),
        gla_seg=seg.astype(F32) / GLA_DV, gla_tri=tri.astype(F32),
        wout=w_out.astype(BF16), ffn_g=row(norm_ffn_g), ffn_win=w_ffn_in.astype(BF16),
        ffn_cw=ffn_conv_w, ffn_cb=row(ffn_conv_b), ffn_wo=w_ffn_out.astype(BF16))


def _rope_tables(pos):
    half = MLA_ROPE // 2
    inv = ROPE_THETA ** (-jnp.arange(half, dtype=F32) / half)
    ang = pos.astype(F32)[:, None] * inv[None, :]
    reps = 2 * LANE // MLA_ROPE
    return jnp.tile(jnp.cos(ang), (1, reps)), jnp.tile(jnp.sin(ang), (1, reps))


def _s5_state_out(hfin):
    shape = hfin.shape[:2] + (S5_G, S5_P)
    return jnp.stack([hfin[..., :S5_N].reshape(shape), hfin[..., S5_N:].reshape(shape)], axis=-1)


def kernel(x_prompt, x_sample, cache_mla_ckv, cache_mla_krope, page_table, state_s5, state_gla, state_ffn_conv, norm_mix_g, w_in, s5_a_re, s5_a_im, s5_log_dt, s5_b_re, s5_b_im, s5_c_re, s5_c_im, s5_d, s5_w_glu, s5_b_glu, mla_q_norm_g, mla_w_qb, mla_kv_norm_g, mla_w_uk, mla_w_uv, gla_w_gate, gla_b_gate, gla_norm_g, w_out, norm_ffn_g, w_ffn_in, ffn_conv_w, ffn_conv_b, w_ffn_out, norm_final_g):
    bp, sp = x_prompt.shape[:2]
    bs = x_sample.shape[0]
    n_pages = page_table.shape[1]
    past_len = n_pages * cache_mla_ckv.shape[2]
    kw = GLA_H * GLA_DK

    prm = _prepare_params(norm_mix_g, w_in, s5_a_re, s5_a_im, s5_log_dt, s5_b_re, s5_b_im, s5_c_re, s5_c_im,
                          s5_d, s5_w_glu, s5_b_glu, mla_q_norm_g, mla_w_qb, mla_kv_norm_g, mla_w_uk,
                          mla_w_uv, gla_w_gate, gla_b_gate, gla_norm_g, w_out, norm_ffn_g, w_ffn_in,
                          ffn_conv_w, ffn_conv_b, w_ffn_out)
    cos_p, sin_p = _rope_tables(jnp.arange(sp))
    cos_s, sin_s = _rope_tables(jnp.full((bs,), past_len))
    gfin = norm_final_g[None]
    krope_t = jnp.swapaxes(cache_mla_krope, 2, 3)
    h0_all = jnp.concatenate([state_s5[..., 0].reshape(DEPTH, bs, S5_N),
                              state_s5[..., 1].reshape(DEPTH, bs, S5_N)], axis=2)
    gla_all = state_gla.reshape(DEPTH, bs, kw, GLA_DV)
    conv_all = state_ffn_conv.reshape(DEPTH, bs, 2 * D_FF)
    h0_zero = jnp.zeros((bp, 2 * S5_N), F32)
    xp = x_prompt.reshape(bp * sp, D_MODEL)
    xd = x_sample.reshape(bs, D_MODEL)
    outs = [[] for _ in range(10)]

    for l in range(DEPTH):
        final_g = gfin if l == DEPTH - 1 else None

        u, pm, pg = _in_proj(xp, prm, l, tm=512)
        o_s5, hfin = _s5(u, h0_zero, prm, l, r_rows=bp, t_total=sp, t_steps=128)
        qcat, kcat, ckv, kpe = _mla_prep(pm, cos_p, sin_p, prm, l, n_seq=bp, seq_len=sp, tm=512,
                                         q_dtype=BF16)
        o_mla = _attn_prompt(qcat, kcat, prm, l, n_seq=bp, seq_len=sp)
        o_gla, st = _gla_prompt(pg, prm, l, n_seq=bp, seq_len=sp)
        xp, cbuf = _ffn_prompt(xp, o_s5, o_mla, o_gla, prm, l, n_seq=bp, seq_len=sp, final_g=final_g)
        for k, v in zip(range(5), (ckv, kpe, hfin, st, cbuf)):
            outs[k].append(v)

        u, pm, pg = _in_proj(xd, prm, l, tm=bs)
        o_s5, hfin = _s5(u, h0_all[l], prm, l, r_rows=bs, t_total=1, t_steps=1)
        qcat, kcat, ckv, kpe = _mla_prep(pm, cos_s, sin_s, prm, l, n_seq=1, seq_len=bs, tm=bs, q_dtype=F32)
        o_lat = _attn_decode(page_table, qcat.transpose(1, 0, 2), kcat[:, None, :], cache_mla_ckv,
                             krope_t, layer=l)
        o_gla, st = _gla_step(pg, gla_all, prm, l)
        xd, gate = _ffn_step(xd, o_s5, o_lat.reshape(bs, MLA_H * MLA_L), o_gla, conv_all, prm, l,
                             final_g=final_g)
        for k, v in zip(range(5, 10), (ckv, kpe, hfin, st, gate)):
            outs[k].append(v)

    outs = [jnp.stack(o) for o in outs]
    st5 = outs[3].reshape(DEPTH, bp, GLA_H, GLA_DV, GLA_H, GLA_DK)
    gla_p = jnp.stack([st5[:, :, h, :, h, :] for h in range(GLA_H)], axis=2).swapaxes(3, 4)
    conv_s = jnp.stack([state_ffn_conv[:, :, 1], outs[9]], axis=2)
    return (xp.reshape(bp, sp, D_MODEL), xd.reshape(bs, 1, D_MODEL),
            outs[0].reshape(DEPTH, bp, sp // PAGE, PAGE, MLA_L),
            outs[1].reshape(DEPTH, bp, sp // PAGE, PAGE, MLA_ROPE),
            _s5_state_out(outs[2]), gla_p, outs[4],
            outs[5][:, :, None, :], outs[6][:, :, None, :], _s5_state_out(outs[7]),
            outs[8].reshape(DEPTH, bs, GLA_H, GLA_DK, GLA_DV), conv_s)
```
